```python
import math
import jax, jax.numpy as jnp
from jax import lax
import numpy as np

D_MODEL = 2048
BATCH = 4
SEQ = 2048
DEPTH = 1
DEC_BATCH = 128
DEC_SEQ = 4
PAST_LEN = 16384
PAGE_SIZE = 128

GDN_HEADS = 8
GDN_DK = 128
GDN_DV = 128
GDN_WIDTH = GDN_HEADS * GDN_DV
CONV_WIDTH = 4
CONV_CH = GDN_HEADS * (2 * GDN_DK + GDN_DV)
RET_HEADS = 8
RET_DK = 128
RET_DV = 128
RET_QK = RET_HEADS * RET_DK
RET_WIDTH = RET_HEADS * RET_DV
MIX_WIDTH = GDN_WIDTH + RET_WIDTH
N_IN = CONV_CH + GDN_WIDTH + 2 * GDN_HEADS + 2 * RET_QK + 2 * RET_WIDTH
A_COL_START = CONV_CH + GDN_WIDTH + GDN_HEADS
DECAY_PROJ_SCALE = 0.1
D_FF = 5632
MACARON_WEIGHT = 0.5
N_ADA = 9
CHUNK = 64
ROPE_BASE = 10000.0
NORM_EPS = 1e-6
L2_EPS = 1e-6

kernel_name = "hymba_gdn_retention_macaron_adaln_step"


def rms_norm(x, gain):
    xf = x.astype(jnp.float32)
    y = xf * lax.rsqrt(jnp.mean(xf * xf, axis=-1, keepdims=True) + NORM_EPS)
    return (y * gain.astype(jnp.float32)).astype(x.dtype)


def modulate(h, shift, scale):
    return h * (1.0 + scale[:, None, :]) + shift[:, None, :]


def swiglu(h, w_gate, w_up, w_down):
    return (jax.nn.silu(h @ w_gate) * (h @ w_up)) @ w_down


def l2_normalize(x):
    return x * lax.rsqrt(jnp.sum(x * x, axis=-1, keepdims=True) + L2_EPS)


def causal_short_conv(x, buf, w):
    t = x.shape[1]
    xp = jnp.concatenate([buf.astype(x.dtype), x], axis=1)
    out = xp[:, 0:t] * w[0]
    for i in range(1, CONV_WIDTH):
        out = out + xp[:, i:i + t] * w[i]
    return jax.nn.silu(out), xp[:, t:]


def rotary(x, pos):
    half = x.shape[-1] // 2
    inv_freq = ROPE_BASE ** (-jnp.arange(half, dtype=jnp.float32) / half)
    ang = pos.astype(jnp.float32)[:, None] * inv_freq[None, :]
    cos = jnp.cos(ang)[None, :, None, :]
    sin = jnp.sin(ang)[None, :, None, :]
    x1, x2 = x[..., :half], x[..., half:]
    return jnp.concatenate([x1 * cos - x2 * sin, x1 * sin + x2 * cos], axis=-1)


def to_chunks(a, n, c):
    b, _, h = a.shape[:3]
    a = a.reshape((b, n, c, h) + a.shape[3:])
    return jnp.moveaxis(jnp.swapaxes(a, 2, 3), 1, 0)


def from_chunks(o):
    n, b, h, c, d = o.shape
    return jnp.transpose(o, (1, 0, 3, 2, 4)).reshape(b, n * c, h, d)


def gated_delta_rule(q, k, v, beta, g, s0):
    t = q.shape[1]
    c = min(CHUNK, t)
    n = t // c
    dk = q.shape[-1]
    dv = v.shape[-1]
    q = to_chunks(q, n, c) * (dk ** -0.5)
    k = to_chunks(k, n, c)
    v = to_chunks(v, n, c)
    beta = to_chunks(beta, n, c)
    gc = jnp.cumsum(to_chunks(g, n, c), axis=-1)
    causal = jnp.tril(jnp.ones((c, c), dtype=bool))
    strict = jnp.tril(jnp.ones((c, c), dtype=bool), -1)
    decay = jnp.exp(jnp.where(causal, gc[..., :, None] - gc[..., None, :], -jnp.inf))
    kb = k * beta[..., None]
    a_mat = jnp.where(strict, jnp.einsum('nbhid,nbhjd->nbhij', kb, k) * decay, 0.0) + jnp.eye(c, dtype=jnp.float32)
    rhs = jnp.concatenate([v * beta[..., None], kb * jnp.exp(gc)[..., None]], axis=-1)
    sol = lax.linalg.triangular_solve(a_mat, rhs, left_side=True, lower=True, unit_diagonal=True)
    u, w = sol[..., :dv], sol[..., dv:]
    qk = jnp.einsum('nbhid,nbhjd->nbhij', q, k) * decay
    q_dec = q * jnp.exp(gc)[..., None]
    k_dec = k * jnp.exp(gc[..., -1:] - gc)[..., None]
    g_last = jnp.exp(gc[..., -1])[..., None, None]

    def step(s, inp):
        qk_i, qd_i, kd_i, u_i, w_i, gl_i = inp
        v_new = u_i - jnp.einsum('bhcd,bhde->bhce', w_i, s)
        o = jnp.einsum('bhcd,bhde->bhce', qd_i, s) + jnp.einsum('bhij,bhje->bhie', qk_i, v_new)
        s = s * gl_i + jnp.einsum('bhcd,bhce->bhde', kd_i, v_new)
        return s, o

    s_final, o = lax.scan(step, s0, (qk, q_dec, k_dec, u, w, g_last))
    return from_chunks(o), s_final


def multiscale_retention(q, k, v, s0):
    t = q.shape[1]
    c = min(CHUNK, t)
    n = t // c
    h = q.shape[2]
    dk = q.shape[-1]
    q = to_chunks(q, n, c)
    k = to_chunks(k, n, c) * (dk ** -0.5)
    v = to_chunks(v, n, c)
    log_gamma = jnp.log(1.0 - 2.0 ** (-5.0 - jnp.arange(h, dtype=jnp.float32)))
    idx = jnp.arange(c, dtype=jnp.float32)
    rel = idx[:, None] - idx[None, :]
    dmat = jnp.where(rel >= 0, jnp.exp(jnp.maximum(rel, 0.0) * log_gamma[:, None, None]), 0.0)
    inner = jnp.einsum('nbhid,nbhjd->nbhij', q, k) * dmat
    q_dec = q * jnp.exp((idx + 1.0) * log_gamma[:, None])[..., None]
    k_dec = k * jnp.exp((c - 1.0 - idx) * log_gamma[:, None])[..., None]
    g_chunk = jnp.exp(c * log_gamma)[:, None, None]

    def step(s, inp):
        in_i, qd_i, kd_i, v_i = inp
        o = jnp.einsum('bhcd,bhde->bhce', qd_i, s) + jnp.einsum('bhij,bhje->bhie', in_i, v_i)
        s = s * g_chunk + jnp.einsum('bhcd,bhce->bhde', kd_i, v_i)
        return s, o

    s_final, o = lax.scan(step, s0, (inner, q_dec, k_dec, v))
    return from_chunks(o), s_final


def token_mixer(h, pos, conv_buf, s_gdn, s_ret, w_in, conv_w, a_log, dt_bias,
                gdn_norm_w, ret_gn_w, ret_gn_b, w_out):
    b, t, _ = h.shape
    f32 = jnp.float32
    proj = (h @ w_in).astype(f32)
    o1 = CONV_CH
    o2 = o1 + GDN_WIDTH
    o3 = o2 + GDN_HEADS
    o4 = o3 + GDN_HEADS
    o5 = o4 + RET_QK
    o6 = o5 + RET_QK
    o7 = o6 + RET_WIDTH
    qkv, z, beta_raw, a_raw, rq, rk, rv, rg = jnp.split(proj, [o1, o2, o3, o4, o5, o6, o7], axis=-1)

    qkv, conv_new = causal_short_conv(qkv, conv_buf.astype(f32), conv_w.astype(f32))
    gq, gk, gv = jnp.split(qkv, [GDN_HEADS * GDN_DK, 2 * GDN_HEADS * GDN_DK], axis=-1)
    gq = l2_normalize(gq.reshape(b, t, GDN_HEADS, GDN_DK))
    gk = l2_normalize(gk.reshape(b, t, GDN_HEADS, GDN_DK))
    gv = gv.reshape(b, t, GDN_HEADS, GDN_DV)
    beta = jax.nn.sigmoid(beta_raw)
    g = -jnp.exp(a_log.astype(f32)) * jax.nn.softplus(a_raw + dt_bias.astype(f32))
    o_gdn, s_gdn_new = gated_delta_rule(gq, gk, gv, beta, g, s_gdn.astype(f32))
    o_gdn = o_gdn * lax.rsqrt(jnp.mean(o_gdn * o_gdn, axis=-1, keepdims=True) + NORM_EPS) * gdn_norm_w.astype(f32)
    o_gdn = o_gdn.reshape(b, t, GDN_WIDTH) * jax.nn.silu(z)

    rq = rotary(rq.reshape(b, t, RET_HEADS, RET_DK), pos)
    rk = rotary(rk.reshape(b, t, RET_HEADS, RET_DK), pos)
    rv = rv.reshape(b, t, RET_HEADS, RET_DV)
    o_ret, s_ret_new = multiscale_retention(rq, rk, rv, s_ret.astype(f32))
    mu = jnp.mean(o_ret, axis=-1, keepdims=True)
    var = jnp.mean(jnp.square(o_ret - mu), axis=-1, keepdims=True)
    o_ret = ((o_ret - mu) * lax.rsqrt(var + NORM_EPS)).reshape(b, t, RET_WIDTH)
    o_ret = (o_ret * ret_gn_w.astype(f32) + ret_gn_b.astype(f32)) * jax.nn.silu(rg)

    mixed = jnp.concatenate([o_gdn, o_ret], axis=-1).astype(h.dtype) @ w_out
    return mixed, s_gdn_new, conv_new, s_ret_new


def decoder_layer(x, c, pos, s_gdn, s_conv, s_ret, w_ada, b_ada, norm_ffn1, w1_gate, w1_up, w1_down,
                  norm_mix, w_in, conv_w, a_log, dt_bias, gdn_norm_w, ret_gn_w, ret_gn_b, w_out,
                  norm_ffn2, w2_gate, w2_up, w2_down):
    ada = jax.nn.silu(c) @ w_ada + b_ada
    sh1, sc1, g1, sh2, sc2, g2, sh3, sc3, g3 = jnp.split(ada, N_ADA, axis=-1)
    h = modulate(rms_norm(x, norm_ffn1), sh1, sc1)
    x = x + MACARON_WEIGHT * g1[:, None, :] * swiglu(h, w1_gate, w1_up, w1_down)
    h = modulate(rms_norm(x, norm_mix), sh2, sc2)
    mixed, s_gdn, s_conv, s_ret = token_mixer(h, pos, s_conv, s_gdn, s_ret, w_in, conv_w, a_log, dt_bias,
                                              gdn_norm_w, ret_gn_w, ret_gn_b, w_out)
    x = x + g2[:, None, :] * mixed
    h = modulate(rms_norm(x, norm_ffn2), sh3, sc3)
    x = x + MACARON_WEIGHT * g3[:, None, :] * swiglu(h, w2_gate, w2_up, w2_down)
    return x, s_gdn, s_conv, s_ret


def trunk(x, c, pos, s_gdn, s_conv, s_ret, layer_weights, w_ada_final, b_ada_final, norm_final):
    new_gdn, new_conv, new_ret = [], [], []
    for layer in range(DEPTH):
        x, sg, sc, sr = decoder_layer(x, c, pos, s_gdn[layer], s_conv[layer], s_ret[layer],
                                      *[w[layer] for w in layer_weights])
        new_gdn.append(sg.astype(s_gdn.dtype))
        new_conv.append(sc.astype(s_conv.dtype))
        new_ret.append(sr.astype(s_ret.dtype))
    shift, scale = jnp.split(jax.nn.silu(c) @ w_ada_final + b_ada_final, 2, axis=-1)
    y = modulate(rms_norm(x, norm_final), shift, scale)
    return y, jnp.stack(new_gdn), jnp.stack(new_conv), jnp.stack(new_ret)


def setup_inputs(seed: int = 0) -> dict:
    key = jax.random.key(seed)
    ks = jax.random.split(key, 32)
    f32 = jnp.float32

    def nrm(k, shape, scale):
        return jax.random.normal(k, shape, f32) * scale

    col_scale = jnp.ones((N_IN,), f32).at[A_COL_START:A_COL_START + GDN_HEADS].set(DECAY_PROJ_SCALE)
    dt = jnp.exp(jax.random.uniform(ks[17], (DEPTH, GDN_HEADS), f32, math.log(1e-3), math.log(1e-1)))
    return {
        "x_prompt": nrm(ks[0], (BATCH, SEQ, D_MODEL), 1.0),
        "x_sample": nrm(ks[1], (DEC_BATCH, DEC_SEQ, D_MODEL), 1.0),
        "state_gdn": nrm(ks[2], (DEPTH, DEC_BATCH, GDN_HEADS, GDN_DK, GDN_DV), 0.5),
        "state_conv": nrm(ks[3], (DEPTH, DEC_BATCH, CONV_WIDTH - 1, CONV_CH), 1.0),
        "state_ret": nrm(ks[4], (DEPTH, DEC_BATCH, RET_HEADS, RET_DK, RET_DV), 0.5),
        "c_prompt": nrm(ks[5], (BATCH, D_MODEL), 1.0),
        "c_sample": nrm(ks[6], (DEC_BATCH, D_MODEL), 1.0),
        "w_ada": nrm(ks[7], (DEPTH, D_MODEL, N_ADA * D_MODEL), 0.5 * D_MODEL ** -0.5),
        "b_ada": nrm(ks[8], (DEPTH, N_ADA * D_MODEL), 0.01),
        "norm_ffn1": 1.0 + nrm(ks[9], (DEPTH, D_MODEL), 0.02),
        "w1_gate": nrm(ks[10], (DEPTH, D_MODEL, D_FF), D_MODEL ** -0.5),
        "w1_up": nrm(ks[11], (DEPTH, D_MODEL, D_FF), D_MODEL ** -0.5),
        "w1_down": nrm(ks[12], (DEPTH, D_FF, D_MODEL), D_FF ** -0.5),
        "norm_mix": 1.0 + nrm(ks[13], (DEPTH, D_MODEL), 0.02),
        "w_in": nrm(ks[14], (DEPTH, D_MODEL, N_IN), D_MODEL ** -0.5) * col_scale,
        "conv_w": nrm(ks[15], (DEPTH, CONV_WIDTH, CONV_CH), CONV_WIDTH ** -0.5),
        "a_log": jnp.log(jax.random.uniform(ks[16], (DEPTH, GDN_HEADS), f32, 1.0, 16.0)),
        "dt_bias": dt + jnp.log(-jnp.expm1(-dt)),
        "gdn_norm_w": 1.0 + nrm(ks[18], (DEPTH, GDN_DV), 0.02),
        "ret_gn_w": 1.0 + nrm(ks[19], (DEPTH, RET_WIDTH), 0.02),
        "ret_gn_b": nrm(ks[20], (DEPTH, RET_WIDTH), 0.01),
        "w_out": nrm(ks[21], (DEPTH, MIX_WIDTH, D_MODEL), MIX_WIDTH ** -0.5),
        "norm_ffn2": 1.0 + nrm(ks[22], (DEPTH, D_MODEL), 0.02),
        "w2_gate": nrm(ks[23], (DEPTH, D_MODEL, D_FF), D_MODEL ** -0.5),
        "w2_up": nrm(ks[24], (DEPTH, D_MODEL, D_FF), D_MODEL ** -0.5),
        "w2_down": nrm(ks[25], (DEPTH, D_FF, D_MODEL), D_FF ** -0.5),
        "w_ada_final": nrm(ks[26], (D_MODEL, 2 * D_MODEL), 0.5 * D_MODEL ** -0.5),
        "b_ada_final": nrm(ks[27], (2 * D_MODEL,), 0.01),
        "norm_final": 1.0 + nrm(ks[28], (D_MODEL,), 0.02),
    }


def reference(x_prompt, x_sample, state_gdn, state_conv, state_ret, c_prompt, c_sample,
              w_ada, b_ada, norm_ffn1, w1_gate, w1_up, w1_down, norm_mix, w_in, conv_w, a_log, dt_bias,
              gdn_norm_w, ret_gn_w, ret_gn_b, w_out, norm_ffn2, w2_gate, w2_up, w2_down,
              w_ada_final, b_ada_final, norm_final):
    layer_weights = (w_ada, b_ada, norm_ffn1, w1_gate, w1_up, w1_down, norm_mix, w_in, conv_w, a_log,
                     dt_bias, gdn_norm_w, ret_gn_w, ret_gn_b, w_out, norm_ffn2, w2_gate, w2_up, w2_down)
    bp, tp = x_prompt.shape[0], x_prompt.shape[1]
    ts = x_sample.shape[1]
    pos_prompt = jnp.arange(tp, dtype=jnp.int32)
    pos_sample = PAST_LEN + jnp.arange(ts, dtype=jnp.int32)
    zero_gdn = jnp.zeros((DEPTH, bp, GDN_HEADS, GDN_DK, GDN_DV), state_gdn.dtype)
    zero_conv = jnp.zeros((DEPTH, bp, CONV_WIDTH - 1, CONV_CH), state_conv.dtype)
    zero_ret = jnp.zeros((DEPTH, bp, RET_HEADS, RET_DK, RET_DV), state_ret.dtype)
    y_prompt, gdn_p, conv_p, ret_p = trunk(x_prompt, c_prompt, pos_prompt, zero_gdn, zero_conv, zero_ret,
                                           layer_weights, w_ada_final, b_ada_final, norm_final)
    y_sample, gdn_s, conv_s, ret_s = trunk(x_sample, c_sample, pos_sample, state_gdn, state_conv, state_ret,
                                           layer_weights, w_ada_final, b_ada_final, norm_final)
    return (y_prompt, y_sample, gdn_p, conv_p, ret_p, gdn_s, conv_s, ret_s)
```

```python
import functools
import math

import jax
import jax.numpy as jnp
from jax import lax
from jax.experimental import pallas as pl
from jax.experimental.pallas import tpu as pltpu

F32 = jnp.float32
BF16 = jnp.bfloat16

LANES = 128
SUBLANES = 8
VMEM_LIMIT_BYTES = 56 * 1024 * 1024

PAST_LEN = 16384
CONV_WIDTH = 4
CHUNK = 64
ROPE_BASE = 10000.0
NORM_EPS = 1e-6
L2_EPS = 1e-6
MACARON_WEIGHT = 0.5
N_ADA = 9
ROW_TILE = 128


def _sigmoid(x):
    return 1.0 / (1.0 + jnp.exp(-x))


def _silu(x):
    return x * _sigmoid(x)


def _softplus(x):
    return jnp.maximum(x, 0.0) + jnp.log(1.0 + jnp.exp(-jnp.abs(x)))


def _dot(a, b):
    return jnp.dot(a, b, preferred_element_type=F32)


def _dot_nt(a, b):
    return lax.dot_general(a, b, (((1,), (1,)), ((), ())), preferred_element_type=F32)


def _dot_tn(a, b):
    return lax.dot_general(a, b, (((0,), (0,)), ((), ())), preferred_element_type=F32)


def _split3(x):
    hi = x.astype(BF16)
    r = x - hi.astype(F32)
    mid = r.astype(BF16)
    lo = (r - mid.astype(F32)).astype(BF16)
    return hi, mid, lo


def _dot_x3(a, b):
    a_hi = a.astype(BF16)
    a_lo = (a - a_hi.astype(F32)).astype(BF16)
    b_hi = b.astype(BF16)
    b_lo = (b - b_hi.astype(F32)).astype(BF16)
    return _dot(a_hi, b_hi) + (_dot(a_hi, b_lo) + _dot(a_lo, b_hi))


def _rms_mod(x, gain, shift, scale):
    y = x * lax.rsqrt(jnp.mean(x * x, axis=-1, keepdims=True) + NORM_EPS)
    return (y * gain) * (1.0 + scale) + shift


def _params(*sem):
    return pltpu.CompilerParams(dimension_semantics=sem, vmem_limit_bytes=VMEM_LIMIT_BYTES)


def _ada_kernel(c_ref, w_ref, b_ref, o_ref):
    a = _silu(c_ref[...]).astype(BF16)
    o_ref[...] = _dot(a, w_ref[...].astype(BF16)) + b_ref[...]


def _ada_proj(c, w, b, tn):
    m, d = c.shape
    n = w.shape[1]
    return pl.pallas_call(
        _ada_kernel,
        grid=(n // tn,),
        in_specs=[
            pl.BlockSpec((m, d), lambda j: (0, 0)),
            pl.BlockSpec((d, tn), lambda j: (0, j)),
            pl.BlockSpec((1, tn), lambda j: (0, j)),
        ],
        out_specs=pl.BlockSpec((m, tn), lambda j: (0, j)),
        out_shape=jax.ShapeDtypeStruct((m, n), F32),
        compiler_params=_params("parallel"),
        name="ada_proj",
    )(c, w, b.reshape(1, n))


def _ffn_kernel(x_ref, sh_ref, sc_ref, gt_ref, gain_ref, wg_ref, wu_ref, wd_ref,
                gain2_ref, sh2_ref, sc2_ref, *rest, emit_x):
    if emit_x:
        xo_ref, hn_ref, h_scr, acc_scr = rest
    else:
        hn_ref, h_scr, acc_scr = rest
    j = pl.program_id(1)
    tm = x_ref.shape[0]
    n_tiles = tm // ROW_TILE

    @pl.when(j == 0)
    def _():
        def body(r, carry):
            rows = pl.ds(pl.multiple_of(r * ROW_TILE, ROW_TILE), ROW_TILE)
            h = _rms_mod(x_ref[rows, :], gain_ref[...], sh_ref[...], sc_ref[...])
            h_scr[rows, :] = h.astype(BF16)
            return carry
        lax.fori_loop(0, n_tiles, body, 0)

    h = h_scr[...]
    g = _dot(h, wg_ref[...].astype(BF16))
    u = _dot(h, wu_ref[...].astype(BF16))
    a = (_silu(g) * u).astype(BF16)
    d = _dot(a, wd_ref[...].astype(BF16))

    @pl.when(j == 0)
    def _():
        acc_scr[...] = d

    @pl.when(j > 0)
    def _():
        acc_scr[...] += d

    @pl.when(j == pl.num_programs(1) - 1)
    def _():
        def body(r, carry):
            rows = pl.ds(pl.multiple_of(r * ROW_TILE, ROW_TILE), ROW_TILE)
            xo = x_ref[rows, :] + (MACARON_WEIGHT * gt_ref[...]) * acc_scr[rows, :]
            if emit_x:
                xo_ref[rows, :] = xo
            hn = _rms_mod(xo, gain2_ref[...], sh2_ref[...], sc2_ref[...])
            hn_ref[rows, :] = hn.astype(hn_ref.dtype)
            return carry
        lax.fori_loop(0, n_tiles, body, 0)


def _mod_spec(mod, k, d, rows_per_group, tm):
    if mod.ndim == 3:
        blocks_per_group = rows_per_group // tm
        return pl.BlockSpec((None, 1, d), lambda i, j: (i // blocks_per_group, 0, k))
    return pl.BlockSpec((ROW_TILE, d), lambda i, j: (0, k))


def _ffn(x, mod, ks, gain, wg, wu, wd, mod2, ks2, gain2, *, rows_per_group, tm, tf, emit_x, hn_dtype):
    m, d = x.shape
    f = wg.shape[1]
    k_sh, k_sc, k_gt = ks
    k_sh2, k_sc2 = ks2
    vec = lambda: pl.BlockSpec((1, d), lambda i, j: (0, 0))
    row_block = lambda: pl.BlockSpec((tm, d), lambda i, j: (i, 0))
    out_shape = [jax.ShapeDtypeStruct((m, d), hn_dtype)]
    out_specs = [row_block()]
    if emit_x:
        out_shape = [jax.ShapeDtypeStruct((m, d), F32)] + out_shape
        out_specs = [row_block()] + out_specs
    return pl.pallas_call(
        functools.partial(_ffn_kernel, emit_x=emit_x),
        grid=(m // tm, f // tf),
        in_specs=[
            row_block(),
            _mod_spec(mod, k_sh, d, rows_per_group, tm),
            _mod_spec(mod, k_sc, d, rows_per_group, tm),
            _mod_spec(mod, k_gt, d, rows_per_group, tm),
            vec(),
            pl.BlockSpec((d, tf), lambda i, j: (0, j)),
            pl.BlockSpec((d, tf), lambda i, j: (0, j)),
            pl.BlockSpec((tf, d), lambda i, j: (j, 0)),
            vec(),
            _mod_spec(mod2, k_sh2, d, rows_per_group, tm),
            _mod_spec(mod2, k_sc2, d, rows_per_group, tm),
        ],
        out_specs=out_specs,
        out_shape=out_shape,
        scratch_shapes=[pltpu.VMEM((tm, d), BF16), pltpu.VMEM((tm, d), F32)],
        compiler_params=_params("parallel", "arbitrary"),
        name="ffn",
    )(x, mod, mod, mod, gain.reshape(1, d), wg, wu, wd, gain2.reshape(1, d), mod2, mod2)


def _proj_kernel(h_ref, w_ref, o_ref):
    o_ref[...] = _dot(h_ref[...], w_ref[...].astype(BF16))


def _proj(h, w, tm, tn):
    m, d = h.shape
    n = w.shape[1]
    return pl.pallas_call(
        _proj_kernel,
        grid=(m // tm, n // tn),
        in_specs=[
            pl.BlockSpec((tm, d), lambda i, j: (i, 0)),
            pl.BlockSpec((d, tn), lambda i, j: (0, j)),
        ],
        out_specs=pl.BlockSpec((tm, tn), lambda i, j: (i, j)),
        out_shape=jax.ShapeDtypeStruct((m, n), F32),
        compiler_params=_params("parallel", "arbitrary"),
        name="in_proj",
    )(h, w)


def _outproj_kernel(x_ref, gt_ref, og_ref, or_ref, wa_ref, wb_ref, o_ref):
    mixed = _dot(og_ref[...], wa_ref[...].astype(BF16)) + _dot(or_ref[...], wb_ref[...].astype(BF16))
    tm = x_ref.shape[0]
    for r in range(tm // ROW_TILE):
        rows = slice(r * ROW_TILE, (r + 1) * ROW_TILE)
        o_ref[rows, :] = x_ref[rows, :] + gt_ref[...] * mixed[rows, :]


def _outproj(x, mod, k_gt, og, orr, w_out, *, rows_per_group, tm, tn):
    m, d = x.shape
    kw = og.shape[1]
    if mod.ndim == 3:
        blocks_per_group = rows_per_group // tm
        gt_spec = pl.BlockSpec((None, 1, tn), lambda i, j: (i // blocks_per_group, 0, k_gt * (d // tn) + j))
    else:
        gt_spec = pl.BlockSpec((ROW_TILE, tn), lambda i, j: (0, k_gt * (d // tn) + j))
    return pl.pallas_call(
        _outproj_kernel,
        grid=(m // tm, d // tn),
        in_specs=[
            pl.BlockSpec((tm, tn), lambda i, j: (i, j)),
            gt_spec,
            pl.BlockSpec((tm, kw), lambda i, j: (i, 0)),
            pl.BlockSpec((tm, kw), lambda i, j: (i, 0)),
            pl.BlockSpec((kw, tn), lambda i, j: (0, j)),
            pl.BlockSpec((kw, tn), lambda i, j: (1, j)),
        ],
        out_specs=pl.BlockSpec((tm, tn), lambda i, j: (i, j)),
        out_shape=jax.ShapeDtypeStruct((m, d), F32),
        compiler_params=_params("parallel", "arbitrary"),
        name="out_proj",
    )(x, mod, og, orr, w_out, w_out)


def _unit_lower_solve(a, rhs, n_rows):
    sol = rhs - _dot_x3(a, rhs)
    p = a
    power = 2
    while power < n_rows:
        p = _dot_x3(p, p)
        sol = sol + _dot_x3(p, sol)
        power *= 2
    return sol


def _gdn_kernel(qkv_ref, z_ref, ba_ref, cinit_ref, s0_ref, convw_ref, alog_ref, dtb_ref, nw_ref,
                o_ref, snew_ref, cnew_ref, buf_scr, s_scr, *, valid_rows, heads, dk, dv):
    n = pl.program_id(1)
    c = qkv_ref.shape[0]
    pad = SUBLANES

    @pl.when(n == 0)
    def _():
        buf_scr[0:pad, :] = cinit_ref[...]
        s_scr[...] = s0_ref[...]

    buf_scr[pad:pad + c, :] = qkv_ref[...]
    w = convw_ref[...]
    first = pad - (CONV_WIDTH - 1)
    acc = buf_scr[first:first + c, :] * w[0:1, :]
    for i in range(1, CONV_WIDTH):
        acc = acc + buf_scr[first + i:first + i + c, :] * w[i:i + 1, :]
    qkv = _silu(acc)

    @pl.when(n == pl.num_programs(1) - 1)
    def _():
        cnew_ref[...] = buf_scr[valid_rows:valid_rows + pad, :]

    buf_scr[0:pad, :] = buf_scr[c:c + pad, :]

    ba = ba_ref[...]
    beta_all = _sigmoid(ba)
    g_all = -jnp.exp(alog_ref[...]) * _softplus(ba + dtb_ref[...])
    row = lax.broadcasted_iota(jnp.int32, (c, c), 0)
    col = lax.broadcasted_iota(jnp.int32, (c, c), 1)
    if valid_rows < c:
        live = lax.broadcasted_iota(jnp.int32, ba.shape, 0) < valid_rows
        beta_all = jnp.where(live, beta_all, 0.0)
        g_all = jnp.where(live, g_all, 0.0)
    tril = jnp.where(row >= col, 1.0, 0.0).astype(BF16)
    g_hi, g_mid, g_lo = _split3(g_all)
    gc_all = _dot(tril, g_hi) + (_dot(tril, g_mid) + _dot(tril, g_lo))

    causal = row >= col
    strict = row > col
    diag = row == col
    nw = nw_ref[...]
    for h in range(heads):
        beta = beta_all[:, h:h + 1]
        gc = gc_all[:, heads + h:heads + h + 1]
        q = qkv[:, h * dk:(h + 1) * dk]
        k = qkv[:, heads * dk + h * dk:heads * dk + (h + 1) * dk]
        v = qkv[:, 2 * heads * dk + h * dv:2 * heads * dk + (h + 1) * dv]
        q = q * lax.rsqrt(jnp.sum(q * q, axis=-1, keepdims=True) + L2_EPS) * (dk ** -0.5)
        k = k * lax.rsqrt(jnp.sum(k * k, axis=-1, keepdims=True) + L2_EPS)
        kb = k * beta
        eg = jnp.exp(gc)
        gc_col = jnp.broadcast_to(gc, (c, c))
        gc_row = jnp.sum(jnp.where(diag, gc_col, 0.0), axis=0, keepdims=True)
        decay = jnp.where(causal, jnp.exp(gc_col - gc_row), 0.0)
        kq = _dot_nt(jnp.concatenate([kb, q], axis=0).astype(BF16), k.astype(BF16))
        a_mat = jnp.where(strict, kq[:c] * decay, 0.0)
        qk = kq[c:] * decay
        rhs = jnp.concatenate([v * beta, kb * eg], axis=1)
        sol = _unit_lower_solve(a_mat, rhs, valid_rows)
        u = sol[:, :dv]
        wmat = sol[:, dv:]
        gc_last = gc[c - 1:c, :]
        q_dec = q * eg
        k_dec = k * jnp.exp(gc_last - gc)
        s = s_scr[h]
        s_bf = s.astype(BF16)
        ws_qs = _dot(jnp.concatenate([wmat, q_dec], axis=0).astype(BF16), s_bf)
        v_new = u - ws_qs[:c]
        v_new_bf = v_new.astype(BF16)
        o = ws_qs[c:] + _dot(qk.astype(BF16), v_new_bf)
        s_scr[h] = s * jnp.exp(gc_last) + _dot_tn(k_dec.astype(BF16), v_new_bf)
        o = o * lax.rsqrt(jnp.mean(o * o, axis=-1, keepdims=True) + NORM_EPS) * nw
        o = o * _silu(z_ref[:, h * dv:(h + 1) * dv])
        o_ref[:, h * dv:(h + 1) * dv] = o.astype(o_ref.dtype)

    @pl.when(n == pl.num_programs(1) - 1)
    def _():
        snew_ref[...] = s_scr[...]


def _gdn(proj, col0, conv_init, s0, conv_w, a_log, dt_bias, norm_w, *, chunk, valid_rows, heads, dk, dv):
    b, t, ncol = proj.shape
    conv_ch = heads * (2 * dk + dv)
    width = heads * dv
    n_chunks = t // chunk
    qkv_blk = col0 // conv_ch
    z_blk = (col0 + conv_ch) // width
    ba_blk = ncol // LANES - 1
    lane_vec = lambda x: jnp.zeros((1, LANES), F32).at[0, heads:2 * heads].set(x.astype(F32))
    return pl.pallas_call(
        functools.partial(_gdn_kernel, valid_rows=valid_rows, heads=heads, dk=dk, dv=dv),
        grid=(b, n_chunks),
        in_specs=[
            pl.BlockSpec((None, chunk, conv_ch), lambda i, n: (i, n, qkv_blk)),
            pl.BlockSpec((None, chunk, width), lambda i, n: (i, n, z_blk)),
            pl.BlockSpec((None, chunk, LANES), lambda i, n: (i, n, ba_blk)),
            pl.BlockSpec((None, SUBLANES, conv_ch), lambda i, n: (i, 0, 0)),
            pl.BlockSpec((None, heads, dk, dv), lambda i, n: (i, 0, 0, 0)),
            pl.BlockSpec((CONV_WIDTH, conv_ch), lambda i, n: (0, 0)),
            pl.BlockSpec((1, LANES), lambda i, n: (0, 0)),
            pl.BlockSpec((1, LANES), lambda i, n: (0, 0)),
            pl.BlockSpec((1, dv), lambda i, n: (0, 0)),
        ],
        out_specs=[
            pl.BlockSpec((None, chunk, width), lambda i, n: (i, n, 0)),
            pl.BlockSpec((None, heads, dk, dv), lambda i, n: (i, 0, 0, 0)),
            pl.BlockSpec((None, SUBLANES, conv_ch), lambda i, n: (i, 0, 0)),
        ],
        out_shape=[
            jax.ShapeDtypeStruct((b, t, width), BF16),
            jax.ShapeDtypeStruct((b, heads, dk, dv), F32),
            jax.ShapeDtypeStruct((b, SUBLANES, conv_ch), F32),
        ],
        scratch_shapes=[
            pltpu.VMEM((chunk + 2 * SUBLANES, conv_ch), F32),
            pltpu.VMEM((heads, dk, dv), F32),
        ],
        compiler_params=_params("parallel", "arbitrary"),
        name="gdn_mixer",
    )(proj, proj, proj, conv_init, s0, conv_w, lane_vec(a_log), lane_vec(dt_bias), norm_w.reshape(1, dv))


def _ret_kernel(q_ref, k_ref, v_ref, g_ref, s0_ref, invf_ref, gnw_ref, gnb_ref,
                o_ref, snew_ref, s_scr, *, valid_rows, pos0, heads, dk, dv):
    n = pl.program_id(1)
    c = q_ref.shape[0]
    half = dk // 2

    @pl.when(n == 0)
    def _():
        s_scr[...] = s0_ref[...]

    t_idx = lax.broadcasted_iota(jnp.int32, (c, dk), 0)
    pos = (pos0 + n * c + t_idx).astype(F32)
    ang = pos * invf_ref[...]
    cos2 = jnp.cos(ang)
    lane = lax.broadcasted_iota(jnp.int32, (c, dk), 1)
    sin2 = jnp.where(lane < half, -jnp.sin(ang), jnp.sin(ang))

    row = lax.broadcasted_iota(jnp.int32, (c, c), 0)
    col = lax.broadcasted_iota(jnp.int32, (c, c), 1)
    rel = (row - col).astype(F32)
    idx = lax.broadcasted_iota(jnp.int32, (c, 1), 0).astype(F32)
    live = lax.broadcasted_iota(jnp.int32, (c, 1), 0) < valid_rows
    for h in range(heads):
        log_gamma = math.log(1.0 - 2.0 ** (-5.0 - h))
        q = q_ref[:, h * dk:(h + 1) * dk]
        k = k_ref[:, h * dk:(h + 1) * dk]
        v = v_ref[:, h * dv:(h + 1) * dv]
        if valid_rows < c:
            v = jnp.where(live, v, 0.0)
        q = q * cos2 + pltpu.roll(q, half, 1) * sin2
        k = (k * cos2 + pltpu.roll(k, half, 1) * sin2) * (dk ** -0.5)
        dmat = jnp.where(rel >= 0, jnp.exp(jnp.maximum(rel, 0.0) * log_gamma), 0.0)
        q_bf = q.astype(BF16)
        inner = _dot_nt(q_bf, k.astype(BF16)) * dmat
        q_dec = q * jnp.exp((idx + 1.0) * log_gamma)
        k_dec = k * jnp.exp((valid_rows - 1.0 - idx) * log_gamma)
        v_bf = v.astype(BF16)
        s = s_scr[h]
        o = _dot(q_dec.astype(BF16), s.astype(BF16)) + _dot(inner.astype(BF16), v_bf)
        s_scr[h] = s * math.exp(valid_rows * log_gamma) + _dot_tn(k_dec.astype(BF16), v_bf)
        mu = jnp.mean(o, axis=-1, keepdims=True)
        oc = o - mu
        var = jnp.mean(oc * oc, axis=-1, keepdims=True)
        o = oc * lax.rsqrt(var + NORM_EPS)
        o = o * gnw_ref[:, h * dv:(h + 1) * dv] + gnb_ref[:, h * dv:(h + 1) * dv]
        o = o * _silu(g_ref[:, h * dv:(h + 1) * dv])
        o_ref[:, h * dv:(h + 1) * dv] = o.astype(o_ref.dtype)

    @pl.when(n == pl.num_programs(1) - 1)
    def _():
        snew_ref[...] = s_scr[...]


def _ret(proj, col0, s0, gn_w, gn_b, *, chunk, valid_rows, pos0, heads, dk, dv):
    b, t, _ = proj.shape
    width = heads * dv
    n_chunks = t // chunk
    blk0 = col0 // width
    half = dk // 2
    inv_freq = ROPE_BASE ** (-jnp.arange(half, dtype=F32) / half)
    inv_freq2 = jnp.concatenate([inv_freq, inv_freq]).reshape(1, dk)
    col_spec = lambda kk: pl.BlockSpec((None, chunk, width), lambda i, n: (i, n, blk0 + kk))
    return pl.pallas_call(
        functools.partial(_ret_kernel, valid_rows=valid_rows, pos0=pos0, heads=heads, dk=dk, dv=dv),
        grid=(b, n_chunks),
        in_specs=[
            col_spec(0), col_spec(1), col_spec(2), col_spec(3),
            pl.BlockSpec((None, heads, dk, dv), lambda i, n: (i, 0, 0, 0)),
            pl.BlockSpec((1, dk), lambda i, n: (0, 0)),
            pl.BlockSpec((1, width), lambda i, n: (0, 0)),
            pl.BlockSpec((1, width), lambda i, n: (0, 0)),
        ],
        out_specs=[
            pl.BlockSpec((None, chunk, width), lambda i, n: (i, n, 0)),
            pl.BlockSpec((None, heads, dk, dv), lambda i, n: (i, 0, 0, 0)),
        ],
        out_shape=[
            jax.ShapeDtypeStruct((b, t, width), BF16),
            jax.ShapeDtypeStruct((b, heads, dk, dv), F32),
        ],
        scratch_shapes=[pltpu.VMEM((heads, dk, dv), F32)],
        compiler_params=_params("parallel", "arbitrary"),
        name="ret_mixer",
    )(proj, proj, proj, proj, s0, inv_freq2, gn_w.reshape(1, width), gn_b.reshape(1, width))


def _trunk(x, mod, mod_f, rows_per_group, seqs, seq_rows, valid_rows, pos0, to_seq, from_seq,
           conv_init, s_gdn0, s_ret0, w, tm, tf):
    d = x.shape[1]
    heads, dk, dv = s_gdn0.shape[1:]
    x1, h2 = _ffn(x, mod, (0, 1, 2), w["norm_ffn1"], w["w1_gate"], w["w1_up"], w["w1_down"],
                  mod, (3, 4), w["norm_mix"], rows_per_group=rows_per_group, tm=tm, tf=tf,
                  emit_x=True, hn_dtype=BF16)
    proj = _proj(h2, w["w_in_aligned"], tm=min(tm, 512), tn=640)
    proj = to_seq(proj)
    chunk = min(CHUNK, seq_rows)
    conv_ch = heads * (2 * dk + dv)
    o_gdn, s_gdn, conv_new = _gdn(proj, 0, conv_init, s_gdn0, w["conv_w"], w["a_log"], w["dt_bias"],
                                  w["gdn_norm_w"], chunk=chunk, valid_rows=valid_rows,
                                  heads=heads, dk=dk, dv=dv)
    o_ret, s_ret = _ret(proj, conv_ch + heads * dv, s_ret0, w["ret_gn_w"], w["ret_gn_b"],
                        chunk=chunk, valid_rows=valid_rows, pos0=pos0, heads=heads, dk=dk, dv=dv)
    o_gdn = from_seq(o_gdn)
    o_ret = from_seq(o_ret)
    x2 = _outproj(x1, mod, 5, o_gdn, o_ret, w["w_out"], rows_per_group=rows_per_group,
                  tm=min(tm, 512), tn=512)
    (y,) = _ffn(x2, mod, (6, 7, 8), w["norm_ffn2"], w["w2_gate"], w["w2_up"], w["w2_down"],
                mod_f, (0, 1), w["norm_final"], rows_per_group=rows_per_group, tm=tm, tf=tf,
                emit_x=False, hn_dtype=F32)
    return y, s_gdn, conv_new[:, SUBLANES - (CONV_WIDTH - 1):, :], s_ret


def kernel(x_prompt, x_sample, state_gdn, state_conv, state_ret, c_prompt, c_sample, w_ada, b_ada, norm_ffn1, w1_gate, w1_up, w1_down, norm_mix, w_in, conv_w, a_log, dt_bias, gdn_norm_w, ret_gn_w, ret_gn_b, w_out, norm_ffn2, w2_gate, w2_up, w2_down, w_ada_final, b_ada_final, norm_final):
    bp, tp, d = x_prompt.shape
    bs, ts, _ = x_sample.shape
    depth, _, heads, dk, dv = state_gdn.shape
    assert depth == 1, "single-layer trunk"
    assert bs == ROW_TILE, "time-major sample rows must align adaLN vectors with row tiles"
    conv_ch = state_conv.shape[-1]
    width = heads * dv
    n_in = w_in.shape[-1]
    ba0 = conv_ch + width
    ret0 = ba0 + 2 * heads

    n_c = bs + bp
    n_c_pad = -(-n_c // SUBLANES) * SUBLANES
    c_all = jnp.concatenate([c_sample, c_prompt, jnp.zeros((n_c_pad - n_c, d), F32)], axis=0)
    ada = _ada_proj(c_all, w_ada[0], b_ada[0], tn=512)
    ada_f = _ada_proj(c_all, w_ada_final, b_ada_final, tn=512)
    mod_s, mod_p = ada, ada[bs:bs + bp].reshape(bp, 1, N_ADA * d)
    modf_s, modf_p = ada_f, ada_f[bs:bs + bp].reshape(bp, 1, 2 * d)

    w_in0 = w_in[0]
    w_in_aligned = jnp.concatenate(
        [w_in0[:, :ba0], w_in0[:, ret0:], w_in0[:, ba0:ret0],
         jnp.zeros((d, LANES - 2 * heads), F32)], axis=1)
    w = dict(norm_ffn1=norm_ffn1[0], w1_gate=w1_gate[0], w1_up=w1_up[0], w1_down=w1_down[0],
             norm_mix=norm_mix[0], w_in_aligned=w_in_aligned, conv_w=conv_w[0], a_log=a_log[0],
             dt_bias=dt_bias[0], gdn_norm_w=gdn_norm_w[0], ret_gn_w=ret_gn_w[0], ret_gn_b=ret_gn_b[0],
             w_out=w_out[0], norm_ffn2=norm_ffn2[0], w2_gate=w2_gate[0], w2_up=w2_up[0],
             w2_down=w2_down[0], norm_final=norm_final)

    zeros_state = jnp.zeros((bp, heads, dk, dv), F32)
    y_p, gdn_p, conv_p, ret_p = _trunk(
        x_prompt.reshape(bp * tp, d), mod_p, modf_p, tp, bp, tp, min(CHUNK, tp), 0,
        lambda a: a.reshape(bp, tp, a.shape[-1]), lambda a: a.reshape(bp * tp, a.shape[-1]),
        jnp.zeros((bp, SUBLANES, conv_ch), F32), zeros_state, zeros_state, w, tm=512, tf=256)

    ts_pad = -(-ts // SUBLANES) * SUBLANES
    to_seq = lambda a: jnp.pad(a.reshape(ts, bs, a.shape[-1]).transpose(1, 0, 2),
                               ((0, 0), (0, ts_pad - ts), (0, 0)))
    from_seq = lambda a: a[:, :ts].transpose(1, 0, 2).reshape(ts * bs, a.shape[-1])
    conv_init = jnp.pad(state_conv[0], ((0, 0), (SUBLANES - (CONV_WIDTH - 1), 0), (0, 0)))
    y_s, gdn_s, conv_s, ret_s = _trunk(
        x_sample.transpose(1, 0, 2).reshape(ts * bs, d), mod_s, modf_s, None, bs, ts_pad, ts, PAST_LEN,
        to_seq, from_seq, conv_init, state_gdn[0], state_ret[0], w, tm=ts * bs, tf=256)
    y_s = y_s.reshape(ts, bs, d).transpose(1, 0, 2)

    return (y_p.reshape(bp, tp, d), y_s, gdn_p[None], conv_p[None], ret_p[None],
            gdn_s[None], conv_s[None], ret_s[None])
```

```python
import functools
import math

import jax
import jax.numpy as jnp
from jax import lax
from jax.experimental import pallas as pl
from jax.experimental.pallas import tpu as pltpu

F32 = jnp.float32
BF16 = jnp.bfloat16

LANES = 128
SUBLANES = 8
VMEM_LIMIT_BYTES = 56 * 1024 * 1024

PAST_LEN = 16384
CONV_WIDTH = 4
CHUNK = 64
ROPE_BASE = 10000.0
NORM_EPS = 1e-6
L2_EPS = 1e-6
MACARON_WEIGHT = 0.5
N_ADA = 9
ROW_TILE = 128


def _sigmoid(x):
    return 1.0 / (1.0 + jnp.exp(-x))


def _silu(x):
    return x * _sigmoid(x)


def _softplus(x):
    return jnp.maximum(x, 0.0) + jnp.log(1.0 + jnp.exp(-jnp.abs(x)))


def _dot(a, b):
    return jnp.dot(a, b, preferred_element_type=F32)


def _dot_nt(a, b):
    return lax.dot_general(a, b, (((1,), (1,)), ((), ())), preferred_element_type=F32)


def _dot_tn(a, b):
    return lax.dot_general(a, b, (((0,), (0,)), ((), ())), preferred_element_type=F32)


def _split3(x):
    hi = x.astype(BF16)
    r = x - hi.astype(F32)
    mid = r.astype(BF16)
    lo = (r - mid.astype(F32)).astype(BF16)
    return hi, mid, lo


def _rms_mod(x, gain, shift, scale):
    y = x * lax.rsqrt(jnp.mean(x * x, axis=-1, keepdims=True) + NORM_EPS)
    return (y * gain) * (1.0 + scale) + shift


def _params(*sem):
    return pltpu.CompilerParams(dimension_semantics=sem, vmem_limit_bytes=VMEM_LIMIT_BYTES)


def _ada_kernel(c_ref, w_ref, b_ref, o_ref):
    a = _silu(c_ref[...]).astype(BF16)
    o_ref[...] = _dot(a, w_ref[...].astype(BF16)) + b_ref[...]


def _ada_proj(c, w, b, tn):
    m, d = c.shape
    n = w.shape[1]
    return pl.pallas_call(
        _ada_kernel,
        grid=(n // tn,),
        in_specs=[
            pl.BlockSpec((m, d), lambda j: (0, 0)),
            pl.BlockSpec((d, tn), lambda j: (0, j)),
            pl.BlockSpec((1, tn), lambda j: (0, j)),
        ],
        out_specs=pl.BlockSpec((m, tn), lambda j: (0, j)),
        out_shape=jax.ShapeDtypeStruct((m, n), F32),
        compiler_params=_params("parallel"),
        name="ada_proj",
    )(c, w, b.reshape(1, n))


def _ffn_kernel(x_ref, sh_ref, sc_ref, gt_ref, gain_ref, wg_ref, wu_ref, wd_ref,
                gain2_ref, sh2_ref, sc2_ref, *rest, emit_x):
    if emit_x:
        xo_ref, hn_ref, h_scr, acc_scr = rest
    else:
        hn_ref, h_scr, acc_scr = rest
    j = pl.program_id(1)
    tm = x_ref.shape[0]
    n_tiles = tm // ROW_TILE

    @pl.when(j == 0)
    def _():
        def body(r, carry):
            rows = pl.ds(pl.multiple_of(r * ROW_TILE, ROW_TILE), ROW_TILE)
            h = _rms_mod(x_ref[rows, :], gain_ref[...], sh_ref[...], sc_ref[...])
            h_scr[rows, :] = h.astype(BF16)
            return carry
        lax.fori_loop(0, n_tiles, body, 0)

    h = h_scr[...]
    g = _dot(h, wg_ref[...].astype(BF16))
    u = _dot(h, wu_ref[...].astype(BF16))
    a = (_silu(g) * u).astype(BF16)
    d = _dot(a, wd_ref[...].astype(BF16))

    @pl.when(j == 0)
    def _():
        acc_scr[...] = d

    @pl.when(j > 0)
    def _():
        acc_scr[...] += d

    @pl.when(j == pl.num_programs(1) - 1)
    def _():
        def body(r, carry):
            rows = pl.ds(pl.multiple_of(r * ROW_TILE, ROW_TILE), ROW_TILE)
            xo = x_ref[rows, :] + (MACARON_WEIGHT * gt_ref[...]) * acc_scr[rows, :]
            if emit_x:
                xo_ref[rows, :] = xo
            hn = _rms_mod(xo, gain2_ref[...], sh2_ref[...], sc2_ref[...])
            hn_ref[rows, :] = hn.astype(hn_ref.dtype)
            return carry
        lax.fori_loop(0, n_tiles, body, 0)


def _mod_spec(mod, k, d, rows_per_group, tm):
    if mod.ndim == 3:
        blocks_per_group = rows_per_group // tm
        return pl.BlockSpec((None, 1, d), lambda i, j: (i // blocks_per_group, 0, k))
    return pl.BlockSpec((ROW_TILE, d), lambda i, j: (0, k))


def _ffn(x, mod, ks, gain, wg, wu, wd, mod2, ks2, gain2, *, rows_per_group, tm, tf, emit_x, hn_dtype):
    m, d = x.shape
    f = wg.shape[1]
    k_sh, k_sc, k_gt = ks
    k_sh2, k_sc2 = ks2
    vec = lambda: pl.BlockSpec((1, d), lambda i, j: (0, 0))
    row_block = lambda: pl.BlockSpec((tm, d), lambda i, j: (i, 0))
    out_shape = [jax.ShapeDtypeStruct((m, d), hn_dtype)]
    out_specs = [row_block()]
    if emit_x:
        out_shape = [jax.ShapeDtypeStruct((m, d), F32)] + out_shape
        out_specs = [row_block()] + out_specs
    return pl.pallas_call(
        functools.partial(_ffn_kernel, emit_x=emit_x),
        grid=(m // tm, f // tf),
        in_specs=[
            row_block(),
            _mod_spec(mod, k_sh, d, rows_per_group, tm),
            _mod_spec(mod, k_sc, d, rows_per_group, tm),
            _mod_spec(mod, k_gt, d, rows_per_group, tm),
            vec(),
            pl.BlockSpec((d, tf), lambda i, j: (0, j)),
            pl.BlockSpec((d, tf), lambda i, j: (0, j)),
            pl.BlockSpec((tf, d), lambda i, j: (j, 0)),
            vec(),
            _mod_spec(mod2, k_sh2, d, rows_per_group, tm),
            _mod_spec(mod2, k_sc2, d, rows_per_group, tm),
        ],
        out_specs=out_specs,
        out_shape=out_shape,
        scratch_shapes=[pltpu.VMEM((tm, d), BF16), pltpu.VMEM((tm, d), F32)],
        compiler_params=_params("parallel", "arbitrary"),
        name="ffn",
    )(x, mod, mod, mod, gain.reshape(1, d), wg, wu, wd, gain2.reshape(1, d), mod2, mod2)


def _proj_kernel(h_ref, w_ref, o_ref):
    o_ref[...] = _dot(h_ref[...], w_ref[...].astype(BF16))


def _proj(h, w, tm, tn):
    m, d = h.shape
    n = w.shape[1]
    return pl.pallas_call(
        _proj_kernel,
        grid=(m // tm, n // tn),
        in_specs=[
            pl.BlockSpec((tm, d), lambda i, j: (i, 0)),
            pl.BlockSpec((d, tn), lambda i, j: (0, j)),
        ],
        out_specs=pl.BlockSpec((tm, tn), lambda i, j: (i, j)),
        out_shape=jax.ShapeDtypeStruct((m, n), F32),
        compiler_params=_params("parallel", "arbitrary"),
        name="in_proj",
    )(h, w)


def _outproj_kernel(x_ref, gt_ref, og_ref, or_ref, wa_ref, wb_ref, o_ref):
    mixed = _dot(og_ref[...], wa_ref[...].astype(BF16)) + _dot(or_ref[...], wb_ref[...].astype(BF16))
    tm = x_ref.shape[0]
    for r in range(tm // ROW_TILE):
        rows = slice(r * ROW_TILE, (r + 1) * ROW_TILE)
        o_ref[rows, :] = x_ref[rows, :] + gt_ref[...] * mixed[rows, :]


def _outproj(x, mod, k_gt, og, orr, w_out, *, rows_per_group, tm, tn):
    m, d = x.shape
    kw = og.shape[1]
    if mod.ndim == 3:
        blocks_per_group = rows_per_group // tm
        gt_spec = pl.BlockSpec((None, 1, tn), lambda i, j: (i // blocks_per_group, 0, k_gt * (d // tn) + j))
    else:
        gt_spec = pl.BlockSpec((ROW_TILE, tn), lambda i, j: (0, k_gt * (d // tn) + j))
    return pl.pallas_call(
        _outproj_kernel,
        grid=(m // tm, d // tn),
        in_specs=[
            pl.BlockSpec((tm, tn), lambda i, j: (i, j)),
            gt_spec,
            pl.BlockSpec((tm, kw), lambda i, j: (i, 0)),
            pl.BlockSpec((tm, kw), lambda i, j: (i, 0)),
            pl.BlockSpec((kw, tn), lambda i, j: (0, j)),
            pl.BlockSpec((kw, tn), lambda i, j: (1, j)),
        ],
        out_specs=pl.BlockSpec((tm, tn), lambda i, j: (i, j)),
        out_shape=jax.ShapeDtypeStruct((m, d), F32),
        compiler_params=_params("parallel", "arbitrary"),
        name="out_proj",
    )(x, mod, og, orr, w_out, w_out)


SOLVE_BLOCK = 16


def _nilpotent_apply(chains, index):
    power = 1
    while power < index:
        square = 2 * power < index
        for ch in chains:
            right = jnp.concatenate([ch["sol"], ch["nil"]], axis=1) if square else ch["sol"]
            ch["prod"] = _dot(ch["nil"].astype(BF16), right.astype(BF16))
        for ch in chains:
            width = ch["sol"].shape[1]
            upd = ch["prod"][:, :width]
            ch["sol"] = ch["sol"] - upd if power == 1 else ch["sol"] + upd
            if square:
                ch["nil"] = ch["prod"][:, width:]
        power *= 2


def _gdn_kernel(qkv_ref, z_ref, ba_ref, cinit_ref, s0_ref, convw_ref, alog_ref, dtb_ref, nw_ref,
                o_ref, snew_ref, cnew_ref, tail_scr, s_scr, *, valid_rows, heads, dk, dv):
    n = pl.program_id(1)
    last = pl.num_programs(1) - 1
    nb, c, _ = qkv_ref.shape
    hist = SUBLANES

    @pl.when(n == 0)
    def _():
        tail_scr[...] = cinit_ref[...]
        s_scr[...] = s0_ref[...]

    row = lax.broadcasted_iota(jnp.int32, (c, c), 0)
    col = lax.broadcasted_iota(jnp.int32, (c, c), 1)
    causal = row >= col
    strict = row > col
    diag = row == col
    tril = jnp.where(causal, 1.0, 0.0).astype(BF16)
    w = convw_ref[...]
    nw = nw_ref[...]
    neg_a = -jnp.exp(alog_ref[...])
    dtb = dtb_ref[...]

    chains = []
    for i in range(nb):
        x = qkv_ref[i]
        xp = jnp.concatenate([tail_scr[i], x], axis=0)
        acc = x * w[CONV_WIDTH - 1:CONV_WIDTH, :]
        for sft in range(1, CONV_WIDTH):
            tap = CONV_WIDTH - 1 - sft
            acc = acc + pltpu.roll(xp, sft, 0)[hist:, :] * w[tap:tap + 1, :]
        qkv = _silu(acc)
        tail_scr[i] = x[c - hist:, :]

        @pl.when(n == last)
        def _(i=i, xp=xp):
            cnew_ref[i] = xp[valid_rows:valid_rows + hist, :]

        ba = ba_ref[i]
        beta_all = _sigmoid(ba)
        g_all = neg_a * _softplus(ba + dtb)
        if valid_rows < c:
            live = lax.broadcasted_iota(jnp.int32, ba.shape, 0) < valid_rows
            beta_all = jnp.where(live, beta_all, 0.0)
            g_all = jnp.where(live, g_all, 0.0)
        g_hi, g_mid, g_lo = _split3(g_all)
        gc_all = _dot(tril, g_hi) + (_dot(tril, g_mid) + _dot(tril, g_lo))
        gc_last_all = gc_all[c - 1:c, :]
        eg_all = jnp.exp(gc_all)
        kdec_all = jnp.exp(gc_last_all - gc_all)
        gl_all = jnp.exp(gc_last_all)

        for h in range(heads):
            beta = beta_all[:, h:h + 1]
            gc = gc_all[:, heads + h:heads + h + 1]
            eg = eg_all[:, heads + h:heads + h + 1]
            q = qkv[:, h * dk:(h + 1) * dk]
            k = qkv[:, heads * dk + h * dk:heads * dk + (h + 1) * dk]
            v = qkv[:, 2 * heads * dk + h * dv:2 * heads * dk + (h + 1) * dv]
            q = q * (lax.rsqrt(jnp.sum(q * q, axis=-1, keepdims=True) + L2_EPS) * (dk ** -0.5))
            k = k * lax.rsqrt(jnp.sum(k * k, axis=-1, keepdims=True) + L2_EPS)
            kb = k * beta
            gc_col = jnp.broadcast_to(gc, (c, c))
            gc_row = jnp.sum(jnp.where(diag, gc_col, 0.0), axis=0, keepdims=True)
            decay = jnp.where(causal, jnp.exp(gc_col - gc_row), 0.0)
            chains.append(dict(
                i=i, h=h, decay=decay,
                kbq=jnp.concatenate([kb, q], axis=0).astype(BF16), k_bf=k.astype(BF16),
                rhs=jnp.concatenate([v * beta, kb * eg], axis=1),
                q_dec=q * eg,
                k_dec=(k * kdec_all[:, heads + h:heads + h + 1]).astype(BF16),
                gl=gl_all[:, heads + h:heads + h + 1]))

    for ch in chains:
        ch["kq"] = _dot_nt(ch["kbq"], ch["k_bf"])
    for ch in chains:
        ch["a"] = jnp.where(strict, ch["kq"][:c] * ch["decay"], 0.0)
        ch["qk"] = (ch["kq"][c:] * ch["decay"]).astype(BF16)

    if valid_rows <= SOLVE_BLOCK:
        for ch in chains:
            ch["nil"], ch["sol"] = ch["a"], ch["rhs"]
        _nilpotent_apply(chains, valid_rows)
    else:
        same_block = (row // SOLVE_BLOCK) == (col // SOLVE_BLOCK)
        eye = jnp.where(diag, 1.0, 0.0)
        for ch in chains:
            ch["nil"] = jnp.where(same_block, ch["a"], 0.0)
            ch["sol"] = eye
        _nilpotent_apply(chains, SOLVE_BLOCK)
        for ch in chains:
            off_block = jnp.where(same_block, 0.0, ch["a"])
            right = jnp.concatenate([ch["rhs"], off_block], axis=1)
            ch["prod"] = _dot(ch["sol"].astype(BF16), right.astype(BF16))
        for ch in chains:
            ch["sol"] = ch["prod"][:, :dv + dk]
            ch["nil"] = ch["prod"][:, dv + dk:]
        _nilpotent_apply(chains, c // SOLVE_BLOCK)

    for ch in chains:
        ch["s"] = s_scr[ch["i"], ch["h"]]
        lhs = jnp.concatenate([ch["sol"][:, dv:], ch["q_dec"]], axis=0).astype(BF16)
        ch["ws_qs"] = _dot(lhs, ch["s"].astype(BF16))
    for ch in chains:
        ch["v_new"] = (ch["sol"][:, :dv] - ch["ws_qs"][:c]).astype(BF16)
    for ch in chains:
        s_scr[ch["i"], ch["h"]] = ch["s"] * ch["gl"] + _dot_tn(ch["k_dec"], ch["v_new"])
        ch["o"] = ch["ws_qs"][c:] + _dot(ch["qk"], ch["v_new"])
    for ch in chains:
        i, h, o = ch["i"], ch["h"], ch["o"]
        o = o * lax.rsqrt(jnp.mean(o * o, axis=-1, keepdims=True) + NORM_EPS) * nw
        o = o * _silu(z_ref[i, :, h * dv:(h + 1) * dv])
        o_ref[i, :, h * dv:(h + 1) * dv] = o.astype(o_ref.dtype)

    @pl.when(n == last)
    def _():
        snew_ref[...] = s_scr[...]


def _gdn(proj, col0, conv_init, s0, conv_w, a_log, dt_bias, norm_w, *, chunk, valid_rows, nb, heads, dk, dv):
    b, t, ncol = proj.shape
    conv_ch = heads * (2 * dk + dv)
    width = heads * dv
    n_chunks = t // chunk
    qkv_blk = col0 // conv_ch
    z_blk = (col0 + conv_ch) // width
    ba_blk = ncol // LANES - 1
    lane_vec = lambda x: jnp.zeros((1, LANES), F32).at[0, heads:2 * heads].set(x.astype(F32))
    return pl.pallas_call(
        functools.partial(_gdn_kernel, valid_rows=valid_rows, heads=heads, dk=dk, dv=dv),
        grid=(b // nb, n_chunks),
        in_specs=[
            pl.BlockSpec((nb, chunk, conv_ch), lambda i, n: (i, n, qkv_blk)),
            pl.BlockSpec((nb, chunk, width), lambda i, n: (i, n, z_blk)),
            pl.BlockSpec((nb, chunk, LANES), lambda i, n: (i, n, ba_blk)),
            pl.BlockSpec((nb, SUBLANES, conv_ch), lambda i, n: (i, 0, 0)),
            pl.BlockSpec((nb, heads, dk, dv), lambda i, n: (i, 0, 0, 0)),
            pl.BlockSpec((CONV_WIDTH, conv_ch), lambda i, n: (0, 0)),
            pl.BlockSpec((1, LANES), lambda i, n: (0, 0)),
            pl.BlockSpec((1, LANES), lambda i, n: (0, 0)),
            pl.BlockSpec((1, dv), lambda i, n: (0, 0)),
        ],
        out_specs=[
            pl.BlockSpec((nb, chunk, width), lambda i, n: (i, n, 0)),
            pl.BlockSpec((nb, heads, dk, dv), lambda i, n: (i, 0, 0, 0)),
            pl.BlockSpec((nb, SUBLANES, conv_ch), lambda i, n: (i, 0, 0)),
        ],
        out_shape=[
            jax.ShapeDtypeStruct((b, t, width), BF16),
            jax.ShapeDtypeStruct((b, heads, dk, dv), F32),
            jax.ShapeDtypeStruct((b, SUBLANES, conv_ch), F32),
        ],
        scratch_shapes=[
            pltpu.VMEM((nb, SUBLANES, conv_ch), F32),
            pltpu.VMEM((nb, heads, dk, dv), F32),
        ],
        compiler_params=_params("parallel", "arbitrary"),
        name="gdn_mixer",
    )(proj, proj, proj, conv_init, s0, conv_w, lane_vec(a_log), lane_vec(dt_bias), norm_w.reshape(1, dv))


def _ret_kernel(q_ref, k_ref, v_ref, g_ref, s0_ref, invf_ref, gnw_ref, gnb_ref,
                o_ref, snew_ref, s_scr, *, valid_rows, pos0, heads, dk, dv):
    n = pl.program_id(1)
    nb, c, _ = q_ref.shape
    half = dk // 2

    @pl.when(n == 0)
    def _():
        s_scr[...] = s0_ref[...]

    t_idx = lax.broadcasted_iota(jnp.int32, (c, dk), 0)
    pos = (pos0 + n * c + t_idx).astype(F32)
    ang = pos * invf_ref[...]
    cos2 = jnp.cos(ang)
    lane = lax.broadcasted_iota(jnp.int32, (c, dk), 1)
    sin2 = jnp.where(lane < half, -jnp.sin(ang), jnp.sin(ang))

    row = lax.broadcasted_iota(jnp.int32, (c, c), 0)
    col = lax.broadcasted_iota(jnp.int32, (c, c), 1)
    rel = (row - col).astype(F32)
    idx = lax.broadcasted_iota(jnp.int32, (c, 1), 0).astype(F32)
    live = lax.broadcasted_iota(jnp.int32, (c, 1), 0) < valid_rows

    chains = []
    for h in range(heads):
        log_gamma = math.log(1.0 - 2.0 ** (-5.0 - h))
        dmat = jnp.where(rel >= 0, jnp.exp(jnp.maximum(rel, 0.0) * log_gamma), 0.0)
        q_scale = jnp.exp((idx + 1.0) * log_gamma)
        k_scale = jnp.exp((valid_rows - 1.0 - idx) * log_gamma)
        for i in range(nb):
            q = q_ref[i, :, h * dk:(h + 1) * dk]
            k = k_ref[i, :, h * dk:(h + 1) * dk]
            v = v_ref[i, :, h * dv:(h + 1) * dv]
            if valid_rows < c:
                v = jnp.where(live, v, 0.0)
            q = q * cos2 + pltpu.roll(q, half, 1) * sin2
            k = (k * cos2 + pltpu.roll(k, half, 1) * sin2) * (dk ** -0.5)
            chains.append(dict(
                i=i, h=h, dmat=dmat, s_decay=math.exp(valid_rows * log_gamma),
                q_bf=q.astype(BF16), k_bf=k.astype(BF16), v_bf=v.astype(BF16),
                q_dec=(q * q_scale).astype(BF16), k_dec=(k * k_scale).astype(BF16)))

    for ch in chains:
        ch["inner"] = (_dot_nt(ch["q_bf"], ch["k_bf"]) * ch["dmat"]).astype(BF16)
    for ch in chains:
        s = s_scr[ch["i"], ch["h"]]
        ch["o"] = _dot(ch["q_dec"], s.astype(BF16)) + _dot(ch["inner"], ch["v_bf"])
        s_scr[ch["i"], ch["h"]] = s * ch["s_decay"] + _dot_tn(ch["k_dec"], ch["v_bf"])
    for ch in chains:
        i, h, o = ch["i"], ch["h"], ch["o"]
        mu = jnp.mean(o, axis=-1, keepdims=True)
        oc = o - mu
        var = jnp.mean(oc * oc, axis=-1, keepdims=True)
        o = oc * lax.rsqrt(var + NORM_EPS)
        o = o * gnw_ref[:, h * dv:(h + 1) * dv] + gnb_ref[:, h * dv:(h + 1) * dv]
        o = o * _silu(g_ref[i, :, h * dv:(h + 1) * dv])
        o_ref[i, :, h * dv:(h + 1) * dv] = o.astype(o_ref.dtype)

    @pl.when(n == pl.num_programs(1) - 1)
    def _():
        snew_ref[...] = s_scr[...]


def _ret(proj, col0, s0, gn_w, gn_b, *, chunk, valid_rows, nb, pos0, heads, dk, dv):
    b, t, _ = proj.shape
    width = heads * dv
    n_chunks = t // chunk
    blk0 = col0 // width
    half = dk // 2
    inv_freq = ROPE_BASE ** (-jnp.arange(half, dtype=F32) / half)
    inv_freq2 = jnp.concatenate([inv_freq, inv_freq]).reshape(1, dk)
    col_spec = lambda kk: pl.BlockSpec((nb, chunk, width), lambda i, n: (i, n, blk0 + kk))
    return pl.pallas_call(
        functools.partial(_ret_kernel, valid_rows=valid_rows, pos0=pos0, heads=heads, dk=dk, dv=dv),
        grid=(b // nb, n_chunks),
        in_specs=[
            col_spec(0), col_spec(1), col_spec(2), col_spec(3),
            pl.BlockSpec((nb, heads, dk, dv), lambda i, n: (i, 0, 0, 0)),
            pl.BlockSpec((1, dk), lambda i, n: (0, 0)),
            pl.BlockSpec((1, width), lambda i, n: (0, 0)),
            pl.BlockSpec((1, width), lambda i, n: (0, 0)),
        ],
        out_specs=[
            pl.BlockSpec((nb, chunk, width), lambda i, n: (i, n, 0)),
            pl.BlockSpec((nb, heads, dk, dv), lambda i, n: (i, 0, 0, 0)),
        ],
        out_shape=[
            jax.ShapeDtypeStruct((b, t, width), BF16),
            jax.ShapeDtypeStruct((b, heads, dk, dv), F32),
        ],
        scratch_shapes=[pltpu.VMEM((nb, heads, dk, dv), F32)],
        compiler_params=_params("parallel", "arbitrary"),
        name="ret_mixer",
    )(proj, proj, proj, proj, s0, inv_freq2, gn_w.reshape(1, width), gn_b.reshape(1, width))


def _trunk(x, mod, mod_f, rows_per_group, seq_rows, valid_rows, nb, pos0, to_seq, from_seq,
           conv_init, s_gdn0, s_ret0, w, tm, tf):
    heads, dk, dv = s_gdn0.shape[1:]
    x1, h2 = _ffn(x, mod, (0, 1, 2), w["norm_ffn1"], w["w1_gate"], w["w1_up"], w["w1_down"],
                  mod, (3, 4), w["norm_mix"], rows_per_group=rows_per_group, tm=tm, tf=tf,
                  emit_x=True, hn_dtype=BF16)
    proj = _proj(h2, w["w_in_aligned"], tm=min(tm, 512), tn=640)
    proj = to_seq(proj)
    chunk = min(CHUNK, seq_rows)
    conv_ch = heads * (2 * dk + dv)
    o_gdn, s_gdn, conv_new = _gdn(proj, 0, conv_init, s_gdn0, w["conv_w"], w["a_log"], w["dt_bias"],
                                  w["gdn_norm_w"], chunk=chunk, valid_rows=valid_rows, nb=nb,
                                  heads=heads, dk=dk, dv=dv)
    o_ret, s_ret = _ret(proj, conv_ch + heads * dv, s_ret0, w["ret_gn_w"], w["ret_gn_b"],
                        chunk=chunk, valid_rows=valid_rows, nb=nb, pos0=pos0, heads=heads, dk=dk, dv=dv)
    o_gdn = from_seq(o_gdn)
    o_ret = from_seq(o_ret)
    x2 = _outproj(x1, mod, 5, o_gdn, o_ret, w["w_out"], rows_per_group=rows_per_group,
                  tm=min(tm, 512), tn=512)
    (y,) = _ffn(x2, mod, (6, 7, 8), w["norm_ffn2"], w["w2_gate"], w["w2_up"], w["w2_down"],
                mod_f, (0, 1), w["norm_final"], rows_per_group=rows_per_group, tm=tm, tf=tf,
                emit_x=False, hn_dtype=F32)
    return y, s_gdn, conv_new[:, SUBLANES - (CONV_WIDTH - 1):, :], s_ret


def kernel(x_prompt, x_sample, state_gdn, state_conv, state_ret, c_prompt, c_sample, w_ada, b_ada, norm_ffn1, w1_gate, w1_up, w1_down, norm_mix, w_in, conv_w, a_log, dt_bias, gdn_norm_w, ret_gn_w, ret_gn_b, w_out, norm_ffn2, w2_gate, w2_up, w2_down, w_ada_final, b_ada_final, norm_final):
    bp, tp, d = x_prompt.shape
    bs, ts, _ = x_sample.shape
    depth, _, heads, dk, dv = state_gdn.shape
    assert depth == 1, "single-layer trunk"
    assert bs == ROW_TILE, "time-major sample rows must align adaLN vectors with row tiles"
    conv_ch = state_conv.shape[-1]
    width = heads * dv
    ba0 = conv_ch + width
    ret0 = ba0 + 2 * heads

    n_c = bs + bp
    n_c_pad = -(-n_c // SUBLANES) * SUBLANES
    c_all = jnp.concatenate([c_sample, c_prompt, jnp.zeros((n_c_pad - n_c, d), F32)], axis=0)
    ada = _ada_proj(c_all, w_ada[0], b_ada[0], tn=512)
    ada_f = _ada_proj(c_all, w_ada_final, b_ada_final, tn=512)
    mod_s, mod_p = ada, ada[bs:bs + bp].reshape(bp, 1, N_ADA * d)
    modf_s, modf_p = ada_f, ada_f[bs:bs + bp].reshape(bp, 1, 2 * d)

    w_in0 = w_in[0]
    w_in_aligned = jnp.concatenate(
        [w_in0[:, :ba0], w_in0[:, ret0:], w_in0[:, ba0:ret0],
         jnp.zeros((d, LANES - 2 * heads), F32)], axis=1)
    w = dict(norm_ffn1=norm_ffn1[0], w1_gate=w1_gate[0], w1_up=w1_up[0], w1_down=w1_down[0],
             norm_mix=norm_mix[0], w_in_aligned=w_in_aligned, conv_w=conv_w[0], a_log=a_log[0],
             dt_bias=dt_bias[0], gdn_norm_w=gdn_norm_w[0], ret_gn_w=ret_gn_w[0], ret_gn_b=ret_gn_b[0],
             w_out=w_out[0], norm_ffn2=norm_ffn2[0], w2_gate=w2_gate[0], w2_up=w2_up[0],
             w2_down=w2_down[0], norm_final=norm_final)

    zeros_state = jnp.zeros((bp, heads, dk, dv), F32)
    y_p, gdn_p, conv_p, ret_p = _trunk(
        x_prompt.reshape(bp * tp, d), mod_p, modf_p, tp, tp, min(CHUNK, tp), 2, 0,
        lambda a: a.reshape(bp, tp, a.shape[-1]), lambda a: a.reshape(bp * tp, a.shape[-1]),
        jnp.zeros((bp, SUBLANES, conv_ch), F32), zeros_state, zeros_state, w, tm=512, tf=256)

    ts_pad = -(-ts // SUBLANES) * SUBLANES
    to_seq = lambda a: jnp.pad(a.reshape(ts, bs, a.shape[-1]).transpose(1, 0, 2),
                               ((0, 0), (0, ts_pad - ts), (0, 0)))
    from_seq = lambda a: a[:, :ts].transpose(1, 0, 2).reshape(ts * bs, a.shape[-1])
    conv_init = jnp.pad(state_conv[0], ((0, 0), (SUBLANES - (CONV_WIDTH - 1), 0), (0, 0)))
    y_s, gdn_s, conv_s, ret_s = _trunk(
        x_sample.transpose(1, 0, 2).reshape(ts * bs, d), mod_s, modf_s, None, ts_pad, ts, 8, PAST_LEN,
        to_seq, from_seq, conv_init, state_gdn[0], state_ret[0], w, tm=ts * bs, tf=256)
    y_s = y_s.reshape(ts, bs, d).transpose(1, 0, 2)

    return (y_p.reshape(bp, tp, d), y_s, gdn_p[None], conv_p[None], ret_p[None],
            gdn_s[None], conv_s[None], ret_s[None])
```

```python
import functools
import math

import jax
import jax.numpy as jnp
from jax import lax
from jax.experimental import pallas as pl
from jax.experimental.pallas import tpu as pltpu

F32 = jnp.float32
BF16 = jnp.bfloat16

LANES = 128
SUBLANES = 8
VMEM_LIMIT_BYTES = 60 * 1024 * 1024

PAST_LEN = 16384
CONV_WIDTH = 4
CHUNK = 64
ROPE_BASE = 10000.0
NORM_EPS = 1e-6
L2_EPS = 1e-6
MACARON_WEIGHT = 0.5
N_ADA = 9
ROW_TILE = 128
FFN_SUB_ROWS = 512


def _sigmoid(x):
    return 1.0 / (1.0 + jnp.exp(-x))


def _silu(x):
    return x * _sigmoid(x)


def _softplus(x):
    return jnp.maximum(x, 0.0) + jnp.log(1.0 + jnp.exp(-jnp.abs(x)))


def _dot(a, b):
    return jnp.dot(a, b, preferred_element_type=F32)


def _dot_nt(a, b):
    return lax.dot_general(a, b, (((1,), (1,)), ((), ())), preferred_element_type=F32)


def _dot_tn(a, b):
    return lax.dot_general(a, b, (((0,), (0,)), ((), ())), preferred_element_type=F32)


def _split3(x):
    hi = x.astype(BF16)
    r = x - hi.astype(F32)
    mid = r.astype(BF16)
    lo = (r - mid.astype(F32)).astype(BF16)
    return hi, mid, lo


def _rms_mod(x, gain, shift, scale):
    y = x * lax.rsqrt(jnp.mean(x * x, axis=-1, keepdims=True) + NORM_EPS)
    return (y * gain) * (1.0 + scale) + shift


def _params(*sem):
    return pltpu.CompilerParams(dimension_semantics=sem, vmem_limit_bytes=VMEM_LIMIT_BYTES)


def _ada_kernel(c_ref, w_ref, b_ref, o_ref):
    a = _silu(c_ref[...]).astype(BF16)
    o_ref[...] = _dot(a, w_ref[...].astype(BF16)) + b_ref[...]


def _ada_proj(c, w, b, tn):
    m, d = c.shape
    n = w.shape[1]
    return pl.pallas_call(
        _ada_kernel,
        grid=(n // tn,),
        in_specs=[
            pl.BlockSpec((m, d), lambda j: (0, 0)),
            pl.BlockSpec((d, tn), lambda j: (0, j)),
            pl.BlockSpec((1, tn), lambda j: (0, j)),
        ],
        out_specs=pl.BlockSpec((m, tn), lambda j: (0, j)),
        out_shape=jax.ShapeDtypeStruct((m, n), F32),
        compiler_params=_params("parallel"),
        name="ada_proj",
    )(c, w, b.reshape(1, n))


def _ffn_kernel(x_ref, sh_ref, sc_ref, gt_ref, gain_ref, wg_ref, wu_ref, wd_ref,
                gain2_ref, sh2_ref, sc2_ref, *rest, emit_x):
    if emit_x:
        xo_ref, hn_ref, h_scr = rest
        acc_ref = xo_ref
    else:
        hn_ref, h_scr = rest
        acc_ref = hn_ref
    j = pl.program_id(1)
    tm = x_ref.shape[0]
    n_tiles = tm // ROW_TILE

    @pl.when(j == 0)
    def _():
        def body(r, carry):
            rows = pl.ds(pl.multiple_of(r * ROW_TILE, ROW_TILE), ROW_TILE)
            h = _rms_mod(x_ref[rows, :], gain_ref[...], sh_ref[...], sc_ref[...])
            h_scr[rows, :] = h.astype(BF16)
            acc_ref[rows, :] = jnp.zeros((ROW_TILE, acc_ref.shape[1]), F32)
            return carry
        lax.fori_loop(0, n_tiles, body, 0)

    wg = wg_ref[...].astype(BF16)
    wu = wu_ref[...].astype(BF16)
    wd = wd_ref[...].astype(BF16)
    sub = min(tm, FFN_SUB_ROWS)
    for r in range(tm // sub):
        rows = slice(r * sub, (r + 1) * sub)
        h = h_scr[rows, :]
        a = (_silu(_dot(h, wg)) * _dot(h, wu)).astype(BF16)
        acc_ref[rows, :] += _dot(a, wd)

    @pl.when(j == pl.num_programs(1) - 1)
    def _():
        def body(r, carry):
            rows = pl.ds(pl.multiple_of(r * ROW_TILE, ROW_TILE), ROW_TILE)
            xo = x_ref[rows, :] + (MACARON_WEIGHT * gt_ref[...]) * acc_ref[rows, :]
            if emit_x:
                xo_ref[rows, :] = xo
            hn = _rms_mod(xo, gain2_ref[...], sh2_ref[...], sc2_ref[...])
            hn_ref[rows, :] = hn.astype(hn_ref.dtype)
            return carry
        lax.fori_loop(0, n_tiles, body, 0)


def _mod_spec(mod, k, d, rows_per_group, tm):
    if mod.ndim == 3:
        blocks_per_group = rows_per_group // tm
        return pl.BlockSpec((None, 1, d), lambda i, j: (i // blocks_per_group, 0, k))
    return pl.BlockSpec((ROW_TILE, d), lambda i, j: (0, k))


def _ffn(x, mod, ks, gain, wg, wu, wd, mod2, ks2, gain2, *, rows_per_group, tm, tf, emit_x, hn_dtype):
    m, d = x.shape
    f = wg.shape[1]
    k_sh, k_sc, k_gt = ks
    k_sh2, k_sc2 = ks2
    vec = lambda: pl.BlockSpec((1, d), lambda i, j: (0, 0))
    row_block = lambda: pl.BlockSpec((tm, d), lambda i, j: (i, 0))
    out_block = lambda: pl.BlockSpec((tm, d), lambda i, j: (i, 0), pipeline_mode=pl.Buffered(1))
    out_shape = [jax.ShapeDtypeStruct((m, d), hn_dtype)]
    out_specs = [out_block()]
    if emit_x:
        out_shape = [jax.ShapeDtypeStruct((m, d), F32)] + out_shape
        out_specs = [out_block()] + out_specs
    return pl.pallas_call(
        functools.partial(_ffn_kernel, emit_x=emit_x),
        grid=(m // tm, f // tf),
        in_specs=[
            row_block(),
            _mod_spec(mod, k_sh, d, rows_per_group, tm),
            _mod_spec(mod, k_sc, d, rows_per_group, tm),
            _mod_spec(mod, k_gt, d, rows_per_group, tm),
            vec(),
            pl.BlockSpec((d, tf), lambda i, j: (0, j)),
            pl.BlockSpec((d, tf), lambda i, j: (0, j)),
            pl.BlockSpec((tf, d), lambda i, j: (j, 0)),
            vec(),
            _mod_spec(mod2, k_sh2, d, rows_per_group, tm),
            _mod_spec(mod2, k_sc2, d, rows_per_group, tm),
        ],
        out_specs=out_specs,
        out_shape=out_shape,
        scratch_shapes=[pltpu.VMEM((tm, d), BF16)],
        compiler_params=_params("parallel", "arbitrary"),
        name="ffn",
    )(x, mod, mod, mod, gain.reshape(1, d), wg, wu, wd, gain2.reshape(1, d), mod2, mod2)


def _proj_kernel(h_ref, w_ref, o_ref, wb_scr):
    @pl.when(pl.program_id(1) == 0)
    def _():
        wb_scr[...] = w_ref[...].astype(BF16)

    o_ref[...] = _dot(h_ref[...], wb_scr[...])


def _proj(h, w, col0, n_cols, tm, tn):
    m, d = h.shape
    blk0 = col0 // tn
    return pl.pallas_call(
        _proj_kernel,
        grid=(n_cols // tn, m // tm),
        in_specs=[
            pl.BlockSpec((tm, d), lambda j, i: (i, 0)),
            pl.BlockSpec((d, tn), lambda j, i: (0, blk0 + j)),
        ],
        out_specs=pl.BlockSpec((tm, tn), lambda j, i: (i, j)),
        out_shape=jax.ShapeDtypeStruct((m, n_cols), F32),
        scratch_shapes=[pltpu.VMEM((d, tn), BF16)],
        compiler_params=_params("parallel", "arbitrary"),
        name="in_proj",
    )(h, w)


def _outproj_kernel(x_ref, gt_ref, og_ref, or_ref, w_ref, o_ref, wb_scr):
    @pl.when(pl.program_id(0) == 0)
    def _():
        wb_scr[...] = w_ref[...].astype(BF16)

    kw = og_ref.shape[1]
    tm = x_ref.shape[0]
    mixed = _dot(og_ref[...], wb_scr[:kw, :]) + _dot(or_ref[...], wb_scr[kw:, :])
    for r in range(tm // ROW_TILE):
        rows = slice(r * ROW_TILE, (r + 1) * ROW_TILE)
        o_ref[rows, :] = x_ref[rows, :] + gt_ref[...] * mixed[rows, :]


def _outproj(x, mod, k_gt, og, orr, w_out, *, rows_per_group, tm):
    m, d = x.shape
    kw = og.shape[1]
    if mod.ndim == 3:
        blocks_per_group = rows_per_group // tm
        gt_spec = pl.BlockSpec((None, 1, d), lambda i: (i // blocks_per_group, 0, k_gt))
    else:
        gt_spec = pl.BlockSpec((ROW_TILE, d), lambda i: (0, k_gt))
    return pl.pallas_call(
        _outproj_kernel,
        grid=(m // tm,),
        in_specs=[
            pl.BlockSpec((tm, d), lambda i: (i, 0)),
            gt_spec,
            pl.BlockSpec((tm, kw), lambda i: (i, 0)),
            pl.BlockSpec((tm, kw), lambda i: (i, 0)),
            pl.BlockSpec(w_out.shape, lambda i: (0, 0), pipeline_mode=pl.Buffered(1)),
        ],
        out_specs=pl.BlockSpec((tm, d), lambda i: (i, 0)),
        out_shape=jax.ShapeDtypeStruct((m, d), F32),
        scratch_shapes=[pltpu.VMEM(w_out.shape, BF16)],
        compiler_params=_params("arbitrary"),
        name="out_proj",
    )(x, mod, og, orr, w_out)


SOLVE_BLOCK = 16


def _nilpotent_apply(chains, index):
    power = 1
    while power < index:
        square = 2 * power < index
        for ch in chains:
            right = jnp.concatenate([ch["sol"], ch["nil"]], axis=1) if square else ch["sol"]
            ch["prod"] = _dot(ch["nil"].astype(BF16), right.astype(BF16))
        for ch in chains:
            width = ch["sol"].shape[1]
            upd = ch["prod"][:, :width]
            ch["sol"] = ch["sol"] - upd if power == 1 else ch["sol"] + upd
            if square:
                ch["nil"] = ch["prod"][:, width:]
        power *= 2


def _gdn_kernel(qkv_ref, z_ref, ba_ref, cinit_ref, s0_ref, convw_ref, alog_ref, dtb_ref, nw_ref,
                o_ref, snew_ref, cnew_ref, tail_scr, s_scr, *, valid_rows, heads, dk, dv):
    n = pl.program_id(1)
    last = pl.num_programs(1) - 1
    nb, c, _ = qkv_ref.shape
    hist = SUBLANES

    @pl.when(n == 0)
    def _():
        tail_scr[...] = cinit_ref[...]
        s_scr[...] = s0_ref[...]

    row = lax.broadcasted_iota(jnp.int32, (c, c), 0)
    col = lax.broadcasted_iota(jnp.int32, (c, c), 1)
    causal = row >= col
    strict = row > col
    diag = row == col
    tril = jnp.where(causal, 1.0, 0.0).astype(BF16)
    w = convw_ref[...]
    nw = nw_ref[...]
    neg_a = -jnp.exp(alog_ref[...])
    dtb = dtb_ref[...]

    chains = []
    for i in range(nb):
        x = qkv_ref[i]
        xp = jnp.concatenate([tail_scr[i], x], axis=0)
        acc = x * w[CONV_WIDTH - 1:CONV_WIDTH, :]
        for sft in range(1, CONV_WIDTH):
            tap = CONV_WIDTH - 1 - sft
            acc = acc + pltpu.roll(xp, sft, 0)[hist:, :] * w[tap:tap + 1, :]
        qkv = _silu(acc)
        tail_scr[i] = x[c - hist:, :]

        @pl.when(n == last)
        def _(i=i, xp=xp):
            cnew_ref[i] = xp[valid_rows:valid_rows + hist, :]

        ba = ba_ref[i]
        beta_all = _sigmoid(ba)
        g_all = neg_a * _softplus(ba + dtb)
        if valid_rows < c:
            live = lax.broadcasted_iota(jnp.int32, ba.shape, 0) < valid_rows
            beta_all = jnp.where(live, beta_all, 0.0)
            g_all = jnp.where(live, g_all, 0.0)
        g_hi, g_mid, g_lo = _split3(g_all)
        gc_all = _dot(tril, g_hi) + (_dot(tril, g_mid) + _dot(tril, g_lo))
        gc_last_all = gc_all[c - 1:c, :]
        eg_all = jnp.exp(gc_all)
        kdec_all = jnp.exp(gc_last_all - gc_all)
        gl_all = jnp.exp(gc_last_all)

        for h in range(heads):
            beta = beta_all[:, h:h + 1]
            gc = gc_all[:, heads + h:heads + h + 1]
            eg = eg_all[:, heads + h:heads + h + 1]
            q = qkv[:, h * dk:(h + 1) * dk]
            k = qkv[:, heads * dk + h * dk:heads * dk + (h + 1) * dk]
            v = qkv[:, 2 * heads * dk + h * dv:2 * heads * dk + (h + 1) * dv]
            q = q * (lax.rsqrt(jnp.sum(q * q, axis=-1, keepdims=True) + L2_EPS) * (dk ** -0.5))
            k = k * lax.rsqrt(jnp.sum(k * k, axis=-1, keepdims=True) + L2_EPS)
            kb = k * beta
            gc_col = jnp.broadcast_to(gc, (c, c))
            gc_row = jnp.sum(jnp.where(diag, gc_col, 0.0), axis=0, keepdims=True)
            decay = jnp.where(causal, jnp.exp(gc_col - gc_row), 0.0)
            chains.append(dict(
                i=i, h=h, decay=decay,
                kbq=jnp.concatenate([kb, q], axis=0).astype(BF16), k_bf=k.astype(BF16),
                rhs=jnp.concatenate([v * beta, kb * eg], axis=1),
                q_dec=q * eg,
                k_dec=(k * kdec_all[:, heads + h:heads + h + 1]).astype(BF16),
                gl=gl_all[:, heads + h:heads + h + 1]))

    for ch in chains:
        ch["kq"] = _dot_nt(ch["kbq"], ch["k_bf"])
    for ch in chains:
        ch["a"] = jnp.where(strict, ch["kq"][:c] * ch["decay"], 0.0)
        ch["qk"] = (ch["kq"][c:] * ch["decay"]).astype(BF16)

    if valid_rows <= SOLVE_BLOCK:
        for ch in chains:
            ch["nil"], ch["sol"] = ch["a"], ch["rhs"]
        _nilpotent_apply(chains, valid_rows)
    else:
        same_block = (row // SOLVE_BLOCK) == (col // SOLVE_BLOCK)
        eye = jnp.where(diag, 1.0, 0.0)
        for ch in chains:
            ch["nil"] = jnp.where(same_block, ch["a"], 0.0)
            ch["sol"] = eye
        _nilpotent_apply(chains, SOLVE_BLOCK)
        for ch in chains:
            off_block = jnp.where(same_block, 0.0, ch["a"])
            right = jnp.concatenate([ch["rhs"], off_block], axis=1)
            ch["prod"] = _dot(ch["sol"].astype(BF16), right.astype(BF16))
        for ch in chains:
            ch["sol"] = ch["prod"][:, :dv + dk]
            ch["nil"] = ch["prod"][:, dv + dk:]
        _nilpotent_apply(chains, c // SOLVE_BLOCK)

    for ch in chains:
        ch["s"] = s_scr[ch["i"], ch["h"]]
        lhs = jnp.concatenate([ch["sol"][:, dv:], ch["q_dec"]], axis=0).astype(BF16)
        ch["ws_qs"] = _dot(lhs, ch["s"].astype(BF16))
    for ch in chains:
        ch["v_new"] = (ch["sol"][:, :dv] - ch["ws_qs"][:c]).astype(BF16)
    for ch in chains:
        s_scr[ch["i"], ch["h"]] = ch["s"] * ch["gl"] + _dot_tn(ch["k_dec"], ch["v_new"])
        ch["o"] = ch["ws_qs"][c:] + _dot(ch["qk"], ch["v_new"])
    for ch in chains:
        i, h, o = ch["i"], ch["h"], ch["o"]
        o = o * lax.rsqrt(jnp.mean(o * o, axis=-1, keepdims=True) + NORM_EPS) * nw
        o = o * _silu(z_ref[i, :, h * dv:(h + 1) * dv])
        o_ref[i, :, h * dv:(h + 1) * dv] = o.astype(o_ref.dtype)

    @pl.when(n == last)
    def _():
        snew_ref[...] = s_scr[...]


def _gdn(proj, proj_ba, conv_init, s0, conv_w, a_log, dt_bias, norm_w, *, chunk, valid_rows, nb, heads, dk, dv):
    b, t, _ = proj.shape
    assert valid_rows <= SOLVE_BLOCK or (valid_rows == chunk and chunk % SOLVE_BLOCK == 0)
    conv_ch = heads * (2 * dk + dv)
    width = heads * dv
    n_chunks = t // chunk
    qkv_blk = 0
    z_blk = conv_ch // width
    ba_blk = 0
    lane_vec = lambda x: jnp.zeros((1, LANES), F32).at[0, heads:2 * heads].set(x.astype(F32))
    return pl.pallas_call(
        functools.partial(_gdn_kernel, valid_rows=valid_rows, heads=heads, dk=dk, dv=dv),
        grid=(b // nb, n_chunks),
        in_specs=[
            pl.BlockSpec((nb, chunk, conv_ch), lambda i, n: (i, n, qkv_blk)),
            pl.BlockSpec((nb, chunk, width), lambda i, n: (i, n, z_blk)),
            pl.BlockSpec((nb, chunk, LANES), lambda i, n: (i, n, ba_blk)),
            pl.BlockSpec((nb, SUBLANES, conv_ch), lambda i, n: (i, 0, 0)),
            pl.BlockSpec((nb, heads, dk, dv), lambda i, n: (i, 0, 0, 0)),
            pl.BlockSpec((CONV_WIDTH, conv_ch), lambda i, n: (0, 0)),
            pl.BlockSpec((1, LANES), lambda i, n: (0, 0)),
            pl.BlockSpec((1, LANES), lambda i, n: (0, 0)),
            pl.BlockSpec((1, dv), lambda i, n: (0, 0)),
        ],
        out_specs=[
            pl.BlockSpec((nb, chunk, width), lambda i, n: (i, n, 0)),
            pl.BlockSpec((nb, heads, dk, dv), lambda i, n: (i, 0, 0, 0)),
            pl.BlockSpec((nb, SUBLANES, conv_ch), lambda i, n: (i, 0, 0)),
        ],
        out_shape=[
            jax.ShapeDtypeStruct((b, t, width), BF16),
            jax.ShapeDtypeStruct((b, heads, dk, dv), F32),
            jax.ShapeDtypeStruct((b, SUBLANES, conv_ch), F32),
        ],
        scratch_shapes=[
            pltpu.VMEM((nb, SUBLANES, conv_ch), F32),
            pltpu.VMEM((nb, heads, dk, dv), F32),
        ],
        compiler_params=_params("parallel", "arbitrary"),
        name="gdn_mixer",
    )(proj, proj, proj_ba, conv_init, s0, conv_w, lane_vec(a_log), lane_vec(dt_bias), norm_w.reshape(1, dv))


def _ret_kernel(q_ref, k_ref, v_ref, g_ref, s0_ref, invf_ref, gnw_ref, gnb_ref,
                o_ref, snew_ref, s_scr, *, valid_rows, pos0, heads, dk, dv):
    n = pl.program_id(1)
    nb, c, _ = q_ref.shape
    half = dk // 2

    @pl.when(n == 0)
    def _():
        s_scr[...] = s0_ref[...]

    t_idx = lax.broadcasted_iota(jnp.int32, (c, dk), 0)
    pos = (pos0 + n * c + t_idx).astype(F32)
    ang = pos * invf_ref[...]
    cos2 = jnp.cos(ang)
    lane = lax.broadcasted_iota(jnp.int32, (c, dk), 1)
    sin2 = jnp.where(lane < half, -jnp.sin(ang), jnp.sin(ang))

    row = lax.broadcasted_iota(jnp.int32, (c, c), 0)
    col = lax.broadcasted_iota(jnp.int32, (c, c), 1)
    rel = (row - col).astype(F32)
    idx = lax.broadcasted_iota(jnp.int32, (c, 1), 0).astype(F32)
    live = lax.broadcasted_iota(jnp.int32, (c, 1), 0) < valid_rows

    chains = []
    for h in range(heads):
        log_gamma = math.log(1.0 - 2.0 ** (-5.0 - h))
        dmat = jnp.where(rel >= 0, jnp.exp(jnp.maximum(rel, 0.0) * log_gamma), 0.0)
        q_scale = jnp.exp((idx + 1.0) * log_gamma)
        k_scale = jnp.exp((valid_rows - 1.0 - idx) * log_gamma)
        for i in range(nb):
            q = q_ref[i, :, h * dk:(h + 1) * dk]
            k = k_ref[i, :, h * dk:(h + 1) * dk]
            v = v_ref[i, :, h * dv:(h + 1) * dv]
            if valid_rows < c:
                v = jnp.where(live, v, 0.0)
            q = q * cos2 + pltpu.roll(q, half, 1) * sin2
            k = (k * cos2 + pltpu.roll(k, half, 1) * sin2) * (dk ** -0.5)
            chains.append(dict(
                i=i, h=h, dmat=dmat, s_decay=math.exp(valid_rows * log_gamma),
                q_bf=q.astype(BF16), k_bf=k.astype(BF16), v_bf=v.astype(BF16),
                q_dec=(q * q_scale).astype(BF16), k_dec=(k * k_scale).astype(BF16)))

    for ch in chains:
        ch["inner"] = (_dot_nt(ch["q_bf"], ch["k_bf"]) * ch["dmat"]).astype(BF16)
    for ch in chains:
        s = s_scr[ch["i"], ch["h"]]
        ch["o"] = _dot(ch["q_dec"], s.astype(BF16)) + _dot(ch["inner"], ch["v_bf"])
        s_scr[ch["i"], ch["h"]] = s * ch["s_decay"] + _dot_tn(ch["k_dec"], ch["v_bf"])
    for ch in chains:
        i, h, o = ch["i"], ch["h"], ch["o"]
        mu = jnp.mean(o, axis=-1, keepdims=True)
        oc = o - mu
        var = jnp.mean(oc * oc, axis=-1, keepdims=True)
        o = oc * lax.rsqrt(var + NORM_EPS)
        o = o * gnw_ref[:, h * dv:(h + 1) * dv] + gnb_ref[:, h * dv:(h + 1) * dv]
        o = o * _silu(g_ref[i, :, h * dv:(h + 1) * dv])
        o_ref[i, :, h * dv:(h + 1) * dv] = o.astype(o_ref.dtype)

    @pl.when(n == pl.num_programs(1) - 1)
    def _():
        snew_ref[...] = s_scr[...]


def _ret(proj, col0, s0, gn_w, gn_b, *, chunk, valid_rows, nb, pos0, heads, dk, dv):
    b, t, _ = proj.shape
    width = heads * dv
    n_chunks = t // chunk
    blk0 = col0 // width
    half = dk // 2
    inv_freq = ROPE_BASE ** (-jnp.arange(half, dtype=F32) / half)
    inv_freq2 = jnp.concatenate([inv_freq, inv_freq]).reshape(1, dk)
    col_spec = lambda kk: pl.BlockSpec((nb, chunk, width), lambda i, n: (i, n, blk0 + kk))
    return pl.pallas_call(
        functools.partial(_ret_kernel, valid_rows=valid_rows, pos0=pos0, heads=heads, dk=dk, dv=dv),
        grid=(b // nb, n_chunks),
        in_specs=[
            col_spec(0), col_spec(1), col_spec(2), col_spec(3),
            pl.BlockSpec((nb, heads, dk, dv), lambda i, n: (i, 0, 0, 0)),
            pl.BlockSpec((1, dk), lambda i, n: (0, 0)),
            pl.BlockSpec((1, width), lambda i, n: (0, 0)),
            pl.BlockSpec((1, width), lambda i, n: (0, 0)),
        ],
        out_specs=[
            pl.BlockSpec((nb, chunk, width), lambda i, n: (i, n, 0)),
            pl.BlockSpec((nb, heads, dk, dv), lambda i, n: (i, 0, 0, 0)),
        ],
        out_shape=[
            jax.ShapeDtypeStruct((b, t, width), BF16),
            jax.ShapeDtypeStruct((b, heads, dk, dv), F32),
        ],
        scratch_shapes=[pltpu.VMEM((nb, heads, dk, dv), F32)],
        compiler_params=_params("parallel", "arbitrary"),
        name="ret_mixer",
    )(proj, proj, proj, proj, s0, inv_freq2, gn_w.reshape(1, width), gn_b.reshape(1, width))


def _trunk(x, mod, mod_f, rows_per_group, seq_rows, valid_rows, nb, pos0, to_seq, from_seq,
           conv_init, s_gdn0, s_ret0, w, tm, tf):
    heads, dk, dv = s_gdn0.shape[1:]
    x1, h2 = _ffn(x, mod, (0, 1, 2), w["norm_ffn1"], w["w1_gate"], w["w1_up"], w["w1_down"],
                  mod, (3, 4), w["norm_mix"], rows_per_group=rows_per_group, tm=tm, tf=tf,
                  emit_x=True, hn_dtype=BF16)
    conv_ch = heads * (2 * dk + dv)
    width = heads * dv
    tn = 512
    proj_g = to_seq(_proj(h2, w["w_in"], 0, conv_ch + width, tm, tn))
    proj_ba = to_seq(_proj(h2, w["w_in"], conv_ch + width, LANES, tm, LANES))
    proj_r = to_seq(_proj(h2, w["w_in_ret"], 0, 4 * width, tm, tn))
    chunk = min(CHUNK, seq_rows)
    o_gdn, s_gdn, conv_new = _gdn(proj_g, proj_ba, conv_init, s_gdn0, w["conv_w"], w["a_log"], w["dt_bias"],
                                  w["gdn_norm_w"], chunk=chunk, valid_rows=valid_rows, nb=nb,
                                  heads=heads, dk=dk, dv=dv)
    o_ret, s_ret = _ret(proj_r, 0, s_ret0, w["ret_gn_w"], w["ret_gn_b"],
                        chunk=chunk, valid_rows=valid_rows, nb=nb, pos0=pos0, heads=heads, dk=dk, dv=dv)
    o_gdn = from_seq(o_gdn)
    o_ret = from_seq(o_ret)
    x2 = _outproj(x1, mod, 5, o_gdn, o_ret, w["w_out"], rows_per_group=rows_per_group, tm=min(tm, 512))
    (y,) = _ffn(x2, mod, (6, 7, 8), w["norm_ffn2"], w["w2_gate"], w["w2_up"], w["w2_down"],
                mod_f, (0, 1), w["norm_final"], rows_per_group=rows_per_group, tm=tm, tf=tf,
                emit_x=False, hn_dtype=F32)
    return y, s_gdn, conv_new[:, SUBLANES - (CONV_WIDTH - 1):, :], s_ret


def kernel(x_prompt, x_sample, state_gdn, state_conv, state_ret, c_prompt, c_sample, w_ada, b_ada, norm_ffn1, w1_gate, w1_up, w1_down, norm_mix, w_in, conv_w, a_log, dt_bias, gdn_norm_w, ret_gn_w, ret_gn_b, w_out, norm_ffn2, w2_gate, w2_up, w2_down, w_ada_final, b_ada_final, norm_final):
    bp, tp, d = x_prompt.shape
    bs, ts, _ = x_sample.shape
    depth, _, heads, dk, dv = state_gdn.shape
    assert depth == 1, "single-layer trunk"
    assert bs == ROW_TILE, "time-major sample rows must align adaLN vectors with row tiles"
    conv_ch = state_conv.shape[-1]
    width = heads * dv
    ba0 = conv_ch + width
    ret0 = ba0 + 2 * heads

    n_c = bs + bp
    n_c_pad = -(-n_c // SUBLANES) * SUBLANES
    c_all = jnp.concatenate([c_sample, c_prompt, jnp.zeros((n_c_pad - n_c, d), F32)], axis=0)
    ada = _ada_proj(c_all, w_ada[0], b_ada[0], tn=512)
    ada_f = _ada_proj(c_all, w_ada_final, b_ada_final, tn=512)
    mod_s, mod_p = ada, ada[bs:bs + bp].reshape(bp, 1, N_ADA * d)
    modf_s, modf_p = ada_f, ada_f[bs:bs + bp].reshape(bp, 1, 2 * d)

    assert ba0 % LANES == 0 and w_in.shape[-1] >= ba0 + LANES
    w = dict(norm_ffn1=norm_ffn1[0], w1_gate=w1_gate[0], w1_up=w1_up[0], w1_down=w1_down[0],
             norm_mix=norm_mix[0], w_in=w_in[0], w_in_ret=w_in[0][:, ret0:], conv_w=conv_w[0], a_log=a_log[0],
             dt_bias=dt_bias[0], gdn_norm_w=gdn_norm_w[0], ret_gn_w=ret_gn_w[0], ret_gn_b=ret_gn_b[0],
             w_out=w_out[0], norm_ffn2=norm_ffn2[0], w2_gate=w2_gate[0], w2_up=w2_up[0],
             w2_down=w2_down[0], norm_final=norm_final)

    zeros_state = jnp.zeros((bp, heads, dk, dv), F32)
    y_p, gdn_p, conv_p, ret_p = _trunk(
        x_prompt.reshape(bp * tp, d), mod_p, modf_p, tp, tp, min(CHUNK, tp), 2, 0,
        lambda a: a.reshape(bp, tp, a.shape[-1]), lambda a: a.reshape(bp * tp, a.shape[-1]),
        jnp.zeros((bp, SUBLANES, conv_ch), F32), zeros_state, zeros_state, w, tm=1024, tf=256)

    ts_pad = -(-ts // SUBLANES) * SUBLANES
    to_seq = lambda a: jnp.pad(a.reshape(ts, bs, a.shape[-1]).transpose(1, 0, 2),
                               ((0, 0), (0, ts_pad - ts), (0, 0)))
    from_seq = lambda a: a[:, :ts].transpose(1, 0, 2).reshape(ts * bs, a.shape[-1])
    conv_init = jnp.pad(state_conv[0], ((0, 0), (SUBLANES - (CONV_WIDTH - 1), 0), (0, 0)))
    y_s, gdn_s, conv_s, ret_s = _trunk(
        x_sample.transpose(1, 0, 2).reshape(ts * bs, d), mod_s, modf_s, None, ts_pad, ts, 8, PAST_LEN,
        to_seq, from_seq, conv_init, state_gdn[0], state_ret[0], w, tm=ts * bs, tf=256)
    y_s = y_s.reshape(ts, bs, d).transpose(1, 0, 2)

    return (y_p.reshape(bp, tp, d), y_s, gdn_p[None], conv_p[None], ret_p[None],
            gdn_s[None], conv_s[None], ret_s[None])
```

```python
import functools
import math

import jax
import jax.numpy as jnp
from jax import lax
from jax.experimental import pallas as pl
from jax.experimental.pallas import tpu as pltpu

F32 = jnp.float32
BF16 = jnp.bfloat16

LANES = 128
SUBLANES = 8
VMEM_LIMIT_BYTES = 60 * 1024 * 1024

PAST_LEN = 16384
CONV_WIDTH = 4
CHUNK = 64
ROPE_BASE = 10000.0
NORM_EPS = 1e-6
L2_EPS = 1e-6
MACARON_WEIGHT = 0.5
N_ADA = 9
ROW_TILE = 128
FFN_SUB_ROWS = 512
PROJ_COLS = 1024


def _sigmoid(x):
    return 1.0 / (1.0 + jnp.exp(-x))


def _silu(x):
    return x * _sigmoid(x)


def _softplus(x):
    return jnp.maximum(x, 0.0) + jnp.log(1.0 + jnp.exp(-jnp.abs(x)))


def _dot(a, b):
    return jnp.dot(a, b, preferred_element_type=F32)


def _dot_nt(a, b):
    return lax.dot_general(a, b, (((1,), (1,)), ((), ())), preferred_element_type=F32)


def _dot_tn(a, b):
    return lax.dot_general(a, b, (((0,), (0,)), ((), ())), preferred_element_type=F32)


def _split3(x):
    hi = x.astype(BF16)
    r = x - hi.astype(F32)
    mid = r.astype(BF16)
    lo = (r - mid.astype(F32)).astype(BF16)
    return hi, mid, lo


def _rms_mod(x, gain, shift, scale):
    y = x * lax.rsqrt(jnp.mean(x * x, axis=-1, keepdims=True) + NORM_EPS)
    return (y * gain) * (1.0 + scale) + shift


def _params(*sem):
    return pltpu.CompilerParams(dimension_semantics=sem, vmem_limit_bytes=VMEM_LIMIT_BYTES)


def _ada_kernel(c_ref, w_ref, b_ref, o_ref):
    a = _silu(c_ref[...]).astype(BF16)
    o_ref[...] = _dot(a, w_ref[...].astype(BF16)) + b_ref[...]


def _ada_proj(c, w, b, tn):
    m, d = c.shape
    n = w.shape[1]
    return pl.pallas_call(
        _ada_kernel,
        grid=(n // tn,),
        in_specs=[
            pl.BlockSpec((m, d), lambda j: (0, 0)),
            pl.BlockSpec((d, tn), lambda j: (0, j)),
            pl.BlockSpec((1, tn), lambda j: (0, j)),
        ],
        out_specs=pl.BlockSpec((m, tn), lambda j: (0, j)),
        out_shape=jax.ShapeDtypeStruct((m, n), F32),
        compiler_params=_params("parallel"),
        name="ada_proj",
    )(c, w, b.reshape(1, n))


def _ffn_kernel(x_ref, sh_ref, sc_ref, gt_ref, gain_ref, wg_ref, wu_ref, wd_ref,
                gain2_ref, sh2_ref, sc2_ref, *rest, emit_x):
    if emit_x:
        xo_ref, hn_ref, h_scr = rest
        acc_ref = xo_ref
    else:
        hn_ref, h_scr = rest
        acc_ref = hn_ref
    j = pl.program_id(1)
    tm = x_ref.shape[0]
    n_tiles = tm // ROW_TILE

    @pl.when(j == 0)
    def _():
        def body(r, carry):
            rows = pl.ds(pl.multiple_of(r * ROW_TILE, ROW_TILE), ROW_TILE)
            h = _rms_mod(x_ref[rows, :], gain_ref[...], sh_ref[...], sc_ref[...])
            h_scr[rows, :] = h.astype(BF16)
            acc_ref[rows, :] = jnp.zeros((ROW_TILE, acc_ref.shape[1]), F32)
            return carry
        lax.fori_loop(0, n_tiles, body, 0)

    wg = wg_ref[...].astype(BF16)
    wu = wu_ref[...].astype(BF16)
    wd = wd_ref[...].astype(BF16)
    sub = min(tm, FFN_SUB_ROWS)
    for r in range(tm // sub):
        rows = slice(r * sub, (r + 1) * sub)
        h = h_scr[rows, :]
        a = (_silu(_dot(h, wg)) * _dot(h, wu)).astype(BF16)
        acc_ref[rows, :] += _dot(a, wd)

    @pl.when(j == pl.num_programs(1) - 1)
    def _():
        def body(r, carry):
            rows = pl.ds(pl.multiple_of(r * ROW_TILE, ROW_TILE), ROW_TILE)
            xo = x_ref[rows, :] + (MACARON_WEIGHT * gt_ref[...]) * acc_ref[rows, :]
            if emit_x:
                xo_ref[rows, :] = xo
            hn = _rms_mod(xo, gain2_ref[...], sh2_ref[...], sc2_ref[...])
            hn_ref[rows, :] = hn.astype(hn_ref.dtype)
            return carry
        lax.fori_loop(0, n_tiles, body, 0)


def _mod_spec(mod, k, d, rows_per_group, tm):
    if mod.ndim == 3:
        blocks_per_group = rows_per_group // tm
        return pl.BlockSpec((None, 1, d), lambda i, j: (i // blocks_per_group, 0, k))
    return pl.BlockSpec((ROW_TILE, d), lambda i, j: (0, k))


def _ffn(x, mod, ks, gain, wg, wu, wd, mod2, ks2, gain2, *, rows_per_group, tm, tf, emit_x, hn_dtype):
    m, d = x.shape
    f = wg.shape[1]
    k_sh, k_sc, k_gt = ks
    k_sh2, k_sc2 = ks2
    vec = lambda: pl.BlockSpec((1, d), lambda i, j: (0, 0))
    row_block = lambda: pl.BlockSpec((tm, d), lambda i, j: (i, 0))
    out_block = lambda: pl.BlockSpec((tm, d), lambda i, j: (i, 0), pipeline_mode=pl.Buffered(1))
    out_shape = [jax.ShapeDtypeStruct((m, d), hn_dtype)]
    out_specs = [out_block()]
    if emit_x:
        out_shape = [jax.ShapeDtypeStruct((m, d), F32)] + out_shape
        out_specs = [out_block()] + out_specs
    return pl.pallas_call(
        functools.partial(_ffn_kernel, emit_x=emit_x),
        grid=(m // tm, f // tf),
        in_specs=[
            out_block(),
            _mod_spec(mod, k_sh, d, rows_per_group, tm),
            _mod_spec(mod, k_sc, d, rows_per_group, tm),
            _mod_spec(mod, k_gt, d, rows_per_group, tm),
            vec(),
            pl.BlockSpec((d, tf), lambda i, j: (0, j)),
            pl.BlockSpec((d, tf), lambda i, j: (0, j)),
            pl.BlockSpec((tf, d), lambda i, j: (j, 0)),
            vec(),
            _mod_spec(mod2, k_sh2, d, rows_per_group, tm),
            _mod_spec(mod2, k_sc2, d, rows_per_group, tm),
        ],
        out_specs=out_specs,
        out_shape=out_shape,
        scratch_shapes=[pltpu.VMEM((tm, d), BF16)],
        compiler_params=_params("parallel", "arbitrary"),
        name="ffn",
    )(x, mod, mod, mod, gain.reshape(1, d), wg, wu, wd, gain2.reshape(1, d), mod2, mod2)


def _proj_kernel(h_ref, wt_ref, o_ref, wb_scr):
    @pl.when(pl.program_id(1) == 0)
    def _():
        wb_scr[...] = wt_ref[...].astype(BF16)

    o_ref[...] = _dot_nt(h_ref[...], wb_scr[...])


def _proj(h, wt, row_of, n_cols, tm, tn):
    m, d = h.shape
    return pl.pallas_call(
        _proj_kernel,
        grid=(n_cols // tn, m // tm),
        in_specs=[
            pl.BlockSpec((tm, d), lambda j, i: (i, 0)),
            pl.BlockSpec((pl.Element(tn), pl.Element(d)),
                         lambda j, i: (pl.multiple_of(row_of(j), SUBLANES), 0)),
        ],
        out_specs=pl.BlockSpec((tm, tn), lambda j, i: (i, j)),
        out_shape=jax.ShapeDtypeStruct((m, n_cols), F32),
        scratch_shapes=[pltpu.VMEM((tn, d), BF16)],
        compiler_params=_params("parallel", "arbitrary"),
        name="in_proj",
    )(h, wt)


def _outproj_kernel(x_ref, gt_ref, og_ref, or_ref, w_ref, o_ref, wb_scr):
    @pl.when(pl.program_id(0) == 0)
    def _():
        wb_scr[...] = w_ref[...].astype(BF16)

    kw = og_ref.shape[1]
    tm = x_ref.shape[0]
    mixed = _dot(og_ref[...], wb_scr[:kw, :]) + _dot(or_ref[...], wb_scr[kw:, :])
    for r in range(tm // ROW_TILE):
        rows = slice(r * ROW_TILE, (r + 1) * ROW_TILE)
        o_ref[rows, :] = x_ref[rows, :] + gt_ref[...] * mixed[rows, :]


def _outproj(x, mod, k_gt, og, orr, w_out, *, rows_per_group, tm):
    m, d = x.shape
    kw = og.shape[1]
    if mod.ndim == 3:
        blocks_per_group = rows_per_group // tm
        gt_spec = pl.BlockSpec((None, 1, d), lambda i: (i // blocks_per_group, 0, k_gt))
    else:
        gt_spec = pl.BlockSpec((ROW_TILE, d), lambda i: (0, k_gt))
    return pl.pallas_call(
        _outproj_kernel,
        grid=(m // tm,),
        in_specs=[
            pl.BlockSpec((tm, d), lambda i: (i, 0)),
            gt_spec,
            pl.BlockSpec((tm, kw), lambda i: (i, 0)),
            pl.BlockSpec((tm, kw), lambda i: (i, 0)),
            pl.BlockSpec(w_out.shape, lambda i: (0, 0), pipeline_mode=pl.Buffered(1)),
        ],
        out_specs=pl.BlockSpec((tm, d), lambda i: (i, 0)),
        out_shape=jax.ShapeDtypeStruct((m, d), F32),
        scratch_shapes=[pltpu.VMEM(w_out.shape, BF16)],
        compiler_params=_params("arbitrary"),
        name="out_proj",
    )(x, mod, og, orr, w_out)


SOLVE_BLOCK = 16


def _nilpotent_apply(chains, index):
    power = 1
    while power < index:
        square = 2 * power < index
        for ch in chains:
            right = jnp.concatenate([ch["sol"], ch["nil"]], axis=1) if square else ch["sol"]
            ch["prod"] = _dot(ch["nil"].astype(BF16), right.astype(BF16))
        for ch in chains:
            width = ch["sol"].shape[1]
            upd = ch["prod"][:, :width]
            ch["sol"] = ch["sol"] - upd if power == 1 else ch["sol"] + upd
            if square:
                ch["nil"] = ch["prod"][:, width:]
        power *= 2


def _gdn_kernel(qkv_ref, z_ref, ba_ref, cinit_ref, s0_ref, convw_ref, alog_ref, dtb_ref, nw_ref,
                o_ref, snew_ref, cnew_ref, tail_scr, s_scr, *, valid_rows, heads, dk, dv):
    n = pl.program_id(1)
    last = pl.num_programs(1) - 1
    nb, c, _ = qkv_ref.shape
    hist = SUBLANES

    @pl.when(n == 0)
    def _():
        tail_scr[...] = cinit_ref[...]
        s_scr[...] = s0_ref[...]

    row = lax.broadcasted_iota(jnp.int32, (c, c), 0)
    col = lax.broadcasted_iota(jnp.int32, (c, c), 1)
    causal = row >= col
    strict = row > col
    diag = row == col
    tril = jnp.where(causal, 1.0, 0.0).astype(BF16)
    w = convw_ref[...]
    nw = nw_ref[...]
    neg_a = -jnp.exp(alog_ref[...])
    dtb = dtb_ref[...]

    chains = []
    for i in range(nb):
        x = qkv_ref[i]
        xp = jnp.concatenate([tail_scr[i], x], axis=0)
        acc = x * w[CONV_WIDTH - 1:CONV_WIDTH, :]
        for sft in range(1, CONV_WIDTH):
            tap = CONV_WIDTH - 1 - sft
            acc = acc + pltpu.roll(xp, sft, 0)[hist:, :] * w[tap:tap + 1, :]
        qkv = _silu(acc)
        tail_scr[i] = x[c - hist:, :]

        @pl.when(n == last)
        def _(i=i, xp=xp):
            cnew_ref[i] = xp[valid_rows:valid_rows + hist, :]

        ba = ba_ref[i]
        beta_all = _sigmoid(ba)
        g_all = neg_a * _softplus(ba + dtb)
        if valid_rows < c:
            live = lax.broadcasted_iota(jnp.int32, ba.shape, 0) < valid_rows
            beta_all = jnp.where(live, beta_all, 0.0)
            g_all = jnp.where(live, g_all, 0.0)
        g_hi, g_mid, g_lo = _split3(g_all)
        gc_all = _dot(tril, g_hi) + (_dot(tril, g_mid) + _dot(tril, g_lo))
        gc_last_all = gc_all[c - 1:c, :]
        eg_all = jnp.exp(gc_all)
        kdec_all = jnp.exp(gc_last_all - gc_all)
        gl_all = jnp.exp(gc_last_all)

        for h in range(heads):
            beta = beta_all[:, h:h + 1]
            gc = gc_all[:, heads + h:heads + h + 1]
            eg = eg_all[:, heads + h:heads + h + 1]
            q = qkv[:, h * dk:(h + 1) * dk]
            k = qkv[:, heads * dk + h * dk:heads * dk + (h + 1) * dk]
            v = qkv[:, 2 * heads * dk + h * dv:2 * heads * dk + (h + 1) * dv]
            q = q * (lax.rsqrt(jnp.sum(q * q, axis=-1, keepdims=True) + L2_EPS) * (dk ** -0.5))
            k = k * lax.rsqrt(jnp.sum(k * k, axis=-1, keepdims=True) + L2_EPS)
            kb = k * beta
            gc_col = jnp.broadcast_to(gc, (c, c))
            gc_row = jnp.sum(jnp.where(diag, gc_col, 0.0), axis=0, keepdims=True)
            decay = jnp.where(causal, jnp.exp(gc_col - gc_row), 0.0)
            chains.append(dict(
                i=i, h=h, decay=decay,
                kbq=jnp.concatenate([kb, q], axis=0).astype(BF16), k_bf=k.astype(BF16),
                rhs=jnp.concatenate([v * beta, kb * eg], axis=1),
                q_dec=q * eg,
                k_dec=(k * kdec_all[:, heads + h:heads + h + 1]).astype(BF16),
                gl=gl_all[:, heads + h:heads + h + 1]))

    for ch in chains:
        ch["kq"] = _dot_nt(ch["kbq"], ch["k_bf"])
    for ch in chains:
        ch["a"] = jnp.where(strict, ch["kq"][:c] * ch["decay"], 0.0)
        ch["qk"] = (ch["kq"][c:] * ch["decay"]).astype(BF16)

    if valid_rows <= SOLVE_BLOCK:
        for ch in chains:
            ch["nil"], ch["sol"] = ch["a"], ch["rhs"]
        _nilpotent_apply(chains, valid_rows)
    else:
        same_block = (row // SOLVE_BLOCK) == (col // SOLVE_BLOCK)
        eye = jnp.where(diag, 1.0, 0.0)
        for ch in chains:
            ch["nil"] = jnp.where(same_block, ch["a"], 0.0)
            ch["sol"] = eye
        _nilpotent_apply(chains, SOLVE_BLOCK)
        for ch in chains:
            off_block = jnp.where(same_block, 0.0, ch["a"])
            right = jnp.concatenate([ch["rhs"], off_block], axis=1)
            ch["prod"] = _dot(ch["sol"].astype(BF16), right.astype(BF16))
        for ch in chains:
            ch["sol"] = ch["prod"][:, :dv + dk]
            ch["nil"] = ch["prod"][:, dv + dk:]
        _nilpotent_apply(chains, c // SOLVE_BLOCK)

    for ch in chains:
        ch["s"] = s_scr[ch["i"], ch["h"]]
        lhs = jnp.concatenate([ch["sol"][:, dv:], ch["q_dec"]], axis=0).astype(BF16)
        ch["ws_qs"] = _dot(lhs, ch["s"].astype(BF16))
    for ch in chains:
        ch["v_new"] = (ch["sol"][:, :dv] - ch["ws_qs"][:c]).astype(BF16)
    for ch in chains:
        s_scr[ch["i"], ch["h"]] = ch["s"] * ch["gl"] + _dot_tn(ch["k_dec"], ch["v_new"])
        ch["o"] = ch["ws_qs"][c:] + _dot(ch["qk"], ch["v_new"])
    for ch in chains:
        i, h, o = ch["i"], ch["h"], ch["o"]
        o = o * lax.rsqrt(jnp.mean(o * o, axis=-1, keepdims=True) + NORM_EPS) * nw
        o = o * _silu(z_ref[i, :, h * dv:(h + 1) * dv])
        o_ref[i, :, h * dv:(h + 1) * dv] = o.astype(o_ref.dtype)

    @pl.when(n == last)
    def _():
        snew_ref[...] = s_scr[...]


def _gdn(proj, proj_ba, conv_init, s0, conv_w, a_log, dt_bias, norm_w, *, chunk, valid_rows, nb, heads, dk, dv):
    b, t, _ = proj.shape
    assert valid_rows <= SOLVE_BLOCK or (valid_rows == chunk and chunk % SOLVE_BLOCK == 0)
    conv_ch = heads * (2 * dk + dv)
    width = heads * dv
    n_chunks = t // chunk
    qkv_blk = 0
    z_blk = conv_ch // width
    ba_blk = 0
    lane_vec = lambda x: jnp.zeros((1, LANES), F32).at[0, heads:2 * heads].set(x.astype(F32))
    return pl.pallas_call(
        functools.partial(_gdn_kernel, valid_rows=valid_rows, heads=heads, dk=dk, dv=dv),
        grid=(b // nb, n_chunks),
        in_specs=[
            pl.BlockSpec((nb, chunk, conv_ch), lambda i, n: (i, n, qkv_blk)),
            pl.BlockSpec((nb, chunk, width), lambda i, n: (i, n, z_blk)),
            pl.BlockSpec((nb, chunk, LANES), lambda i, n: (i, n, ba_blk)),
            pl.BlockSpec((nb, SUBLANES, conv_ch), lambda i, n: (i, 0, 0)),
            pl.BlockSpec((nb, heads, dk, dv), lambda i, n: (i, 0, 0, 0)),
            pl.BlockSpec((CONV_WIDTH, conv_ch), lambda i, n: (0, 0)),
            pl.BlockSpec((1, LANES), lambda i, n: (0, 0)),
            pl.BlockSpec((1, LANES), lambda i, n: (0, 0)),
            pl.BlockSpec((1, dv), lambda i, n: (0, 0)),
        ],
        out_specs=[
            pl.BlockSpec((nb, chunk, width), lambda i, n: (i, n, 0)),
            pl.BlockSpec((nb, heads, dk, dv), lambda i, n: (i, 0, 0, 0)),
            pl.BlockSpec((nb, SUBLANES, conv_ch), lambda i, n: (i, 0, 0)),
        ],
        out_shape=[
            jax.ShapeDtypeStruct((b, t, width), BF16),
            jax.ShapeDtypeStruct((b, heads, dk, dv), F32),
            jax.ShapeDtypeStruct((b, SUBLANES, conv_ch), F32),
        ],
        scratch_shapes=[
            pltpu.VMEM((nb, SUBLANES, conv_ch), F32),
            pltpu.VMEM((nb, heads, dk, dv), F32),
        ],
        compiler_params=_params("parallel", "arbitrary"),
        name="gdn_mixer",
    )(proj, proj, proj_ba, conv_init, s0, conv_w, lane_vec(a_log), lane_vec(dt_bias), norm_w.reshape(1, dv))


def _ret_kernel(q_ref, k_ref, v_ref, g_ref, s0_ref, invf_ref, gnw_ref, gnb_ref,
                o_ref, snew_ref, s_scr, *, valid_rows, pos0, heads, dk, dv):
    n = pl.program_id(1)
    nb, c, _ = q_ref.shape
    half = dk // 2

    @pl.when(n == 0)
    def _():
        s_scr[...] = s0_ref[...]

    t_idx = lax.broadcasted_iota(jnp.int32, (c, dk), 0)
    pos = (pos0 + n * c + t_idx).astype(F32)
    ang = pos * invf_ref[...]
    cos2 = jnp.cos(ang)
    lane = lax.broadcasted_iota(jnp.int32, (c, dk), 1)
    sin2 = jnp.where(lane < half, -jnp.sin(ang), jnp.sin(ang))

    row = lax.broadcasted_iota(jnp.int32, (c, c), 0)
    col = lax.broadcasted_iota(jnp.int32, (c, c), 1)
    rel = (row - col).astype(F32)
    idx = lax.broadcasted_iota(jnp.int32, (c, 1), 0).astype(F32)
    live = lax.broadcasted_iota(jnp.int32, (c, 1), 0) < valid_rows

    chains = []
    for h in range(heads):
        log_gamma = math.log(1.0 - 2.0 ** (-5.0 - h))
        dmat = jnp.where(rel >= 0, jnp.exp(jnp.maximum(rel, 0.0) * log_gamma), 0.0)
        q_scale = jnp.exp((idx + 1.0) * log_gamma)
        k_scale = jnp.exp((valid_rows - 1.0 - idx) * log_gamma)
        for i in range(nb):
            q = q_ref[i, :, h * dk:(h + 1) * dk]
            k = k_ref[i, :, h * dk:(h + 1) * dk]
            v = v_ref[i, :, h * dv:(h + 1) * dv]
            if valid_rows < c:
                v = jnp.where(live, v, 0.0)
            q = q * cos2 + pltpu.roll(q, half, 1) * sin2
            k = (k * cos2 + pltpu.roll(k, half, 1) * sin2) * (dk ** -0.5)
            chains.append(dict(
                i=i, h=h, dmat=dmat, s_decay=math.exp(valid_rows * log_gamma),
                q_bf=q.astype(BF16), k_bf=k.astype(BF16), v_bf=v.astype(BF16),
                q_dec=(q * q_scale).astype(BF16), k_dec=(k * k_scale).astype(BF16)))

    for ch in chains:
        ch["inner"] = (_dot_nt(ch["q_bf"], ch["k_bf"]) * ch["dmat"]).astype(BF16)
    for ch in chains:
        s = s_scr[ch["i"], ch["h"]]
        ch["o"] = _dot(ch["q_dec"], s.astype(BF16)) + _dot(ch["inner"], ch["v_bf"])
        s_scr[ch["i"], ch["h"]] = s * ch["s_decay"] + _dot_tn(ch["k_dec"], ch["v_bf"])
    for ch in chains:
        i, h, o = ch["i"], ch["h"], ch["o"]
        mu = jnp.mean(o, axis=-1, keepdims=True)
        oc = o - mu
        var = jnp.mean(oc * oc, axis=-1, keepdims=True)
        o = oc * lax.rsqrt(var + NORM_EPS)
        o = o * gnw_ref[:, h * dv:(h + 1) * dv] + gnb_ref[:, h * dv:(h + 1) * dv]
        o = o * _silu(g_ref[i, :, h * dv:(h + 1) * dv])
        o_ref[i, :, h * dv:(h + 1) * dv] = o.astype(o_ref.dtype)

    @pl.when(n == pl.num_programs(1) - 1)
    def _():
        snew_ref[...] = s_scr[...]


def _ret(proj, col0, s0, gn_w, gn_b, *, chunk, valid_rows, nb, pos0, heads, dk, dv):
    b, t, _ = proj.shape
    width = heads * dv
    n_chunks = t // chunk
    blk0 = col0 // width
    half = dk // 2
    inv_freq = ROPE_BASE ** (-jnp.arange(half, dtype=F32) / half)
    inv_freq2 = jnp.concatenate([inv_freq, inv_freq]).reshape(1, dk)
    col_spec = lambda kk: pl.BlockSpec((nb, chunk, width), lambda i, n: (i, n, blk0 + kk))
    return pl.pallas_call(
        functools.partial(_ret_kernel, valid_rows=valid_rows, pos0=pos0, heads=heads, dk=dk, dv=dv),
        grid=(b // nb, n_chunks),
        in_specs=[
            col_spec(0), col_spec(1), col_spec(2), col_spec(3),
            pl.BlockSpec((nb, heads, dk, dv), lambda i, n: (i, 0, 0, 0)),
            pl.BlockSpec((1, dk), lambda i, n: (0, 0)),
            pl.BlockSpec((1, width), lambda i, n: (0, 0)),
            pl.BlockSpec((1, width), lambda i, n: (0, 0)),
        ],
        out_specs=[
            pl.BlockSpec((nb, chunk, width), lambda i, n: (i, n, 0)),
            pl.BlockSpec((nb, heads, dk, dv), lambda i, n: (i, 0, 0, 0)),
        ],
        out_shape=[
            jax.ShapeDtypeStruct((b, t, width), BF16),
            jax.ShapeDtypeStruct((b, heads, dk, dv), F32),
        ],
        scratch_shapes=[pltpu.VMEM((nb, heads, dk, dv), F32)],
        compiler_params=_params("parallel", "arbitrary"),
        name="ret_mixer",
    )(proj, proj, proj, proj, s0, inv_freq2, gn_w.reshape(1, width), gn_b.reshape(1, width))


def _trunk(x, mod, mod_f, rows_per_group, seq_rows, valid_rows, nb, pos0, to_seq, from_seq,
           conv_init, s_gdn0, s_ret0, w, tm, tf):
    heads, dk, dv = s_gdn0.shape[1:]
    x1, h2 = _ffn(x, mod, (0, 1, 2), w["norm_ffn1"], w["w1_gate"], w["w1_up"], w["w1_down"],
                  mod, (3, 4), w["norm_mix"], rows_per_group=rows_per_group, tm=tm, tf=tf,
                  emit_x=True, hn_dtype=BF16)
    conv_ch = heads * (2 * dk + dv)
    width = heads * dv
    ba0 = conv_ch + width
    tn = PROJ_COLS
    skip = lambda j: j * tn + jnp.where(j * tn >= ba0, 2 * heads, 0)
    proj = to_seq(_proj(h2, w["w_in_t"], skip, ba0 + 4 * width, tm, tn))
    proj_ba = to_seq(_proj(h2, w["w_in_t"], lambda j: ba0, LANES, tm, LANES))
    chunk = min(CHUNK, seq_rows)
    o_gdn, s_gdn, conv_new = _gdn(proj, proj_ba, conv_init, s_gdn0, w["conv_w"], w["a_log"], w["dt_bias"],
                                  w["gdn_norm_w"], chunk=chunk, valid_rows=valid_rows, nb=nb,
                                  heads=heads, dk=dk, dv=dv)
    o_ret, s_ret = _ret(proj, ba0, s_ret0, w["ret_gn_w"], w["ret_gn_b"],
                        chunk=chunk, valid_rows=valid_rows, nb=nb, pos0=pos0, heads=heads, dk=dk, dv=dv)
    o_gdn = from_seq(o_gdn)
    o_ret = from_seq(o_ret)
    x2 = _outproj(x1, mod, 5, o_gdn, o_ret, w["w_out"], rows_per_group=rows_per_group, tm=min(tm, 512))
    (y,) = _ffn(x2, mod, (6, 7, 8), w["norm_ffn2"], w["w2_gate"], w["w2_up"], w["w2_down"],
                mod_f, (0, 1), w["norm_final"], rows_per_group=rows_per_group, tm=tm, tf=tf,
                emit_x=False, hn_dtype=F32)
    return y, s_gdn, conv_new[:, SUBLANES - (CONV_WIDTH - 1):, :], s_ret


def kernel(x_prompt, x_sample, state_gdn, state_conv, state_ret, c_prompt, c_sample, w_ada, b_ada, norm_ffn1, w1_gate, w1_up, w1_down, norm_mix, w_in, conv_w, a_log, dt_bias, gdn_norm_w, ret_gn_w, ret_gn_b, w_out, norm_ffn2, w2_gate, w2_up, w2_down, w_ada_final, b_ada_final, norm_final):
    bp, tp, d = x_prompt.shape
    bs, ts, _ = x_sample.shape
    depth, _, heads, dk, dv = state_gdn.shape
    assert depth == 1, "single-layer trunk"
    assert bs == ROW_TILE, "time-major sample rows must align adaLN vectors with row tiles"
    conv_ch = state_conv.shape[-1]
    width = heads * dv
    ba0 = conv_ch + width
    ret0 = ba0 + 2 * heads

    n_c = bs + bp
    n_c_pad = -(-n_c // SUBLANES) * SUBLANES
    c_all = jnp.concatenate([c_sample, c_prompt, jnp.zeros((n_c_pad - n_c, d), F32)], axis=0)
    ada = _ada_proj(c_all, w_ada[0], b_ada[0], tn=512)
    ada_f = _ada_proj(c_all, w_ada_final, b_ada_final, tn=512)
    mod_s, mod_p = ada, ada[bs:bs + bp].reshape(bp, 1, N_ADA * d)
    modf_s, modf_p = ada_f, ada_f[bs:bs + bp].reshape(bp, 1, 2 * d)

    assert ba0 % PROJ_COLS == 0 and ret0 % SUBLANES == 0 and w_in.shape[-1] == ret0 + 4 * width
    w = dict(norm_ffn1=norm_ffn1[0], w1_gate=w1_gate[0], w1_up=w1_up[0], w1_down=w1_down[0],
             norm_mix=norm_mix[0], w_in_t=w_in[0].T, conv_w=conv_w[0], a_log=a_log[0],
             dt_bias=dt_bias[0], gdn_norm_w=gdn_norm_w[0], ret_gn_w=ret_gn_w[0], ret_gn_b=ret_gn_b[0],
             w_out=w_out[0], norm_ffn2=norm_ffn2[0], w2_gate=w2_gate[0], w2_up=w2_up[0],
             w2_down=w2_down[0], norm_final=norm_final)

    zeros_state = jnp.zeros((bp, heads, dk, dv), F32)
    y_p, gdn_p, conv_p, ret_p = _trunk(
        x_prompt.reshape(bp * tp, d), mod_p, modf_p, tp, tp, min(CHUNK, tp), 2, 0,
        lambda a: a.reshape(bp, tp, a.shape[-1]), lambda a: a.reshape(bp * tp, a.shape[-1]),
        jnp.zeros((bp, SUBLANES, conv_ch), F32), zeros_state, zeros_state, w, tm=1024, tf=512)

    ts_pad = -(-ts // SUBLANES) * SUBLANES
    to_seq = lambda a: jnp.pad(a.reshape(ts, bs, a.shape[-1]).transpose(1, 0, 2),
                               ((0, 0), (0, ts_pad - ts), (0, 0)))
    from_seq = lambda a: a[:, :ts].transpose(1, 0, 2).reshape(ts * bs, a.shape[-1])
    conv_init = jnp.pad(state_conv[0], ((0, 0), (SUBLANES - (CONV_WIDTH - 1), 0), (0, 0)))
    y_s, gdn_s, conv_s, ret_s = _trunk(
        x_sample.transpose(1, 0, 2).reshape(ts * bs, d), mod_s, modf_s, None, ts_pad, ts, 8, PAST_LEN,
        to_seq, from_seq, conv_init, state_gdn[0], state_ret[0], w, tm=ts * bs, tf=512)
    y_s = y_s.reshape(ts, bs, d).transpose(1, 0, 2)

    return (y_p.reshape(bp, tp, d), y_s, gdn_p[None], conv_p[None], ret_p[None],
            gdn_s[None], conv_s[None], ret_s[None])
```

```python
import functools
import math

import jax
import jax.numpy as jnp
from jax import lax
from jax.experimental import pallas as pl
from jax.experimental.pallas import tpu as pltpu

F32 = jnp.float32
BF16 = jnp.bfloat16

LANES = 128
SUBLANES = 8
MXU_COLS = 256
VMEM_LIMIT_BYTES = 60 * 1024 * 1024

PAST_LEN = 16384
CONV_WIDTH = 4
CHUNK = 64
ROPE_BASE = 10000.0
NORM_EPS = 1e-6
L2_EPS = 1e-6
MACARON_WEIGHT = 0.5
N_ADA = 9
ROW_TILE = 128
NORM_ROWS = 16
FFN_SUB_ROWS = 512
PROJ_COLS = 1024
HISTORY_ROWS = SUBLANES
CONV_SUB_ROWS = 64


def _sigmoid(x):
    return 1.0 / (1.0 + jnp.exp(-x))


def _silu(x):
    return x * _sigmoid(x)


def _softplus(x):
    return jnp.maximum(x, 0.0) + jnp.log(1.0 + jnp.exp(-jnp.abs(x)))


def _dot(a, b):
    return jnp.dot(a, b, preferred_element_type=F32)


def _dot_nt(a, b):
    return lax.dot_general(a, b, (((1,), (1,)), ((), ())), preferred_element_type=F32)


def _dot_tn(a, b):
    return lax.dot_general(a, b, (((0,), (0,)), ((), ())), preferred_element_type=F32)


def _split3(x):
    hi = x.astype(BF16)
    r = x - hi.astype(F32)
    mid = r.astype(BF16)
    lo = (r - mid.astype(F32)).astype(BF16)
    return hi, mid, lo


def _rms_mod(x, gain, shift, scale):
    y = x * lax.rsqrt(jnp.mean(x * x, axis=-1, keepdims=True) + NORM_EPS)
    return (y * gain) * (1.0 + scale) + shift


def _mod_rows(ref, g):
    if ref.shape[0] == 1:
        return ref[...]
    return ref[g * NORM_ROWS:(g + 1) * NORM_ROWS, :]


def _for_row_groups(n_rows, fn):
    def tile(t, carry):
        base = pl.multiple_of(t * ROW_TILE, ROW_TILE)
        for g in range(ROW_TILE // NORM_ROWS):
            fn(pl.ds(base + g * NORM_ROWS, NORM_ROWS), g)
        return carry
    lax.fori_loop(0, n_rows // ROW_TILE, tile, 0)


def _params(*sem):
    return pltpu.CompilerParams(dimension_semantics=sem, vmem_limit_bytes=VMEM_LIMIT_BYTES)


def _ada_kernel(c_ref, w_ref, b_ref, o_ref):
    a = _silu(c_ref[...]).astype(BF16)
    o_ref[...] = _dot(a, w_ref[...].astype(BF16)) + b_ref[...]


def _ada_proj(c, w, b, tn):
    m, d = c.shape
    n = w.shape[1]
    return pl.pallas_call(
        _ada_kernel,
        grid=(n // tn,),
        in_specs=[
            pl.BlockSpec((m, d), lambda j: (0, 0)),
            pl.BlockSpec((d, tn), lambda j: (0, j)),
            pl.BlockSpec((1, tn), lambda j: (0, j)),
        ],
        out_specs=pl.BlockSpec((m, tn), lambda j: (0, j)),
        out_shape=jax.ShapeDtypeStruct((m, n), F32),
        compiler_params=_params("parallel"),
        name="ada_proj",
    )(c, w, b.reshape(1, n))


def _ffn_kernel(x_ref, sh_ref, sc_ref, gt_ref, gain_ref, wg_ref, wu_ref, wd_ref,
                gain2_ref, sh2_ref, sc2_ref, *rest, emit_x):
    if emit_x:
        xo_ref, hn_ref, h_scr = rest
        acc_ref = xo_ref
    else:
        hn_ref, h_scr, acc_ref = rest
    j = pl.program_id(1)
    tm = x_ref.shape[0]

    @pl.when(j == 0)
    def _():
        def group(rows, g):
            h = _rms_mod(x_ref[rows, :], gain_ref[...], _mod_rows(sh_ref, g), _mod_rows(sc_ref, g))
            h_scr[rows, :] = h.astype(BF16)
            acc_ref[rows, :] = jnp.zeros((NORM_ROWS, acc_ref.shape[1]), F32)
        _for_row_groups(tm, group)

    wg = wg_ref[...].astype(BF16)
    wu = wu_ref[...].astype(BF16)
    wd = wd_ref[...].astype(BF16)
    sub = min(tm, FFN_SUB_ROWS)
    for r in range(tm // sub):
        rows = slice(r * sub, (r + 1) * sub)
        h = h_scr[rows, :]
        a = (_silu(_dot(h, wg)) * _dot(h, wu)).astype(BF16)
        acc_ref[rows, :] += _dot(a, wd)

    @pl.when(j == pl.num_programs(1) - 1)
    def _():
        def group(rows, g):
            xo = x_ref[rows, :] + (MACARON_WEIGHT * _mod_rows(gt_ref, g)) * acc_ref[rows, :]
            if emit_x:
                xo_ref[rows, :] = xo
            hn = _rms_mod(xo, gain2_ref[...], _mod_rows(sh2_ref, g), _mod_rows(sc2_ref, g))
            hn_ref[rows, :] = hn.astype(hn_ref.dtype)
        _for_row_groups(tm, group)


def _mod_spec(mod, k, d, rows_per_group, tm):
    if mod.ndim == 3:
        blocks_per_group = rows_per_group // tm
        return pl.BlockSpec((None, 1, d), lambda i, j: (i // blocks_per_group, 0, k))
    return pl.BlockSpec((ROW_TILE, d), lambda i, j: (0, k))


def _ffn(x, mod, ks, gain, wg, wu, wd, mod2, ks2, gain2, *, rows_per_group, tm, tf, emit_x, hn_dtype):
    m, d = x.shape
    f = wg.shape[1]
    k_sh, k_sc, k_gt = ks
    k_sh2, k_sc2 = ks2
    vec = lambda: pl.BlockSpec((1, d), lambda i, j: (0, 0))
    row_block = lambda: pl.BlockSpec((tm, d), lambda i, j: (i, 0))
    out_block = lambda: pl.BlockSpec((tm, d), lambda i, j: (i, 0), pipeline_mode=pl.Buffered(1))
    out_shape = [jax.ShapeDtypeStruct((m, d), hn_dtype)]
    out_specs = [out_block()]
    if emit_x:
        out_shape = [jax.ShapeDtypeStruct((m, d), F32)] + out_shape
        out_specs = [out_block()] + out_specs
    return pl.pallas_call(
        functools.partial(_ffn_kernel, emit_x=emit_x),
        grid=(m // tm, f // tf),
        in_specs=[
            row_block(),
            _mod_spec(mod, k_sh, d, rows_per_group, tm),
            _mod_spec(mod, k_sc, d, rows_per_group, tm),
            _mod_spec(mod, k_gt, d, rows_per_group, tm),
            vec(),
            pl.BlockSpec((d, tf), lambda i, j: (0, j)),
            pl.BlockSpec((d, tf), lambda i, j: (0, j)),
            pl.BlockSpec((tf, d), lambda i, j: (j, 0)),
            vec(),
            _mod_spec(mod2, k_sh2, d, rows_per_group, tm),
            _mod_spec(mod2, k_sc2, d, rows_per_group, tm),
        ],
        out_specs=out_specs,
        out_shape=out_shape,
        scratch_shapes=[pltpu.VMEM((tm, d), BF16)] + ([] if emit_x else [pltpu.VMEM((tm, d), F32)]),
        compiler_params=_params("parallel", "arbitrary"),
        name="ffn",
    )(x, mod, mod, mod, gain.reshape(1, d), wg, wu, wd, gain2.reshape(1, d), mod2, mod2)


def _proj_kernel(h_ref, wt_ref, o_ref, wb_scr):
    @pl.when(pl.program_id(1) == 0)
    def _():
        wb_scr[...] = wt_ref[...].astype(BF16)

    o_ref[...] = _dot_nt(h_ref[...], wb_scr[...])


def _proj(h, wt, row_of, n_cols, tm, tn):
    m, d = h.shape
    return pl.pallas_call(
        _proj_kernel,
        grid=(n_cols // tn, m // tm),
        in_specs=[
            pl.BlockSpec((tm, d), lambda j, i: (i, 0)),
            pl.BlockSpec((pl.Element(tn), pl.Element(d)),
                         lambda j, i: (pl.multiple_of(row_of(j), SUBLANES), 0)),
        ],
        out_specs=pl.BlockSpec((tm, tn), lambda j, i: (i, j)),
        out_shape=jax.ShapeDtypeStruct((m, n_cols), F32),
        scratch_shapes=[pltpu.VMEM((tn, d), BF16)],
        compiler_params=_params("parallel", "arbitrary"),
        name="in_proj",
    )(h, wt)


def _proj_conv_kernel(h_ref, wt_ref, convw_ref, o_ref, cnew_ref, wb_scr, tail_scr, *, blocks_per_seq):
    i = pl.program_id(1)

    @pl.when(i == 0)
    def _():
        wb_scr[...] = wt_ref[...].astype(BF16)

    h = h_ref[...]
    tm, tn = o_ref.shape
    first = i % blocks_per_seq == 0
    for cg in range(tn // MXU_COLS):
        cols = slice(cg * MXU_COLS, (cg + 1) * MXU_COLS)
        raw = _dot_nt(h, wb_scr[cols, :])
        w = convw_ref[:, cols]
        prev = jnp.where(first, 0.0, tail_scr[:, cols])
        for r in range(tm // CONV_SUB_ROWS):
            rows = slice(r * CONV_SUB_ROWS, (r + 1) * CONV_SUB_ROWS)
            x = raw[rows, :]
            y, _ = _causal_conv_silu(prev, x, w)
            o_ref[rows, cols] = y
            prev = x[CONV_SUB_ROWS - HISTORY_ROWS:, :]
        tail_scr[:, cols] = prev
        cnew_ref[:, cols] = prev


def _proj_conv(h, wt, conv_w, n_cols, rows_per_seq, tm, tn):
    m, d = h.shape
    blocks_per_seq = rows_per_seq // tm
    return pl.pallas_call(
        functools.partial(_proj_conv_kernel, blocks_per_seq=blocks_per_seq),
        grid=(n_cols // tn, m // tm),
        in_specs=[
            pl.BlockSpec((tm, d), lambda j, i: (i, 0)),
            pl.BlockSpec((tn, d), lambda j, i: (j, 0)),
            pl.BlockSpec((CONV_WIDTH, tn), lambda j, i: (0, j)),
        ],
        out_specs=[
            pl.BlockSpec((tm, tn), lambda j, i: (i, j)),
            pl.BlockSpec((None, HISTORY_ROWS, tn), lambda j, i: (i // blocks_per_seq, 0, j)),
        ],
        out_shape=[
            jax.ShapeDtypeStruct((m, n_cols), F32),
            jax.ShapeDtypeStruct((m // rows_per_seq, HISTORY_ROWS, n_cols), F32),
        ],
        scratch_shapes=[pltpu.VMEM((tn, d), BF16), pltpu.VMEM((HISTORY_ROWS, tn), F32)],
        compiler_params=_params("parallel", "arbitrary"),
        name="in_proj_conv",
    )(h, wt, conv_w)


def _outproj_kernel(x_ref, gt_ref, og_ref, or_ref, w_ref, o_ref, wb_scr):
    @pl.when(pl.program_id(0) == 0)
    def _():
        wb_scr[...] = w_ref[...].astype(BF16)

    kw = og_ref.shape[1]
    tm = x_ref.shape[0]
    mixed = _dot(og_ref[...], wb_scr[:kw, :]) + _dot(or_ref[...], wb_scr[kw:, :])
    for r in range(tm // ROW_TILE):
        rows = slice(r * ROW_TILE, (r + 1) * ROW_TILE)
        o_ref[rows, :] = x_ref[rows, :] + gt_ref[...] * mixed[rows, :]


def _outproj(x, mod, k_gt, og, orr, w_out, *, rows_per_group, tm):
    m, d = x.shape
    kw = og.shape[1]
    if mod.ndim == 3:
        blocks_per_group = rows_per_group // tm
        gt_spec = pl.BlockSpec((None, 1, d), lambda i: (i // blocks_per_group, 0, k_gt))
    else:
        gt_spec = pl.BlockSpec((ROW_TILE, d), lambda i: (0, k_gt))
    return pl.pallas_call(
        _outproj_kernel,
        grid=(m // tm,),
        in_specs=[
            pl.BlockSpec((tm, d), lambda i: (i, 0)),
            gt_spec,
            pl.BlockSpec((tm, kw), lambda i: (i, 0)),
            pl.BlockSpec((tm, kw), lambda i: (i, 0)),
            pl.BlockSpec(w_out.shape, lambda i: (0, 0), pipeline_mode=pl.Buffered(1)),
        ],
        out_specs=pl.BlockSpec((tm, d), lambda i: (i, 0)),
        out_shape=jax.ShapeDtypeStruct((m, d), F32),
        scratch_shapes=[pltpu.VMEM(w_out.shape, BF16)],
        compiler_params=_params("arbitrary"),
        name="out_proj",
    )(x, mod, og, orr, w_out)


SOLVE_BLOCK = 16


def _nilpotent_apply(chains, index):
    power = 1
    while power < index:
        square = 2 * power < index
        for ch in chains:
            right = jnp.concatenate([ch["sol"], ch["nil"]], axis=1) if square else ch["sol"]
            ch["prod"] = _dot(ch["nil"].astype(BF16), right.astype(BF16))
        for ch in chains:
            width = ch["sol"].shape[1]
            upd = ch["prod"][:, :width]
            ch["sol"] = ch["sol"] - upd if power == 1 else ch["sol"] + upd
            if square:
                ch["nil"] = ch["prod"][:, width:]
        power *= 2


def _causal_conv_silu(prev, x, w):
    xp = jnp.concatenate([prev, x], axis=0)
    acc = x * w[CONV_WIDTH - 1:CONV_WIDTH, :]
    for sft in range(1, CONV_WIDTH):
        tap = CONV_WIDTH - 1 - sft
        acc = acc + pltpu.roll(xp, sft, 0)[HISTORY_ROWS:, :] * w[tap:tap + 1, :]
    return _silu(acc), xp


def _gdn_kernel(*refs, valid_rows, heads, dk, dv, conv_here):
    if conv_here:
        (qkv_ref, z_ref, ba_ref, s0_ref, alog_ref, dtb_ref, nw_ref, cinit_ref, convw_ref,
         o_ref, snew_ref, cnew_ref, s_scr, tail_scr) = refs
    else:
        qkv_ref, z_ref, ba_ref, s0_ref, alog_ref, dtb_ref, nw_ref, o_ref, snew_ref, s_scr = refs
    n = pl.program_id(1)
    last = pl.num_programs(1) - 1
    nb, c, _ = qkv_ref.shape

    @pl.when(n == 0)
    def _():
        s_scr[...] = s0_ref[...]
        if conv_here:
            tail_scr[...] = cinit_ref[...]

    row = lax.broadcasted_iota(jnp.int32, (c, c), 0)
    col = lax.broadcasted_iota(jnp.int32, (c, c), 1)
    causal = row >= col
    strict = row > col
    diag = row == col
    tril = jnp.where(causal, 1.0, 0.0).astype(BF16)
    nw = nw_ref[...]
    neg_a = -jnp.exp(alog_ref[...])
    dtb = dtb_ref[...]

    chains = []
    for i in range(nb):
        if conv_here:
            x = qkv_ref[i]
            qkv, xp = _causal_conv_silu(tail_scr[i], x, convw_ref[...])
            tail_scr[i] = x[c - HISTORY_ROWS:, :]

            @pl.when(n == last)
            def _(i=i, xp=xp):
                cnew_ref[i] = xp[valid_rows:valid_rows + HISTORY_ROWS, :]
        else:
            qkv = qkv_ref[i]

        ba = ba_ref[i]
        beta_all = _sigmoid(ba)
        g_all = neg_a * _softplus(ba + dtb)
        if valid_rows < c:
            live = lax.broadcasted_iota(jnp.int32, ba.shape, 0) < valid_rows
            beta_all = jnp.where(live, beta_all, 0.0)
            g_all = jnp.where(live, g_all, 0.0)
        g_hi, g_mid, g_lo = _split3(g_all)
        gc_all = _dot(tril, g_hi) + (_dot(tril, g_mid) + _dot(tril, g_lo))
        gc_last_all = gc_all[c - 1:c, :]
        eg_all = jnp.exp(gc_all)
        kdec_all = jnp.exp(gc_last_all - gc_all)
        gl_all = jnp.exp(gc_last_all)

        for h in range(heads):
            beta = beta_all[:, h:h + 1]
            gc = gc_all[:, heads + h:heads + h + 1]
            eg = eg_all[:, heads + h:heads + h + 1]
            q = qkv[:, h * dk:(h + 1) * dk]
            k = qkv[:, heads * dk + h * dk:heads * dk + (h + 1) * dk]
            v = qkv[:, 2 * heads * dk + h * dv:2 * heads * dk + (h + 1) * dv]
            q = q * (lax.rsqrt(jnp.sum(q * q, axis=-1, keepdims=True) + L2_EPS) * (dk ** -0.5))
            k = k * lax.rsqrt(jnp.sum(k * k, axis=-1, keepdims=True) + L2_EPS)
            kb = k * beta
            gc_col = jnp.broadcast_to(gc, (c, c))
            gc_row = jnp.sum(jnp.where(diag, gc_col, 0.0), axis=0, keepdims=True)
            decay = jnp.where(causal, jnp.exp(gc_col - gc_row), 0.0)
            chains.append(dict(
                i=i, h=h, decay=decay,
                kbq=jnp.concatenate([kb, q], axis=0).astype(BF16), k_bf=k.astype(BF16),
                rhs=jnp.concatenate([v * beta, kb * eg], axis=1),
                q_dec=q * eg,
                k_dec=(k * kdec_all[:, heads + h:heads + h + 1]).astype(BF16),
                gl=gl_all[:, heads + h:heads + h + 1]))

    for ch in chains:
        ch["kq"] = _dot_nt(ch["kbq"], ch["k_bf"])
    for ch in chains:
        ch["a"] = jnp.where(strict, ch["kq"][:c] * ch["decay"], 0.0)
        ch["qk"] = (ch["kq"][c:] * ch["decay"]).astype(BF16)

    if valid_rows <= SOLVE_BLOCK:
        for ch in chains:
            ch["nil"], ch["sol"] = ch["a"], ch["rhs"]
        _nilpotent_apply(chains, valid_rows)
    else:
        same_block = (row // SOLVE_BLOCK) == (col // SOLVE_BLOCK)
        eye = jnp.where(diag, 1.0, 0.0)
        for ch in chains:
            ch["nil"] = jnp.where(same_block, ch["a"], 0.0)
            ch["sol"] = eye
        _nilpotent_apply(chains, SOLVE_BLOCK)
        for ch in chains:
            off_block = jnp.where(same_block, 0.0, ch["a"])
            right = jnp.concatenate([ch["rhs"], off_block], axis=1)
            ch["prod"] = _dot(ch["sol"].astype(BF16), right.astype(BF16))
        for ch in chains:
            ch["sol"] = ch["prod"][:, :dv + dk]
            ch["nil"] = ch["prod"][:, dv + dk:]
        _nilpotent_apply(chains, c // SOLVE_BLOCK)

    for ch in chains:
        ch["s"] = s_scr[ch["i"], ch["h"]]
        lhs = jnp.concatenate([ch["sol"][:, dv:], ch["q_dec"]], axis=0).astype(BF16)
        ch["ws_qs"] = _dot(lhs, ch["s"].astype(BF16))
    for ch in chains:
        ch["v_new"] = (ch["sol"][:, :dv] - ch["ws_qs"][:c]).astype(BF16)
    for ch in chains:
        s_scr[ch["i"], ch["h"]] = ch["s"] * ch["gl"] + _dot_tn(ch["k_dec"], ch["v_new"])
        ch["o"] = ch["ws_qs"][c:] + _dot(ch["qk"], ch["v_new"])
    for ch in chains:
        i, h, o = ch["i"], ch["h"], ch["o"]
        o = o * lax.rsqrt(jnp.mean(o * o, axis=-1, keepdims=True) + NORM_EPS) * nw
        o = o * _silu(z_ref[i, :, h * dv:(h + 1) * dv])
        o_ref[i, :, h * dv:(h + 1) * dv] = o.astype(o_ref.dtype)

    @pl.when(n == last)
    def _():
        snew_ref[...] = s_scr[...]


def _gdn(qkv_src, z_src, proj_ba, s0, a_log, dt_bias, norm_w, conv=None, *, chunk, valid_rows, nb, heads, dk, dv):
    (qkv_arr, qkv_col), (z_arr, z_col) = qkv_src, z_src
    b, t, _ = qkv_arr.shape
    assert valid_rows <= SOLVE_BLOCK or (valid_rows == chunk and chunk % SOLVE_BLOCK == 0)
    conv_ch = heads * (2 * dk + dv)
    width = heads * dv
    n_chunks = t // chunk
    qkv_blk = qkv_col // conv_ch
    z_blk = z_col // width
    lane_vec = lambda x: jnp.zeros((1, LANES), F32).at[0, heads:2 * heads].set(x.astype(F32))
    state_spec = pl.BlockSpec((nb, heads, dk, dv), lambda i, n: (i, 0, 0, 0))
    hist_spec = pl.BlockSpec((nb, HISTORY_ROWS, conv_ch), lambda i, n: (i, 0, 0))
    lane_spec = pl.BlockSpec((1, LANES), lambda i, n: (0, 0))
    in_specs = [
        pl.BlockSpec((nb, chunk, conv_ch), lambda i, n: (i, n, qkv_blk)),
        pl.BlockSpec((nb, chunk, width), lambda i, n: (i, n, z_blk)),
        pl.BlockSpec((nb, chunk, LANES), lambda i, n: (i, n, 0)),
        state_spec, lane_spec, lane_spec,
        pl.BlockSpec((1, dv), lambda i, n: (0, 0)),
    ]
    args = [qkv_arr, z_arr, proj_ba, s0, lane_vec(a_log), lane_vec(dt_bias), norm_w.reshape(1, dv)]
    out_specs = [pl.BlockSpec((nb, chunk, width), lambda i, n: (i, n, 0)), state_spec]
    out_shape = [jax.ShapeDtypeStruct((b, t, width), BF16), jax.ShapeDtypeStruct((b, heads, dk, dv), F32)]
    scratch = [pltpu.VMEM((nb, heads, dk, dv), F32)]
    if conv is not None:
        conv_init, conv_w = conv
        in_specs += [hist_spec, pl.BlockSpec((CONV_WIDTH, conv_ch), lambda i, n: (0, 0))]
        args += [conv_init, conv_w]
        out_specs.append(hist_spec)
        out_shape.append(jax.ShapeDtypeStruct((b, HISTORY_ROWS, conv_ch), F32))
        scratch.append(pltpu.VMEM((nb, HISTORY_ROWS, conv_ch), F32))
    return pl.pallas_call(
        functools.partial(_gdn_kernel, valid_rows=valid_rows, heads=heads, dk=dk, dv=dv,
                          conv_here=conv is not None),
        grid=(b // nb, n_chunks),
        in_specs=in_specs,
        out_specs=out_specs,
        out_shape=out_shape,
        scratch_shapes=scratch,
        compiler_params=_params("parallel", "arbitrary"),
        name="gdn_mixer",
    )(*args)


def _ret_kernel(q_ref, k_ref, v_ref, g_ref, s0_ref, invf_ref, gnw_ref, gnb_ref,
                o_ref, snew_ref, s_scr, *, valid_rows, pos0, heads, dk, dv):
    n = pl.program_id(1)
    nb, c, _ = q_ref.shape
    half = dk // 2

    @pl.when(n == 0)
    def _():
        s_scr[...] = s0_ref[...]

    t_idx = lax.broadcasted_iota(jnp.int32, (c, dk), 0)
    pos = (pos0 + n * c + t_idx).astype(F32)
    ang = pos * invf_ref[...]
    cos2 = jnp.cos(ang)
    lane = lax.broadcasted_iota(jnp.int32, (c, dk), 1)
    sin2 = jnp.where(lane < half, -jnp.sin(ang), jnp.sin(ang))

    row = lax.broadcasted_iota(jnp.int32, (c, c), 0)
    col = lax.broadcasted_iota(jnp.int32, (c, c), 1)
    rel = (row - col).astype(F32)
    idx = lax.broadcasted_iota(jnp.int32, (c, 1), 0).astype(F32)
    live = lax.broadcasted_iota(jnp.int32, (c, 1), 0) < valid_rows

    chains = []
    for h in range(heads):
        log_gamma = math.log(1.0 - 2.0 ** (-5.0 - h))
        dmat = jnp.where(rel >= 0, jnp.exp(jnp.maximum(rel, 0.0) * log_gamma), 0.0)
        q_scale = jnp.exp((idx + 1.0) * log_gamma)
        k_scale = jnp.exp((valid_rows - 1.0 - idx) * log_gamma)
        for i in range(nb):
            q = q_ref[i, :, h * dk:(h + 1) * dk]
            k = k_ref[i, :, h * dk:(h + 1) * dk]
            v = v_ref[i, :, h * dv:(h + 1) * dv]
            if valid_rows < c:
                v = jnp.where(live, v, 0.0)
            q = q * cos2 + pltpu.roll(q, half, 1) * sin2
            k = (k * cos2 + pltpu.roll(k, half, 1) * sin2) * (dk ** -0.5)
            chains.append(dict(
                i=i, h=h, dmat=dmat, s_decay=math.exp(valid_rows * log_gamma),
                q_bf=q.astype(BF16), k_bf=k.astype(BF16), v_bf=v.astype(BF16),
                q_dec=(q * q_scale).astype(BF16), k_dec=(k * k_scale).astype(BF16)))

    for ch in chains:
        ch["inner"] = (_dot_nt(ch["q_bf"], ch["k_bf"]) * ch["dmat"]).astype(BF16)
    for ch in chains:
        s = s_scr[ch["i"], ch["h"]]
        ch["o"] = _dot(ch["q_dec"], s.astype(BF16)) + _dot(ch["inner"], ch["v_bf"])
        s_scr[ch["i"], ch["h"]] = s * ch["s_decay"] + _dot_tn(ch["k_dec"], ch["v_bf"])
    for ch in chains:
        i, h, o = ch["i"], ch["h"], ch["o"]
        mu = jnp.mean(o, axis=-1, keepdims=True)
        oc = o - mu
        var = jnp.mean(oc * oc, axis=-1, keepdims=True)
        o = oc * lax.rsqrt(var + NORM_EPS)
        o = o * gnw_ref[:, h * dv:(h + 1) * dv] + gnb_ref[:, h * dv:(h + 1) * dv]
        o = o * _silu(g_ref[i, :, h * dv:(h + 1) * dv])
        o_ref[i, :, h * dv:(h + 1) * dv] = o.astype(o_ref.dtype)

    @pl.when(n == pl.num_programs(1) - 1)
    def _():
        snew_ref[...] = s_scr[...]


def _ret(proj, col0, s0, gn_w, gn_b, *, chunk, valid_rows, nb, pos0, heads, dk, dv):
    b, t, _ = proj.shape
    width = heads * dv
    n_chunks = t // chunk
    blk0 = col0 // width
    half = dk // 2
    inv_freq = ROPE_BASE ** (-jnp.arange(half, dtype=F32) / half)
    inv_freq2 = jnp.concatenate([inv_freq, inv_freq]).reshape(1, dk)
    col_spec = lambda kk: pl.BlockSpec((nb, chunk, width), lambda i, n: (i, n, blk0 + kk))
    return pl.pallas_call(
        functools.partial(_ret_kernel, valid_rows=valid_rows, pos0=pos0, heads=heads, dk=dk, dv=dv),
        grid=(b // nb, n_chunks),
        in_specs=[
            col_spec(0), col_spec(1), col_spec(2), col_spec(3),
            pl.BlockSpec((nb, heads, dk, dv), lambda i, n: (i, 0, 0, 0)),
            pl.BlockSpec((1, dk), lambda i, n: (0, 0)),
            pl.BlockSpec((1, width), lambda i, n: (0, 0)),
            pl.BlockSpec((1, width), lambda i, n: (0, 0)),
        ],
        out_specs=[
            pl.BlockSpec((nb, chunk, width), lambda i, n: (i, n, 0)),
            pl.BlockSpec((nb, heads, dk, dv), lambda i, n: (i, 0, 0, 0)),
        ],
        out_shape=[
            jax.ShapeDtypeStruct((b, t, width), BF16),
            jax.ShapeDtypeStruct((b, heads, dk, dv), F32),
        ],
        scratch_shapes=[pltpu.VMEM((nb, heads, dk, dv), F32)],
        compiler_params=_params("parallel", "arbitrary"),
        name="ret_mixer",
    )(proj, proj, proj, proj, s0, inv_freq2, gn_w.reshape(1, width), gn_b.reshape(1, width))


def _trunk(x, mod, mod_f, rows_per_group, seq_rows, valid_rows, nb, pos0, to_seq, from_seq,
           conv_init, s_gdn0, s_ret0, w, tm, tf):
    heads, dk, dv = s_gdn0.shape[1:]
    x1, h2 = _ffn(x, mod, (0, 1, 2), w["norm_ffn1"], w["w1_gate"], w["w1_up"], w["w1_down"],
                  mod, (3, 4), w["norm_mix"], rows_per_group=rows_per_group, tm=tm, tf=tf,
                  emit_x=True, hn_dtype=BF16)
    conv_ch = heads * (2 * dk + dv)
    width = heads * dv
    ba0 = conv_ch + width
    tn = PROJ_COLS
    proj_ba = to_seq(_proj(h2, w["w_in_t"], lambda j: ba0, LANES, tm, LANES))
    chunk = min(CHUNK, seq_rows)
    gdn_args = dict(chunk=chunk, valid_rows=valid_rows, nb=nb, heads=heads, dk=dk, dv=dv)
    if conv_init is None:
        qkv, conv_new = _proj_conv(h2, w["w_in_t"], w["conv_w"], conv_ch, seq_rows, tm, tn)
        skip = lambda j: conv_ch + j * tn + jnp.where(j > 0, 2 * heads, 0)
        rest = to_seq(_proj(h2, w["w_in_t"], skip, 5 * width, tm, tn))
        o_gdn, s_gdn = _gdn((to_seq(qkv), 0), (rest, 0), proj_ba, s_gdn0, w["a_log"], w["dt_bias"],
                            w["gdn_norm_w"], **gdn_args)
        ret_src, ret_col = rest, width
    else:
        skip = lambda j: j * tn + jnp.where(j * tn >= ba0, 2 * heads, 0)
        proj = to_seq(_proj(h2, w["w_in_t"], skip, ba0 + 4 * width, tm, tn))
        o_gdn, s_gdn, conv_new = _gdn((proj, 0), (proj, conv_ch), proj_ba, s_gdn0, w["a_log"], w["dt_bias"],
                                      w["gdn_norm_w"], (conv_init, w["conv_w"]), **gdn_args)
        ret_src, ret_col = proj, ba0
    o_ret, s_ret = _ret(ret_src, ret_col, s_ret0, w["ret_gn_w"], w["ret_gn_b"],
                        chunk=chunk, valid_rows=valid_rows, nb=nb, pos0=pos0, heads=heads, dk=dk, dv=dv)
    o_gdn = from_seq(o_gdn)
    o_ret = from_seq(o_ret)
    x2 = _outproj(x1, mod, 5, o_gdn, o_ret, w["w_out"], rows_per_group=rows_per_group, tm=min(tm, 512))
    (y,) = _ffn(x2, mod, (6, 7, 8), w["norm_ffn2"], w["w2_gate"], w["w2_up"], w["w2_down"],
                mod_f, (0, 1), w["norm_final"], rows_per_group=rows_per_group, tm=tm, tf=tf,
                emit_x=False, hn_dtype=F32)
    return y, s_gdn, conv_new[:, SUBLANES - (CONV_WIDTH - 1):, :], s_ret


def kernel(x_prompt, x_sample, state_gdn, state_conv, state_ret, c_prompt, c_sample, w_ada, b_ada, norm_ffn1, w1_gate, w1_up, w1_down, norm_mix, w_in, conv_w, a_log, dt_bias, gdn_norm_w, ret_gn_w, ret_gn_b, w_out, norm_ffn2, w2_gate, w2_up, w2_down, w_ada_final, b_ada_final, norm_final):
    bp, tp, d = x_prompt.shape
    bs, ts, _ = x_sample.shape
    depth, _, heads, dk, dv = state_gdn.shape
    assert depth == 1, "single-layer trunk"
    assert bs == ROW_TILE, "time-major sample rows must align adaLN vectors with row tiles"
    conv_ch = state_conv.shape[-1]
    width = heads * dv
    ba0 = conv_ch + width
    ret0 = ba0 + 2 * heads

    n_c = bs + bp
    n_c_pad = -(-n_c // SUBLANES) * SUBLANES
    c_all = jnp.concatenate([c_sample, c_prompt, jnp.zeros((n_c_pad - n_c, d), F32)], axis=0)
    ada = _ada_proj(c_all, w_ada[0], b_ada[0], tn=512)
    ada_f = _ada_proj(c_all, w_ada_final, b_ada_final, tn=512)
    mod_s, mod_p = ada, ada[bs:bs + bp].reshape(bp, 1, N_ADA * d)
    modf_s, modf_p = ada_f, ada_f[bs:bs + bp].reshape(bp, 1, 2 * d)

    assert ba0 % PROJ_COLS == 0 and ret0 % SUBLANES == 0 and w_in.shape[-1] == ret0 + 4 * width
    w = dict(norm_ffn1=norm_ffn1[0], w1_gate=w1_gate[0], w1_up=w1_up[0], w1_down=w1_down[0],
             norm_mix=norm_mix[0], w_in_t=w_in[0].T, conv_w=conv_w[0], a_log=a_log[0],
             dt_bias=dt_bias[0], gdn_norm_w=gdn_norm_w[0], ret_gn_w=ret_gn_w[0], ret_gn_b=ret_gn_b[0],
             w_out=w_out[0], norm_ffn2=norm_ffn2[0], w2_gate=w2_gate[0], w2_up=w2_up[0],
             w2_down=w2_down[0], norm_final=norm_final)

    zeros_state = jnp.zeros((bp, heads, dk, dv), F32)
    y_p, gdn_p, conv_p, ret_p = _trunk(
        x_prompt.reshape(bp * tp, d), mod_p, modf_p, tp, tp, min(CHUNK, tp), 2, 0,
        lambda a: a.reshape(bp, tp, a.shape[-1]), lambda a: a.reshape(bp * tp, a.shape[-1]),
        None, zeros_state, zeros_state, w, tm=1024, tf=256)

    ts_pad = -(-ts // SUBLANES) * SUBLANES
    to_seq = lambda a: jnp.pad(a.reshape(ts, bs, a.shape[-1]).transpose(1, 0, 2),
                               ((0, 0), (0, ts_pad - ts), (0, 0)))
    from_seq = lambda a: a[:, :ts].transpose(1, 0, 2).reshape(ts * bs, a.shape[-1])
    conv_init = jnp.pad(state_conv[0], ((0, 0), (SUBLANES - (CONV_WIDTH - 1), 0), (0, 0)))
    y_s, gdn_s, conv_s, ret_s = _trunk(
        x_sample.transpose(1, 0, 2).reshape(ts * bs, d), mod_s, modf_s, None, ts_pad, ts, 8, PAST_LEN,
        to_seq, from_seq, conv_init, state_gdn[0], state_ret[0], w, tm=ts * bs, tf=512)
    y_s = y_s.reshape(ts, bs, d).transpose(1, 0, 2)

    return (y_p.reshape(bp, tp, d), y_s, gdn_p[None], conv_p[None], ret_p[None],
            gdn_s[None], conv_s[None], ret_s[None])
```

```python
import functools
import math

import jax
import jax.numpy as jnp
from jax import lax
from jax.experimental import pallas as pl
from jax.experimental.pallas import tpu as pltpu

F32 = jnp.float32
BF16 = jnp.bfloat16

LANES = 128
SUBLANES = 8
MXU_COLS = 256
VMEM_LIMIT_BYTES = 60 * 1024 * 1024

PAST_LEN = 16384
CONV_WIDTH = 4
CHUNK = 64
ROPE_BASE = 10000.0
NORM_EPS = 1e-6
L2_EPS = 1e-6
MACARON_WEIGHT = 0.5
N_ADA = 9
FFN_ROWS = 1024
FFN_COLS = 256
FFN_COLS_SMALL = 512
OUTPROJ_ROWS = 512
ADA_COLS = 512
SEQS_PER_STEP_FULL = 2
SEQS_PER_STEP_SHORT = 8
ROW_TILE = 128
NORM_ROWS = 16
FFN_SUB_ROWS = 512
PROJ_COLS = 1024
HISTORY_ROWS = SUBLANES
CONV_SUB_ROWS = 256


def _sigmoid(x):
    return 1.0 / (1.0 + jnp.exp(-x))


def _silu(x):
    return x * _sigmoid(x)


def _softplus(x):
    return jnp.maximum(x, 0.0) + jnp.log(1.0 + jnp.exp(-jnp.abs(x)))


def _dot(a, b):
    return jnp.dot(a, b, preferred_element_type=F32)


def _dot_nt(a, b):
    return lax.dot_general(a, b, (((1,), (1,)), ((), ())), preferred_element_type=F32)


def _dot_tn(a, b):
    return lax.dot_general(a, b, (((0,), (0,)), ((), ())), preferred_element_type=F32)


def _split3(x):
    hi = x.astype(BF16)
    r = x - hi.astype(F32)
    mid = r.astype(BF16)
    lo = (r - mid.astype(F32)).astype(BF16)
    return hi, mid, lo


def _rms_mod(x, gain, shift, scale):
    y = x * lax.rsqrt(jnp.mean(x * x, axis=-1, keepdims=True) + NORM_EPS)
    return (y * gain) * (1.0 + scale) + shift


def _mod_rows(ref, g):
    if ref.shape[0] == 1:
        return ref[...]
    return ref[g * NORM_ROWS:(g + 1) * NORM_ROWS, :]


def _for_row_groups(n_rows, fn):
    def tile(t, carry):
        base = pl.multiple_of(t * ROW_TILE, ROW_TILE)
        for g in range(ROW_TILE // NORM_ROWS):
            fn(pl.ds(base + g * NORM_ROWS, NORM_ROWS), g)
        return carry
    lax.fori_loop(0, n_rows // ROW_TILE, tile, 0)


def _params(*sem):
    return pltpu.CompilerParams(dimension_semantics=sem, vmem_limit_bytes=VMEM_LIMIT_BYTES)


def _ada_kernel(c_ref, w_ref, b_ref, o_ref):
    a = _silu(c_ref[...]).astype(BF16)
    o_ref[...] = _dot(a, w_ref[...].astype(BF16)) + b_ref[...]


def _ada_proj(c, w, b, tn):
    m, d = c.shape
    n = w.shape[1]
    return pl.pallas_call(
        _ada_kernel,
        grid=(n // tn,),
        in_specs=[
            pl.BlockSpec((m, d), lambda j: (0, 0)),
            pl.BlockSpec((d, tn), lambda j: (0, j)),
            pl.BlockSpec((1, tn), lambda j: (0, j)),
        ],
        out_specs=pl.BlockSpec((m, tn), lambda j: (0, j)),
        out_shape=jax.ShapeDtypeStruct((m, n), F32),
        compiler_params=_params("parallel"),
        name="ada_proj",
    )(c, w, b.reshape(1, n))


def _ffn_kernel(x_ref, sh_ref, sc_ref, gt_ref, gain_ref, wg_ref, wu_ref, wd_ref,
                gain2_ref, sh2_ref, sc2_ref, *rest, emit_x):
    if emit_x:
        xo_ref, hn_hbm, h_scr, stage_scr, stage_sem = rest
        acc_ref = xo_ref
    else:
        hn_hbm, h_scr, stage_scr, stage_sem, acc_ref = rest
    j = pl.program_id(1)
    tm = x_ref.shape[0]

    @pl.when(j == 0)
    def _():
        def group(rows, g):
            h = _rms_mod(x_ref[rows, :], gain_ref[...], _mod_rows(sh_ref, g), _mod_rows(sc_ref, g))
            h_scr[rows, :] = h.astype(BF16)
            acc_ref[rows, :] = jnp.zeros((NORM_ROWS, acc_ref.shape[1]), F32)
        _for_row_groups(tm, group)

    wg = wg_ref[...].astype(BF16)
    wu = wu_ref[...].astype(BF16)
    wd = wd_ref[...].astype(BF16)
    sub = min(tm, FFN_SUB_ROWS)
    for r in range(tm // sub):
        rows = slice(r * sub, (r + 1) * sub)
        h = h_scr[rows, :]
        a = (_silu(_dot(h, wg)) * _dot(h, wu)).astype(BF16)
        acc_ref[rows, :] += _dot(a, wd)

    @pl.when(j == pl.num_programs(1) - 1)
    def _():
        row0 = pl.program_id(0) * tm

        def hn_copy(slot, base):
            dst = hn_hbm.at[pl.ds(pl.multiple_of(row0 + base, ROW_TILE), ROW_TILE), :]
            return pltpu.make_async_copy(stage_scr.at[slot], dst, stage_sem.at[slot])

        def tile_pair(p, carry):
            for slot in range(2):
                base = pl.multiple_of((2 * p + slot) * ROW_TILE, ROW_TILE)

                @pl.when(p > 0)
                def _(slot=slot, base=base):
                    hn_copy(slot, base - 2 * ROW_TILE).wait()

                for g in range(ROW_TILE // NORM_ROWS):
                    rows = pl.ds(base + g * NORM_ROWS, NORM_ROWS)
                    xo = x_ref[rows, :] + (MACARON_WEIGHT * _mod_rows(gt_ref, g)) * acc_ref[rows, :]
                    if emit_x:
                        xo_ref[rows, :] = xo
                    hn = _rms_mod(xo, gain2_ref[...], _mod_rows(sh2_ref, g), _mod_rows(sc2_ref, g))
                    stage_scr[slot, g * NORM_ROWS:(g + 1) * NORM_ROWS, :] = hn.astype(stage_scr.dtype)
                hn_copy(slot, base).start()
            return carry
        lax.fori_loop(0, tm // (2 * ROW_TILE), tile_pair, 0)
        for slot in range(2):
            hn_copy(slot, tm - (2 - slot) * ROW_TILE).wait()


def _mod_spec(mod, k, d, rows_per_group, tm):
    if mod.ndim == 3:
        blocks_per_group = rows_per_group // tm
        return pl.BlockSpec((None, 1, d), lambda i, j: (i // blocks_per_group, 0, k))
    return pl.BlockSpec((ROW_TILE, d), lambda i, j: (0, k))


def _ffn(x, mod, ks, gain, wg, wu, wd, mod2, ks2, gain2, *, rows_per_group, tm, tf, emit_x, hn_dtype):
    m, d = x.shape
    f = wg.shape[1]
    assert m % tm == 0 and f % tf == 0 and tm % ROW_TILE == 0 and d % LANES == 0
    assert mod.ndim == 2 or rows_per_group % tm == 0
    k_sh, k_sc, k_gt = ks
    k_sh2, k_sc2 = ks2
    vec = lambda: pl.BlockSpec((1, d), lambda i, j: (0, 0))
    row_block = lambda: pl.BlockSpec((tm, d), lambda i, j: (i, 0))
    assert tm % (2 * ROW_TILE) == 0
    out_shape = [jax.ShapeDtypeStruct((m, d), hn_dtype)]
    out_specs = [pl.BlockSpec(memory_space=pl.ANY)]
    if emit_x:
        out_shape = [jax.ShapeDtypeStruct((m, d), F32)] + out_shape
        out_specs = [row_block()] + out_specs
    return pl.pallas_call(
        functools.partial(_ffn_kernel, emit_x=emit_x),
        grid=(m // tm, f // tf),
        in_specs=[
            row_block(),
            _mod_spec(mod, k_sh, d, rows_per_group, tm),
            _mod_spec(mod, k_sc, d, rows_per_group, tm),
            _mod_spec(mod, k_gt, d, rows_per_group, tm),
            vec(),
            pl.BlockSpec((d, tf), lambda i, j: (0, j)),
            pl.BlockSpec((d, tf), lambda i, j: (0, j)),
            pl.BlockSpec((tf, d), lambda i, j: (j, 0)),
            vec(),
            _mod_spec(mod2, k_sh2, d, rows_per_group, tm),
            _mod_spec(mod2, k_sc2, d, rows_per_group, tm),
        ],
        out_specs=out_specs,
        out_shape=out_shape,
        scratch_shapes=[pltpu.VMEM((tm, d), BF16), pltpu.VMEM((2, ROW_TILE, d), hn_dtype),
                        pltpu.SemaphoreType.DMA((2,))] + ([] if emit_x else [pltpu.VMEM((tm, d), F32)]),
        compiler_params=_params("parallel", "arbitrary"),
        name="ffn",
    )(x, mod, mod, mod, gain.reshape(1, d), wg, wu, wd, gain2.reshape(1, d), mod2, mod2)


def _proj_kernel(h_ref, wt_ref, o_ref, wb_scr):
    @pl.when(pl.program_id(1) == 0)
    def _():
        wb_scr[...] = wt_ref[...].astype(BF16)

    o_ref[...] = _dot_nt(h_ref[...], wb_scr[...])


def _proj(h, wt, row_of, n_cols, tm, tn):
    m, d = h.shape
    return pl.pallas_call(
        _proj_kernel,
        grid=(n_cols // tn, m // tm),
        in_specs=[
            pl.BlockSpec((tm, d), lambda j, i: (i, 0)),
            pl.BlockSpec((pl.Element(tn), pl.Element(d)),
                         lambda j, i: (pl.multiple_of(row_of(j), SUBLANES), 0)),
        ],
        out_specs=pl.BlockSpec((tm, tn), lambda j, i: (i, j)),
        out_shape=jax.ShapeDtypeStruct((m, n_cols), F32),
        scratch_shapes=[pltpu.VMEM((tn, d), BF16)],
        compiler_params=_params("parallel", "arbitrary"),
        name="in_proj",
    )(h, wt)


def _proj_conv_kernel(h_ref, wt_ref, convw_ref, o_ref, cnew_ref, wb_scr, tail_scr, *, blocks_per_seq):
    i = pl.program_id(1)

    @pl.when(i == 0)
    def _():
        wb_scr[...] = wt_ref[...].astype(BF16)

    h = h_ref[...]
    tm, tn = o_ref.shape
    first = i % blocks_per_seq == 0
    for cg in range(tn // MXU_COLS):
        cols = slice(cg * MXU_COLS, (cg + 1) * MXU_COLS)
        raw = _dot_nt(h, wb_scr[cols, :])
        w = convw_ref[:, cols]
        prev = jnp.where(first, 0.0, tail_scr[:, cols])
        for r in range(tm // CONV_SUB_ROWS):
            rows = slice(r * CONV_SUB_ROWS, (r + 1) * CONV_SUB_ROWS)
            x = raw[rows, :]
            y, _ = _causal_conv_silu(prev, x, w)
            o_ref[rows, cols] = y
            prev = x[CONV_SUB_ROWS - HISTORY_ROWS:, :]
        tail_scr[:, cols] = prev
        cnew_ref[:, cols] = prev


def _proj_conv(h, wt, conv_w, n_cols, rows_per_seq, tm, tn):
    m, d = h.shape
    blocks_per_seq = rows_per_seq // tm
    return pl.pallas_call(
        functools.partial(_proj_conv_kernel, blocks_per_seq=blocks_per_seq),
        grid=(n_cols // tn, m // tm),
        in_specs=[
            pl.BlockSpec((tm, d), lambda j, i: (i, 0)),
            pl.BlockSpec((tn, d), lambda j, i: (j, 0)),
            pl.BlockSpec((CONV_WIDTH, tn), lambda j, i: (0, j)),
        ],
        out_specs=[
            pl.BlockSpec((tm, tn), lambda j, i: (i, j)),
            pl.BlockSpec((None, HISTORY_ROWS, tn), lambda j, i: (i // blocks_per_seq, 0, j)),
        ],
        out_shape=[
            jax.ShapeDtypeStruct((m, n_cols), F32),
            jax.ShapeDtypeStruct((m // rows_per_seq, HISTORY_ROWS, n_cols), F32),
        ],
        scratch_shapes=[pltpu.VMEM((tn, d), BF16), pltpu.VMEM((HISTORY_ROWS, tn), F32)],
        compiler_params=_params("parallel", "arbitrary"),
        name="in_proj_conv",
    )(h, wt, conv_w)


def _outproj_kernel(x_ref, gt_ref, og_ref, or_ref, w_ref, o_ref, wb_scr):
    @pl.when(pl.program_id(0) == 0)
    def _():
        wb_scr[...] = w_ref[...].astype(BF16)

    kw = og_ref.shape[1]
    tm = x_ref.shape[0]
    mixed = _dot(og_ref[...], wb_scr[:kw, :]) + _dot(or_ref[...], wb_scr[kw:, :])
    for r in range(tm // ROW_TILE):
        rows = slice(r * ROW_TILE, (r + 1) * ROW_TILE)
        o_ref[rows, :] = x_ref[rows, :] + gt_ref[...] * mixed[rows, :]


def _outproj(x, mod, k_gt, og, orr, w_out, *, rows_per_group, tm):
    m, d = x.shape
    kw = og.shape[1]
    if mod.ndim == 3:
        blocks_per_group = rows_per_group // tm
        gt_spec = pl.BlockSpec((None, 1, d), lambda i: (i // blocks_per_group, 0, k_gt))
    else:
        gt_spec = pl.BlockSpec((ROW_TILE, d), lambda i: (0, k_gt))
    return pl.pallas_call(
        _outproj_kernel,
        grid=(m // tm,),
        in_specs=[
            pl.BlockSpec((tm, d), lambda i: (i, 0)),
            gt_spec,
            pl.BlockSpec((tm, kw), lambda i: (i, 0)),
            pl.BlockSpec((tm, kw), lambda i: (i, 0)),
            pl.BlockSpec(w_out.shape, lambda i: (0, 0), pipeline_mode=pl.Buffered(1)),
        ],
        out_specs=pl.BlockSpec((tm, d), lambda i: (i, 0)),
        out_shape=jax.ShapeDtypeStruct((m, d), F32),
        scratch_shapes=[pltpu.VMEM(w_out.shape, BF16)],
        compiler_params=_params("arbitrary"),
        name="out_proj",
    )(x, mod, og, orr, w_out)


SOLVE_BLOCK = 16


def _nilpotent_apply(chains, index):
    power = 1
    while power < index:
        square = 2 * power < index
        for ch in chains:
            right = jnp.concatenate([ch["sol"], ch["nil"]], axis=1) if square else ch["sol"]
            ch["prod"] = _dot(ch["nil"].astype(BF16), right.astype(BF16))
        for ch in chains:
            width = ch["sol"].shape[1]
            upd = ch["prod"][:, :width]
            ch["sol"] = ch["sol"] - upd if power == 1 else ch["sol"] + upd
            if square:
                ch["nil"] = ch["prod"][:, width:]
        power *= 2


def _causal_conv_silu(prev, x, w):
    xp = jnp.concatenate([prev, x], axis=0)
    acc = x * w[CONV_WIDTH - 1:CONV_WIDTH, :]
    for sft in range(1, CONV_WIDTH):
        tap = CONV_WIDTH - 1 - sft
        acc = acc + pltpu.roll(xp, sft, 0)[HISTORY_ROWS:, :] * w[tap:tap + 1, :]
    return _silu(acc), xp


def _gdn_kernel(*refs, valid_rows, heads, dk, dv, conv_here):
    if conv_here:
        (qkv_ref, z_ref, ba_ref, s0_ref, alog_ref, dtb_ref, nw_ref, cinit_ref, convw_ref,
         o_ref, snew_ref, cnew_ref, s_scr, tail_scr) = refs
    else:
        qkv_ref, z_ref, ba_ref, s0_ref, alog_ref, dtb_ref, nw_ref, o_ref, snew_ref, s_scr = refs
    n = pl.program_id(1)
    last = pl.num_programs(1) - 1
    nb, c, _ = qkv_ref.shape

    @pl.when(n == 0)
    def _():
        s_scr[...] = s0_ref[...]
        if conv_here:
            tail_scr[...] = cinit_ref[...]

    row = lax.broadcasted_iota(jnp.int32, (c, c), 0)
    col = lax.broadcasted_iota(jnp.int32, (c, c), 1)
    causal = row >= col
    strict = row > col
    diag = row == col
    tril = jnp.where(causal, 1.0, 0.0).astype(BF16)
    nw = nw_ref[...]
    neg_a = -jnp.exp(alog_ref[...])
    dtb = dtb_ref[...]

    chains = []
    for i in range(nb):
        if conv_here:
            x = qkv_ref[i]
            qkv, xp = _causal_conv_silu(tail_scr[i], x, convw_ref[...])
            tail_scr[i] = x[c - HISTORY_ROWS:, :]

            @pl.when(n == last)
            def _(i=i, xp=xp):
                cnew_ref[i] = xp[valid_rows:valid_rows + HISTORY_ROWS, :]
        else:
            qkv = qkv_ref[i]

        ba = ba_ref[i]
        beta_all = _sigmoid(ba)
        g_all = neg_a * _softplus(ba + dtb)
        if valid_rows < c:
            live = lax.broadcasted_iota(jnp.int32, ba.shape, 0) < valid_rows
            beta_all = jnp.where(live, beta_all, 0.0)
            g_all = jnp.where(live, g_all, 0.0)
        g_hi, g_mid, g_lo = _split3(g_all)
        gc_all = _dot(tril, g_hi) + (_dot(tril, g_mid) + _dot(tril, g_lo))
        gc_last_all = gc_all[c - 1:c, :]
        eg_all = jnp.exp(gc_all)
        kdec_all = jnp.exp(gc_last_all - gc_all)
        gl_all = jnp.exp(gc_last_all)

        for h in range(heads):
            beta = beta_all[:, h:h + 1]
            gc = gc_all[:, heads + h:heads + h + 1]
            eg = eg_all[:, heads + h:heads + h + 1]
            q = qkv[:, h * dk:(h + 1) * dk]
            k = qkv[:, heads * dk + h * dk:heads * dk + (h + 1) * dk]
            v = qkv[:, 2 * heads * dk + h * dv:2 * heads * dk + (h + 1) * dv]
            q = q * (lax.rsqrt(jnp.sum(q * q, axis=-1, keepdims=True) + L2_EPS) * (dk ** -0.5))
            k = k * lax.rsqrt(jnp.sum(k * k, axis=-1, keepdims=True) + L2_EPS)
            kb = k * beta
            gc_col = jnp.broadcast_to(gc, (c, c))
            gc_row = jnp.sum(jnp.where(diag, gc_col, 0.0), axis=0, keepdims=True)
            decay = jnp.where(causal, jnp.exp(gc_col - gc_row), 0.0)
            chains.append(dict(
                i=i, h=h, decay=decay,
                kbq=jnp.concatenate([kb, q], axis=0).astype(BF16), k_bf=k.astype(BF16),
                rhs=jnp.concatenate([v * beta, kb * eg], axis=1),
                q_dec=q * eg,
                k_dec=(k * kdec_all[:, heads + h:heads + h + 1]).astype(BF16),
                gl=gl_all[:, heads + h:heads + h + 1]))

    for ch in chains:
        ch["kq"] = _dot_nt(ch["kbq"], ch["k_bf"])
    for ch in chains:
        ch["a"] = jnp.where(strict, ch["kq"][:c] * ch["decay"], 0.0)
        ch["qk"] = (ch["kq"][c:] * ch["decay"]).astype(BF16)

    if valid_rows <= SOLVE_BLOCK:
        for ch in chains:
            ch["nil"], ch["sol"] = ch["a"], ch["rhs"]
        _nilpotent_apply(chains, valid_rows)
    else:
        same_block = (row // SOLVE_BLOCK) == (col // SOLVE_BLOCK)
        eye = jnp.where(diag, 1.0, 0.0)
        for ch in chains:
            ch["nil"] = jnp.where(same_block, ch["a"], 0.0)
            ch["sol"] = eye
        _nilpotent_apply(chains, SOLVE_BLOCK)
        for ch in chains:
            off_block = jnp.where(same_block, 0.0, ch["a"])
            right = jnp.concatenate([ch["rhs"], off_block], axis=1)
            ch["prod"] = _dot(ch["sol"].astype(BF16), right.astype(BF16))
        for ch in chains:
            ch["sol"] = ch["prod"][:, :dv + dk]
            ch["nil"] = ch["prod"][:, dv + dk:]
        _nilpotent_apply(chains, c // SOLVE_BLOCK)

    for ch in chains:
        ch["s"] = s_scr[ch["i"], ch["h"]]
        lhs = jnp.concatenate([ch["sol"][:, dv:], ch["q_dec"]], axis=0).astype(BF16)
        ch["ws_qs"] = _dot(lhs, ch["s"].astype(BF16))
    for ch in chains:
        ch["v_new"] = (ch["sol"][:, :dv] - ch["ws_qs"][:c]).astype(BF16)
    for ch in chains:
        s_scr[ch["i"], ch["h"]] = ch["s"] * ch["gl"] + _dot_tn(ch["k_dec"], ch["v_new"])
        ch["o"] = ch["ws_qs"][c:] + _dot(ch["qk"], ch["v_new"])
    for ch in chains:
        i, h, o = ch["i"], ch["h"], ch["o"]
        o = o * lax.rsqrt(jnp.mean(o * o, axis=-1, keepdims=True) + NORM_EPS) * nw
        o = o * _silu(z_ref[i, :, h * dv:(h + 1) * dv])
        o_ref[i, :, h * dv:(h + 1) * dv] = o.astype(o_ref.dtype)

    @pl.when(n == last)
    def _():
        snew_ref[...] = s_scr[...]


def _gdn(qkv_src, z_src, proj_ba, s0, a_log, dt_bias, norm_w, conv=None, *, chunk, valid_rows, nb, heads, dk, dv):
    (qkv_arr, qkv_col), (z_arr, z_col) = qkv_src, z_src
    b, t, _ = qkv_arr.shape
    assert valid_rows <= SOLVE_BLOCK or (valid_rows == chunk and chunk % SOLVE_BLOCK == 0)
    conv_ch = heads * (2 * dk + dv)
    width = heads * dv
    n_chunks = t // chunk
    qkv_blk = qkv_col // conv_ch
    z_blk = z_col // width
    lane_vec = lambda x: jnp.zeros((1, LANES), F32).at[0, heads:2 * heads].set(x.astype(F32))
    state_spec = pl.BlockSpec((nb, heads, dk, dv), lambda i, n: (i, 0, 0, 0))
    hist_spec = pl.BlockSpec((nb, HISTORY_ROWS, conv_ch), lambda i, n: (i, 0, 0))
    lane_spec = pl.BlockSpec((1, LANES), lambda i, n: (0, 0))
    in_specs = [
        pl.BlockSpec((nb, chunk, conv_ch), lambda i, n: (i, n, qkv_blk)),
        pl.BlockSpec((nb, chunk, width), lambda i, n: (i, n, z_blk)),
        pl.BlockSpec((nb, chunk, LANES), lambda i, n: (i, n, 0)),
        state_spec, lane_spec, lane_spec,
        pl.BlockSpec((1, dv), lambda i, n: (0, 0)),
    ]
    args = [qkv_arr, z_arr, proj_ba, s0, lane_vec(a_log), lane_vec(dt_bias), norm_w.reshape(1, dv)]
    out_specs = [pl.BlockSpec((nb, chunk, width), lambda i, n: (i, n, 0)), state_spec]
    out_shape = [jax.ShapeDtypeStruct((b, t, width), BF16), jax.ShapeDtypeStruct((b, heads, dk, dv), F32)]
    scratch = [pltpu.VMEM((nb, heads, dk, dv), F32)]
    if conv is not None:
        conv_init, conv_w = conv
        in_specs += [hist_spec, pl.BlockSpec((CONV_WIDTH, conv_ch), lambda i, n: (0, 0))]
        args += [conv_init, conv_w]
        out_specs.append(hist_spec)
        out_shape.append(jax.ShapeDtypeStruct((b, HISTORY_ROWS, conv_ch), F32))
        scratch.append(pltpu.VMEM((nb, HISTORY_ROWS, conv_ch), F32))
    return pl.pallas_call(
        functools.partial(_gdn_kernel, valid_rows=valid_rows, heads=heads, dk=dk, dv=dv,
                          conv_here=conv is not None),
        grid=(b // nb, n_chunks),
        in_specs=in_specs,
        out_specs=out_specs,
        out_shape=out_shape,
        scratch_shapes=scratch,
        compiler_params=_params("parallel", "arbitrary"),
        name="gdn_mixer",
    )(*args)


def _ret_kernel(q_ref, k_ref, v_ref, g_ref, s0_ref, invf_ref, gnw_ref, gnb_ref,
                o_ref, snew_ref, s_scr, *, valid_rows, pos0, heads, dk, dv):
    n = pl.program_id(1)
    nb, c, _ = q_ref.shape
    half = dk // 2

    @pl.when(n == 0)
    def _():
        s_scr[...] = s0_ref[...]

    t_idx = lax.broadcasted_iota(jnp.int32, (c, dk), 0)
    pos = (pos0 + n * c + t_idx).astype(F32)
    ang = pos * invf_ref[...]
    cos2 = jnp.cos(ang)
    lane = lax.broadcasted_iota(jnp.int32, (c, dk), 1)
    sin2 = jnp.where(lane < half, -jnp.sin(ang), jnp.sin(ang))

    row = lax.broadcasted_iota(jnp.int32, (c, c), 0)
    col = lax.broadcasted_iota(jnp.int32, (c, c), 1)
    rel = (row - col).astype(F32)
    idx = lax.broadcasted_iota(jnp.int32, (c, 1), 0).astype(F32)
    live = lax.broadcasted_iota(jnp.int32, (c, 1), 0) < valid_rows

    chains = []
    for h in range(heads):
        log_gamma = math.log(1.0 - 2.0 ** (-5.0 - h))
        dmat = jnp.where(rel >= 0, jnp.exp(jnp.maximum(rel, 0.0) * log_gamma), 0.0)
        q_scale = jnp.exp((idx + 1.0) * log_gamma)
        k_scale = jnp.exp((valid_rows - 1.0 - idx) * log_gamma)
        for i in range(nb):
            q = q_ref[i, :, h * dk:(h + 1) * dk]
            k = k_ref[i, :, h * dk:(h + 1) * dk]
            v = v_ref[i, :, h * dv:(h + 1) * dv]
            if valid_rows < c:
                v = jnp.where(live, v, 0.0)
            q = q * cos2 + pltpu.roll(q, half, 1) * sin2
            k = (k * cos2 + pltpu.roll(k, half, 1) * sin2) * (dk ** -0.5)
            chains.append(dict(
                i=i, h=h, dmat=dmat, s_decay=math.exp(valid_rows * log_gamma),
                q_bf=q.astype(BF16), k_bf=k.astype(BF16), v_bf=v.astype(BF16),
                q_dec=(q * q_scale).astype(BF16), k_dec=(k * k_scale).astype(BF16)))

    for ch in chains:
        ch["inner"] = (_dot_nt(ch["q_bf"], ch["k_bf"]) * ch["dmat"]).astype(BF16)
    for ch in chains:
        s = s_scr[ch["i"], ch["h"]]
        ch["o"] = _dot(ch["q_dec"], s.astype(BF16)) + _dot(ch["inner"], ch["v_bf"])
        s_scr[ch["i"], ch["h"]] = s * ch["s_decay"] + _dot_tn(ch["k_dec"], ch["v_bf"])
    for ch in chains:
        i, h, o = ch["i"], ch["h"], ch["o"]
        mu = jnp.mean(o, axis=-1, keepdims=True)
        oc = o - mu
        var = jnp.mean(oc * oc, axis=-1, keepdims=True)
        o = oc * lax.rsqrt(var + NORM_EPS)
        o = o * gnw_ref[:, h * dv:(h + 1) * dv] + gnb_ref[:, h * dv:(h + 1) * dv]
        o = o * _silu(g_ref[i, :, h * dv:(h + 1) * dv])
        o_ref[i, :, h * dv:(h + 1) * dv] = o.astype(o_ref.dtype)

    @pl.when(n == pl.num_programs(1) - 1)
    def _():
        snew_ref[...] = s_scr[...]


def _ret(proj, col0, s0, gn_w, gn_b, *, chunk, valid_rows, nb, pos0, heads, dk, dv):
    b, t, _ = proj.shape
    width = heads * dv
    n_chunks = t // chunk
    blk0 = col0 // width
    half = dk // 2
    inv_freq = ROPE_BASE ** (-jnp.arange(half, dtype=F32) / half)
    inv_freq2 = jnp.concatenate([inv_freq, inv_freq]).reshape(1, dk)
    col_spec = lambda kk: pl.BlockSpec((nb, chunk, width), lambda i, n: (i, n, blk0 + kk))
    return pl.pallas_call(
        functools.partial(_ret_kernel, valid_rows=valid_rows, pos0=pos0, heads=heads, dk=dk, dv=dv),
        grid=(b // nb, n_chunks),
        in_specs=[
            col_spec(0), col_spec(1), col_spec(2), col_spec(3),
            pl.BlockSpec((nb, heads, dk, dv), lambda i, n: (i, 0, 0, 0)),
            pl.BlockSpec((1, dk), lambda i, n: (0, 0)),
            pl.BlockSpec((1, width), lambda i, n: (0, 0)),
            pl.BlockSpec((1, width), lambda i, n: (0, 0)),
        ],
        out_specs=[
            pl.BlockSpec((nb, chunk, width), lambda i, n: (i, n, 0)),
            pl.BlockSpec((nb, heads, dk, dv), lambda i, n: (i, 0, 0, 0)),
        ],
        out_shape=[
            jax.ShapeDtypeStruct((b, t, width), BF16),
            jax.ShapeDtypeStruct((b, heads, dk, dv), F32),
        ],
        scratch_shapes=[pltpu.VMEM((nb, heads, dk, dv), F32)],
        compiler_params=_params("parallel", "arbitrary"),
        name="ret_mixer",
    )(proj, proj, proj, proj, s0, inv_freq2, gn_w.reshape(1, width), gn_b.reshape(1, width))


def _trunk(x, mod, mod_f, rows_per_group, seq_rows, valid_rows, nb, pos0, to_seq, from_seq,
           conv_init, s_gdn0, s_ret0, w, tm, tf):
    heads, dk, dv = s_gdn0.shape[1:]
    x1, h2 = _ffn(x, mod, (0, 1, 2), w["norm_ffn1"], w["w1_gate"], w["w1_up"], w["w1_down"],
                  mod, (3, 4), w["norm_mix"], rows_per_group=rows_per_group, tm=tm, tf=tf,
                  emit_x=True, hn_dtype=BF16)
    conv_ch = heads * (2 * dk + dv)
    width = heads * dv
    ba0 = conv_ch + width
    tn = PROJ_COLS
    proj_ba = to_seq(_proj(h2, w["w_in_t"], lambda j: ba0, LANES, tm, LANES))
    chunk = min(CHUNK, seq_rows)
    nb_gdn, nb_ret = nb
    gdn_args = dict(chunk=chunk, valid_rows=valid_rows, nb=nb_gdn, heads=heads, dk=dk, dv=dv)
    if conv_init is None:
        qkv, conv_new = _proj_conv(h2, w["w_in_t"], w["conv_w"], conv_ch, seq_rows, tm, tn)
        skip = lambda j: conv_ch + j * tn + jnp.where(j > 0, 2 * heads, 0)
        rest = to_seq(_proj(h2, w["w_in_t"], skip, 5 * width, tm, tn))
        o_gdn, s_gdn = _gdn((to_seq(qkv), 0), (rest, 0), proj_ba, s_gdn0, w["a_log"], w["dt_bias"],
                            w["gdn_norm_w"], **gdn_args)
        ret_src, ret_col = rest, width
    else:
        skip = lambda j: j * tn + jnp.where(j * tn >= ba0, 2 * heads, 0)
        proj = to_seq(_proj(h2, w["w_in_t"], skip, ba0 + 4 * width, tm, tn))
        o_gdn, s_gdn, conv_new = _gdn((proj, 0), (proj, conv_ch), proj_ba, s_gdn0, w["a_log"], w["dt_bias"],
                                      w["gdn_norm_w"], (conv_init, w["conv_w"]), **gdn_args)
        ret_src, ret_col = proj, ba0
    o_ret, s_ret = _ret(ret_src, ret_col, s_ret0, w["ret_gn_w"], w["ret_gn_b"],
                        chunk=chunk, valid_rows=valid_rows, nb=nb_ret, pos0=pos0, heads=heads, dk=dk, dv=dv)
    o_gdn = from_seq(o_gdn)
    o_ret = from_seq(o_ret)
    x2 = _outproj(x1, mod, 5, o_gdn, o_ret, w["w_out"], rows_per_group=rows_per_group, tm=min(tm, OUTPROJ_ROWS))
    (y,) = _ffn(x2, mod, (6, 7, 8), w["norm_ffn2"], w["w2_gate"], w["w2_up"], w["w2_down"],
                mod_f, (0, 1), w["norm_final"], rows_per_group=rows_per_group, tm=tm, tf=tf,
                emit_x=False, hn_dtype=F32)
    return y, s_gdn, conv_new[:, SUBLANES - (CONV_WIDTH - 1):, :], s_ret


def kernel(x_prompt, x_sample, state_gdn, state_conv, state_ret, c_prompt, c_sample, w_ada, b_ada, norm_ffn1, w1_gate, w1_up, w1_down, norm_mix, w_in, conv_w, a_log, dt_bias, gdn_norm_w, ret_gn_w, ret_gn_b, w_out, norm_ffn2, w2_gate, w2_up, w2_down, w_ada_final, b_ada_final, norm_final):
    bp, tp, d = x_prompt.shape
    bs, ts, _ = x_sample.shape
    depth, _, heads, dk, dv = state_gdn.shape
    assert depth == 1, "single-layer trunk"
    assert bs == ROW_TILE, "time-major sample rows must align adaLN vectors with row tiles"
    conv_ch = state_conv.shape[-1]
    width = heads * dv
    ba0 = conv_ch + width
    ret0 = ba0 + 2 * heads

    n_c = bs + bp
    n_c_pad = -(-n_c // SUBLANES) * SUBLANES
    c_all = jnp.concatenate([c_sample, c_prompt, jnp.zeros((n_c_pad - n_c, d), F32)], axis=0)
    ada = _ada_proj(c_all, w_ada[0], b_ada[0], tn=ADA_COLS)
    ada_f = _ada_proj(c_all, w_ada_final, b_ada_final, tn=ADA_COLS)
    mod_s, mod_p = ada, ada[bs:bs + bp].reshape(bp, 1, N_ADA * d)
    modf_s, modf_p = ada_f, ada_f[bs:bs + bp].reshape(bp, 1, 2 * d)

    assert ba0 % PROJ_COLS == 0 and ret0 % SUBLANES == 0 and w_in.shape[-1] == ret0 + 4 * width
    w = dict(norm_ffn1=norm_ffn1[0], w1_gate=w1_gate[0], w1_up=w1_up[0], w1_down=w1_down[0],
             norm_mix=norm_mix[0], w_in_t=w_in[0].T, conv_w=conv_w[0], a_log=a_log[0],
             dt_bias=dt_bias[0], gdn_norm_w=gdn_norm_w[0], ret_gn_w=ret_gn_w[0], ret_gn_b=ret_gn_b[0],
             w_out=w_out[0], norm_ffn2=norm_ffn2[0], w2_gate=w2_gate[0], w2_up=w2_up[0],
             w2_down=w2_down[0], norm_final=norm_final)

    zeros_state = jnp.zeros((bp, heads, dk, dv), F32)
    y_p, gdn_p, conv_p, ret_p = _trunk(
        x_prompt.reshape(bp * tp, d), mod_p, modf_p, tp, tp, min(CHUNK, tp), (bp, SEQS_PER_STEP_FULL), 0,
        lambda a: a.reshape(bp, tp, a.shape[-1]), lambda a: a.reshape(bp * tp, a.shape[-1]),
        None, zeros_state, zeros_state, w, tm=FFN_ROWS, tf=FFN_COLS)

    ts_pad = -(-ts // SUBLANES) * SUBLANES
    to_seq = lambda a: jnp.pad(a.reshape(ts, bs, a.shape[-1]).transpose(1, 0, 2),
                               ((0, 0), (0, ts_pad - ts), (0, 0)))
    from_seq = lambda a: a[:, :ts].transpose(1, 0, 2).reshape(ts * bs, a.shape[-1])
    conv_init = jnp.pad(state_conv[0], ((0, 0), (SUBLANES - (CONV_WIDTH - 1), 0), (0, 0)))
    y_s, gdn_s, conv_s, ret_s = _trunk(
        x_sample.transpose(1, 0, 2).reshape(ts * bs, d), mod_s, modf_s, None, ts_pad, ts,
        (SEQS_PER_STEP_SHORT, SEQS_PER_STEP_SHORT), PAST_LEN,
        to_seq, from_seq, conv_init, state_gdn[0], state_ret[0], w, tm=ts * bs, tf=FFN_COLS_SMALL)
    y_s = y_s.reshape(ts, bs, d).transpose(1, 0, 2)

    return (y_p.reshape(bp, tp, d), y_s, gdn_p[None], conv_p[None], ret_p[None],
            gdn_s[None], conv_s[None], ret_s[None])
```

```python
import functools
import math

import jax
import jax.numpy as jnp
from jax import lax
from jax.experimental import pallas as pl
from jax.experimental.pallas import tpu as pltpu

F32 = jnp.float32
BF16 = jnp.bfloat16

LANES = 128
SUBLANES = 8
MXU_COLS = 256
VMEM_LIMIT_BYTES = 60 * 1024 * 1024

PAST_LEN = 16384
CONV_WIDTH = 4
CHUNK = 64
ROPE_BASE = 10000.0
NORM_EPS = 1e-6
L2_EPS = 1e-6
MACARON_WEIGHT = 0.5
N_ADA = 9
FFN_ROWS = 1024
FFN_COLS = 256
FFN_COLS_SMALL = 512
OUTPROJ_ROWS = 512
ADA_COLS = 512
SEQS_PER_STEP_FULL = 2
SEQS_PER_STEP_SHORT = 8
ROW_TILE = 128
NORM_ROWS = 16
FFN_SUB_ROWS = 512
PROJ_COLS = 1024
HISTORY_ROWS = SUBLANES
CONV_SUB_ROWS = 256


def _sigmoid(x):
    return 1.0 / (1.0 + jnp.exp(-x))


def _silu(x):
    return x * _sigmoid(x)


def _softplus(x):
    return jnp.maximum(x, 0.0) + jnp.log(1.0 + jnp.exp(-jnp.abs(x)))


def _dot(a, b):
    return jnp.dot(a, b, preferred_element_type=F32)


def _dot_nt(a, b):
    return lax.dot_general(a, b, (((1,), (1,)), ((), ())), preferred_element_type=F32)


def _dot_tn(a, b):
    return lax.dot_general(a, b, (((0,), (0,)), ((), ())), preferred_element_type=F32)


def _split3(x):
    hi = x.astype(BF16)
    r = x - hi.astype(F32)
    mid = r.astype(BF16)
    lo = (r - mid.astype(F32)).astype(BF16)
    return hi, mid, lo


def _rms_mod(x, gain, shift, scale):
    y = x * lax.rsqrt(jnp.mean(x * x, axis=-1, keepdims=True) + NORM_EPS)
    return (y * gain) * (1.0 + scale) + shift


def _mod_rows(ref, g):
    if ref.shape[0] == 1:
        return ref[...]
    return ref[g * NORM_ROWS:(g + 1) * NORM_ROWS, :]


def _for_row_groups(n_rows, fn):
    def tile(t, carry):
        base = pl.multiple_of(t * ROW_TILE, ROW_TILE)
        for g in range(ROW_TILE // NORM_ROWS):
            fn(pl.ds(base + g * NORM_ROWS, NORM_ROWS), g)
        return carry
    lax.fori_loop(0, n_rows // ROW_TILE, tile, 0)


def _params(*sem):
    return pltpu.CompilerParams(dimension_semantics=sem, vmem_limit_bytes=VMEM_LIMIT_BYTES)


def _ada_kernel(c_ref, w_ref, b_ref, o_ref):
    a = _silu(c_ref[...]).astype(BF16)
    o_ref[...] = _dot(a, w_ref[...].astype(BF16)) + b_ref[...]


def _ada_proj(c, w, b, tn):
    m, d = c.shape
    n = w.shape[1]
    return pl.pallas_call(
        _ada_kernel,
        grid=(n // tn,),
        in_specs=[
            pl.BlockSpec((m, d), lambda j: (0, 0)),
            pl.BlockSpec((d, tn), lambda j: (0, j)),
            pl.BlockSpec((1, tn), lambda j: (0, j)),
        ],
        out_specs=pl.BlockSpec((m, tn), lambda j: (0, j)),
        out_shape=jax.ShapeDtypeStruct((m, n), F32),
        compiler_params=_params("parallel"),
        name="ada_proj",
    )(c, w, b.reshape(1, n))


def _ffn_kernel(x_ref, sh_ref, sc_ref, gt_ref, gain_ref, wg_ref, wu_ref, wd_ref,
                gain2_ref, sh2_ref, sc2_ref, *rest, emit_x):
    if emit_x:
        xo_ref, hn_hbm, h_scr, stage_scr, stage_sem = rest
        acc_ref = xo_ref
    else:
        hn_hbm, h_scr, stage_scr, stage_sem, acc_ref = rest
    j = pl.program_id(1)
    tm = x_ref.shape[0]

    @pl.when(j == 0)
    def _():
        def group(rows, g):
            h = _rms_mod(x_ref[rows, :], gain_ref[...], _mod_rows(sh_ref, g), _mod_rows(sc_ref, g))
            h_scr[rows, :] = h.astype(BF16)
            acc_ref[rows, :] = jnp.zeros((NORM_ROWS, acc_ref.shape[1]), F32)
        _for_row_groups(tm, group)

    wg = wg_ref[...].astype(BF16)
    wu = wu_ref[...].astype(BF16)
    wd = wd_ref[...].astype(BF16)
    sub = min(tm, FFN_SUB_ROWS)
    for r in range(tm // sub):
        rows = slice(r * sub, (r + 1) * sub)
        h = h_scr[rows, :]
        a = (_silu(_dot(h, wg)) * _dot(h, wu)).astype(BF16)
        acc_ref[rows, :] += _dot(a, wd)

    @pl.when(j == pl.num_programs(1) - 1)
    def _():
        i = pl.program_id(0)
        n_tiles = tm // ROW_TILE

        def hn_copy(t, block):
            row = pl.multiple_of(block * tm + t * ROW_TILE, ROW_TILE)
            return pltpu.make_async_copy(stage_scr.at[t], hn_hbm.at[pl.ds(row, ROW_TILE), :], stage_sem.at[t])

        def wait_block(block):
            for t in range(n_tiles):
                hn_copy(t, block).wait()

        @pl.when(i > 0)
        def _():
            wait_block(i - 1)

        def tile(t, carry):
            base = pl.multiple_of(t * ROW_TILE, ROW_TILE)
            for g in range(ROW_TILE // NORM_ROWS):
                rows = pl.ds(base + g * NORM_ROWS, NORM_ROWS)
                xo = x_ref[rows, :] + (MACARON_WEIGHT * _mod_rows(gt_ref, g)) * acc_ref[rows, :]
                if emit_x:
                    xo_ref[rows, :] = xo
                hn = _rms_mod(xo, gain2_ref[...], _mod_rows(sh2_ref, g), _mod_rows(sc2_ref, g))
                stage_scr[t, g * NORM_ROWS:(g + 1) * NORM_ROWS, :] = hn.astype(stage_scr.dtype)
            hn_copy(t, i).start()
            return carry
        lax.fori_loop(0, n_tiles, tile, 0)

        @pl.when(i == pl.num_programs(0) - 1)
        def _():
            wait_block(i)


def _mod_spec(mod, k, d, rows_per_group, tm):
    if mod.ndim == 3:
        blocks_per_group = rows_per_group // tm
        return pl.BlockSpec((None, 1, d), lambda i, j: (i // blocks_per_group, 0, k))
    return pl.BlockSpec((ROW_TILE, d), lambda i, j: (0, k))


def _ffn(x, mod, ks, gain, wg, wu, wd, mod2, ks2, gain2, *, rows_per_group, tm, tf, emit_x, hn_dtype):
    m, d = x.shape
    f = wg.shape[1]
    assert m % tm == 0 and f % tf == 0 and tm % ROW_TILE == 0 and d % LANES == 0
    assert mod.ndim == 2 or rows_per_group % tm == 0
    k_sh, k_sc, k_gt = ks
    k_sh2, k_sc2 = ks2
    vec = lambda: pl.BlockSpec((1, d), lambda i, j: (0, 0))
    row_block = lambda: pl.BlockSpec((tm, d), lambda i, j: (i, 0))
    out_shape = [jax.ShapeDtypeStruct((m, d), hn_dtype)]
    out_specs = [pl.BlockSpec(memory_space=pl.ANY)]
    if emit_x:
        out_shape = [jax.ShapeDtypeStruct((m, d), F32)] + out_shape
        out_specs = [row_block()] + out_specs
    return pl.pallas_call(
        functools.partial(_ffn_kernel, emit_x=emit_x),
        grid=(m // tm, f // tf),
        in_specs=[
            row_block(),
            _mod_spec(mod, k_sh, d, rows_per_group, tm),
            _mod_spec(mod, k_sc, d, rows_per_group, tm),
            _mod_spec(mod, k_gt, d, rows_per_group, tm),
            vec(),
            pl.BlockSpec((d, tf), lambda i, j: (0, j)),
            pl.BlockSpec((d, tf), lambda i, j: (0, j)),
            pl.BlockSpec((tf, d), lambda i, j: (j, 0)),
            vec(),
            _mod_spec(mod2, k_sh2, d, rows_per_group, tm),
            _mod_spec(mod2, k_sc2, d, rows_per_group, tm),
        ],
        out_specs=out_specs,
        out_shape=out_shape,
        scratch_shapes=[pltpu.VMEM((tm, d), BF16), pltpu.VMEM((tm // ROW_TILE, ROW_TILE, d), hn_dtype),
                        pltpu.SemaphoreType.DMA((tm // ROW_TILE,))]
        + ([] if emit_x else [pltpu.VMEM((tm, d), F32)]),
        compiler_params=_params("arbitrary", "arbitrary"),
        name="ffn",
    )(x, mod, mod, mod, gain.reshape(1, d), wg, wu, wd, gain2.reshape(1, d), mod2, mod2)


def _proj_kernel(h_ref, wt_ref, o_ref, wb_scr):
    @pl.when(pl.program_id(1) == 0)
    def _():
        wb_scr[...] = wt_ref[...].astype(BF16)

    o_ref[...] = _dot_nt(h_ref[...], wb_scr[...])


def _proj(h, wt, row_of, n_cols, tm, tn):
    m, d = h.shape
    return pl.pallas_call(
        _proj_kernel,
        grid=(n_cols // tn, m // tm),
        in_specs=[
            pl.BlockSpec((tm, d), lambda j, i: (i, 0)),
            pl.BlockSpec((pl.Element(tn), pl.Element(d)),
                         lambda j, i: (pl.multiple_of(row_of(j), SUBLANES), 0)),
        ],
        out_specs=pl.BlockSpec((tm, tn), lambda j, i: (i, j)),
        out_shape=jax.ShapeDtypeStruct((m, n_cols), F32),
        scratch_shapes=[pltpu.VMEM((tn, d), BF16)],
        compiler_params=_params("parallel", "arbitrary"),
        name="in_proj",
    )(h, wt)


def _proj_conv_kernel(h_ref, wt_ref, convw_ref, o_ref, cnew_ref, wb_scr, tail_scr, *, blocks_per_seq):
    i = pl.program_id(1)

    @pl.when(i == 0)
    def _():
        wb_scr[...] = wt_ref[...].astype(BF16)

    h = h_ref[...]
    tm, tn = o_ref.shape
    first = i % blocks_per_seq == 0
    for cg in range(tn // MXU_COLS):
        cols = slice(cg * MXU_COLS, (cg + 1) * MXU_COLS)
        raw = _dot_nt(h, wb_scr[cols, :])
        w = convw_ref[:, cols]
        prev = jnp.where(first, 0.0, tail_scr[:, cols])
        for r in range(tm // CONV_SUB_ROWS):
            rows = slice(r * CONV_SUB_ROWS, (r + 1) * CONV_SUB_ROWS)
            x = raw[rows, :]
            y, _ = _causal_conv_silu(prev, x, w)
            o_ref[rows, cols] = y
            prev = x[CONV_SUB_ROWS - HISTORY_ROWS:, :]
        tail_scr[:, cols] = prev
        cnew_ref[:, cols] = prev


def _proj_conv(h, wt, conv_w, n_cols, rows_per_seq, tm, tn):
    m, d = h.shape
    blocks_per_seq = rows_per_seq // tm
    return pl.pallas_call(
        functools.partial(_proj_conv_kernel, blocks_per_seq=blocks_per_seq),
        grid=(n_cols // tn, m // tm),
        in_specs=[
            pl.BlockSpec((tm, d), lambda j, i: (i, 0)),
            pl.BlockSpec((tn, d), lambda j, i: (j, 0)),
            pl.BlockSpec((CONV_WIDTH, tn), lambda j, i: (0, j)),
        ],
        out_specs=[
            pl.BlockSpec((tm, tn), lambda j, i: (i, j)),
            pl.BlockSpec((None, HISTORY_ROWS, tn), lambda j, i: (i // blocks_per_seq, 0, j)),
        ],
        out_shape=[
            jax.ShapeDtypeStruct((m, n_cols), F32),
            jax.ShapeDtypeStruct((m // rows_per_seq, HISTORY_ROWS, n_cols), F32),
        ],
        scratch_shapes=[pltpu.VMEM((tn, d), BF16), pltpu.VMEM((HISTORY_ROWS, tn), F32)],
        compiler_params=_params("parallel", "arbitrary"),
        name="in_proj_conv",
    )(h, wt, conv_w)


def _outproj_kernel(x_ref, gt_ref, og_ref, or_ref, w_ref, o_ref, wb_scr):
    @pl.when(pl.program_id(0) == 0)
    def _():
        wb_scr[...] = w_ref[...].astype(BF16)

    kw = og_ref.shape[1]
    tm = x_ref.shape[0]
    mixed = _dot(og_ref[...], wb_scr[:kw, :]) + _dot(or_ref[...], wb_scr[kw:, :])
    for r in range(tm // ROW_TILE):
        rows = slice(r * ROW_TILE, (r + 1) * ROW_TILE)
        o_ref[rows, :] = x_ref[rows, :] + gt_ref[...] * mixed[rows, :]


def _outproj(x, mod, k_gt, og, orr, w_out, *, rows_per_group, tm):
    m, d = x.shape
    kw = og.shape[1]
    if mod.ndim == 3:
        blocks_per_group = rows_per_group // tm
        gt_spec = pl.BlockSpec((None, 1, d), lambda i: (i // blocks_per_group, 0, k_gt))
    else:
        gt_spec = pl.BlockSpec((ROW_TILE, d), lambda i: (0, k_gt))
    return pl.pallas_call(
        _outproj_kernel,
        grid=(m // tm,),
        in_specs=[
            pl.BlockSpec((tm, d), lambda i: (i, 0)),
            gt_spec,
            pl.BlockSpec((tm, kw), lambda i: (i, 0)),
            pl.BlockSpec((tm, kw), lambda i: (i, 0)),
            pl.BlockSpec(w_out.shape, lambda i: (0, 0), pipeline_mode=pl.Buffered(1)),
        ],
        out_specs=pl.BlockSpec((tm, d), lambda i: (i, 0)),
        out_shape=jax.ShapeDtypeStruct((m, d), F32),
        scratch_shapes=[pltpu.VMEM(w_out.shape, BF16)],
        compiler_params=_params("arbitrary"),
        name="out_proj",
    )(x, mod, og, orr, w_out)


SOLVE_BLOCK = 16


def _nilpotent_apply(chains, index):
    power = 1
    while power < index:
        square = 2 * power < index
        for ch in chains:
            right = jnp.concatenate([ch["sol"], ch["nil"]], axis=1) if square else ch["sol"]
            ch["prod"] = _dot(ch["nil"].astype(BF16), right.astype(BF16))
        for ch in chains:
            width = ch["sol"].shape[1]
            upd = ch["prod"][:, :width]
            ch["sol"] = ch["sol"] - upd if power == 1 else ch["sol"] + upd
            if square:
                ch["nil"] = ch["prod"][:, width:]
        power *= 2


def _causal_conv_silu(prev, x, w):
    xp = jnp.concatenate([prev, x], axis=0)
    acc = x * w[CONV_WIDTH - 1:CONV_WIDTH, :]
    for sft in range(1, CONV_WIDTH):
        tap = CONV_WIDTH - 1 - sft
        acc = acc + pltpu.roll(xp, sft, 0)[HISTORY_ROWS:, :] * w[tap:tap + 1, :]
    return _silu(acc), xp


def _gdn_kernel(*refs, valid_rows, heads, dk, dv, conv_here):
    if conv_here:
        (qkv_ref, z_ref, ba_ref, s0_ref, alog_ref, dtb_ref, nw_ref, cinit_ref, convw_ref,
         o_ref, snew_ref, cnew_ref, s_scr, tail_scr) = refs
    else:
        qkv_ref, z_ref, ba_ref, s0_ref, alog_ref, dtb_ref, nw_ref, o_ref, snew_ref, s_scr = refs
    n = pl.program_id(1)
    last = pl.num_programs(1) - 1
    nb, c, _ = qkv_ref.shape

    @pl.when(n == 0)
    def _():
        s_scr[...] = s0_ref[...]
        if conv_here:
            tail_scr[...] = cinit_ref[...]

    row = lax.broadcasted_iota(jnp.int32, (c, c), 0)
    col = lax.broadcasted_iota(jnp.int32, (c, c), 1)
    causal = row >= col
    strict = row > col
    diag = row == col
    tril = jnp.where(causal, 1.0, 0.0).astype(BF16)
    nw = nw_ref[...]
    neg_a = -jnp.exp(alog_ref[...])
    dtb = dtb_ref[...]

    chains = []
    for i in range(nb):
        if conv_here:
            x = qkv_ref[i]
            qkv, xp = _causal_conv_silu(tail_scr[i], x, convw_ref[...])
            tail_scr[i] = x[c - HISTORY_ROWS:, :]

            @pl.when(n == last)
            def _(i=i, xp=xp):
                cnew_ref[i] = xp[valid_rows:valid_rows + HISTORY_ROWS, :]
        else:
            qkv = qkv_ref[i]

        ba = ba_ref[i]
        beta_all = _sigmoid(ba)
        g_all = neg_a * _softplus(ba + dtb)
        if valid_rows < c:
            live = lax.broadcasted_iota(jnp.int32, ba.shape, 0) < valid_rows
            beta_all = jnp.where(live, beta_all, 0.0)
            g_all = jnp.where(live, g_all, 0.0)
        g_hi, g_mid, g_lo = _split3(g_all)
        gc_all = _dot(tril, g_hi) + (_dot(tril, g_mid) + _dot(tril, g_lo))
        gc_last_all = gc_all[c - 1:c, :]
        eg_all = jnp.exp(gc_all)
        kdec_all = jnp.exp(gc_last_all - gc_all)
        gl_all = jnp.exp(gc_last_all)

        for h in range(heads):
            beta = beta_all[:, h:h + 1]
            gc = gc_all[:, heads + h:heads + h + 1]
            eg = eg_all[:, heads + h:heads + h + 1]
            q = qkv[:, h * dk:(h + 1) * dk]
            k = qkv[:, heads * dk + h * dk:heads * dk + (h + 1) * dk]
            v = qkv[:, 2 * heads * dk + h * dv:2 * heads * dk + (h + 1) * dv]
            q = q * (lax.rsqrt(jnp.sum(q * q, axis=-1, keepdims=True) + L2_EPS) * (dk ** -0.5))
            k = k * lax.rsqrt(jnp.sum(k * k, axis=-1, keepdims=True) + L2_EPS)
            kb = k * beta
            gc_col = jnp.broadcast_to(gc, (c, c))
            gc_row = jnp.sum(jnp.where(diag, gc_col, 0.0), axis=0, keepdims=True)
            decay = jnp.where(causal, jnp.exp(gc_col - gc_row), 0.0)
            chains.append(dict(
                i=i, h=h, decay=decay,
                kbq=jnp.concatenate([kb, q], axis=0).astype(BF16), k_bf=k.astype(BF16),
                rhs=jnp.concatenate([v * beta, kb * eg], axis=1),
                q_dec=q * eg,
                k_dec=(k * kdec_all[:, heads + h:heads + h + 1]).astype(BF16),
                gl=gl_all[:, heads + h:heads + h + 1]))

    for ch in chains:
        ch["kq"] = _dot_nt(ch["kbq"], ch["k_bf"])
    for ch in chains:
        ch["a"] = jnp.where(strict, ch["kq"][:c] * ch["decay"], 0.0)
        ch["qk"] = (ch["kq"][c:] * ch["decay"]).astype(BF16)

    if valid_rows <= SOLVE_BLOCK:
        for ch in chains:
            ch["nil"], ch["sol"] = ch["a"], ch["rhs"]
        _nilpotent_apply(chains, valid_rows)
    else:
        same_block = (row // SOLVE_BLOCK) == (col // SOLVE_BLOCK)
        eye = jnp.where(diag, 1.0, 0.0)
        for ch in chains:
            ch["nil"] = jnp.where(same_block, ch["a"], 0.0)
            ch["sol"] = eye
        _nilpotent_apply(chains, SOLVE_BLOCK)
        for ch in chains:
            off_block = jnp.where(same_block, 0.0, ch["a"])
            right = jnp.concatenate([ch["rhs"], off_block], axis=1)
            ch["prod"] = _dot(ch["sol"].astype(BF16), right.astype(BF16))
        for ch in chains:
            ch["sol"] = ch["prod"][:, :dv + dk]
            ch["nil"] = ch["prod"][:, dv + dk:]
        _nilpotent_apply(chains, c // SOLVE_BLOCK)

    for ch in chains:
        ch["s"] = s_scr[ch["i"], ch["h"]]
        lhs = jnp.concatenate([ch["sol"][:, dv:], ch["q_dec"]], axis=0).astype(BF16)
        ch["ws_qs"] = _dot(lhs, ch["s"].astype(BF16))
    for ch in chains:
        ch["v_new"] = (ch["sol"][:, :dv] - ch["ws_qs"][:c]).astype(BF16)
    for ch in chains:
        s_scr[ch["i"], ch["h"]] = ch["s"] * ch["gl"] + _dot_tn(ch["k_dec"], ch["v_new"])
        ch["o"] = ch["ws_qs"][c:] + _dot(ch["qk"], ch["v_new"])
    for ch in chains:
        i, h, o = ch["i"], ch["h"], ch["o"]
        o = o * lax.rsqrt(jnp.mean(o * o, axis=-1, keepdims=True) + NORM_EPS) * nw
        o = o * _silu(z_ref[i, :, h * dv:(h + 1) * dv])
        o_ref[i, :, h * dv:(h + 1) * dv] = o.astype(o_ref.dtype)

    @pl.when(n == last)
    def _():
        snew_ref[...] = s_scr[...]


def _gdn(qkv_src, z_src, proj_ba, s0, a_log, dt_bias, norm_w, conv=None, *, chunk, valid_rows, nb, heads, dk, dv):
    (qkv_arr, qkv_col), (z_arr, z_col) = qkv_src, z_src
    b, t, _ = qkv_arr.shape
    assert valid_rows <= SOLVE_BLOCK or (valid_rows == chunk and chunk % SOLVE_BLOCK == 0)
    conv_ch = heads * (2 * dk + dv)
    width = heads * dv
    n_chunks = t // chunk
    qkv_blk = qkv_col // conv_ch
    z_blk = z_col // width
    lane_vec = lambda x: jnp.zeros((1, LANES), F32).at[0, heads:2 * heads].set(x.astype(F32))
    state_spec = pl.BlockSpec((nb, heads, dk, dv), lambda i, n: (i, 0, 0, 0))
    hist_spec = pl.BlockSpec((nb, HISTORY_ROWS, conv_ch), lambda i, n: (i, 0, 0))
    lane_spec = pl.BlockSpec((1, LANES), lambda i, n: (0, 0))
    in_specs = [
        pl.BlockSpec((nb, chunk, conv_ch), lambda i, n: (i, n, qkv_blk)),
        pl.BlockSpec((nb, chunk, width), lambda i, n: (i, n, z_blk)),
        pl.BlockSpec((nb, chunk, LANES), lambda i, n: (i, n, 0)),
        state_spec, lane_spec, lane_spec,
        pl.BlockSpec((1, dv), lambda i, n: (0, 0)),
    ]
    args = [qkv_arr, z_arr, proj_ba, s0, lane_vec(a_log), lane_vec(dt_bias), norm_w.reshape(1, dv)]
    out_specs = [pl.BlockSpec((nb, chunk, width), lambda i, n: (i, n, 0)), state_spec]
    out_shape = [jax.ShapeDtypeStruct((b, t, width), BF16), jax.ShapeDtypeStruct((b, heads, dk, dv), F32)]
    scratch = [pltpu.VMEM((nb, heads, dk, dv), F32)]
    if conv is not None:
        conv_init, conv_w = conv
        in_specs += [hist_spec, pl.BlockSpec((CONV_WIDTH, conv_ch), lambda i, n: (0, 0))]
        args += [conv_init, conv_w]
        out_specs.append(hist_spec)
        out_shape.append(jax.ShapeDtypeStruct((b, HISTORY_ROWS, conv_ch), F32))
        scratch.append(pltpu.VMEM((nb, HISTORY_ROWS, conv_ch), F32))
    return pl.pallas_call(
        functools.partial(_gdn_kernel, valid_rows=valid_rows, heads=heads, dk=dk, dv=dv,
                          conv_here=conv is not None),
        grid=(b // nb, n_chunks),
        in_specs=in_specs,
        out_specs=out_specs,
        out_shape=out_shape,
        scratch_shapes=scratch,
        compiler_params=_params("parallel", "arbitrary"),
        name="gdn_mixer",
    )(*args)


def _ret_kernel(q_ref, k_ref, v_ref, g_ref, s0_ref, invf_ref, gnw_ref, gnb_ref,
                o_ref, snew_ref, s_scr, *, valid_rows, pos0, heads, dk, dv):
    n = pl.program_id(1)
    nb, c, _ = q_ref.shape
    half = dk // 2

    @pl.when(n == 0)
    def _():
        s_scr[...] = s0_ref[...]

    t_idx = lax.broadcasted_iota(jnp.int32, (c, dk), 0)
    pos = (pos0 + n * c + t_idx).astype(F32)
    ang = pos * invf_ref[...]
    cos2 = jnp.cos(ang)
    lane = lax.broadcasted_iota(jnp.int32, (c, dk), 1)
    sin2 = jnp.where(lane < half, -jnp.sin(ang), jnp.sin(ang))

    row = lax.broadcasted_iota(jnp.int32, (c, c), 0)
    col = lax.broadcasted_iota(jnp.int32, (c, c), 1)
    rel = (row - col).astype(F32)
    idx = lax.broadcasted_iota(jnp.int32, (c, 1), 0).astype(F32)
    live = lax.broadcasted_iota(jnp.int32, (c, 1), 0) < valid_rows

    chains = []
    for h in range(heads):
        log_gamma = math.log(1.0 - 2.0 ** (-5.0 - h))
        dmat = jnp.where(rel >= 0, jnp.exp(jnp.maximum(rel, 0.0) * log_gamma), 0.0)
        q_scale = jnp.exp((idx + 1.0) * log_gamma)
        k_scale = jnp.exp((valid_rows - 1.0 - idx) * log_gamma)
        for i in range(nb):
            q = q_ref[i, :, h * dk:(h + 1) * dk]
            k = k_ref[i, :, h * dk:(h + 1) * dk]
            v = v_ref[i, :, h * dv:(h + 1) * dv]
            if valid_rows < c:
                v = jnp.where(live, v, 0.0)
            q = q * cos2 + pltpu.roll(q, half, 1) * sin2
            k = (k * cos2 + pltpu.roll(k, half, 1) * sin2) * (dk ** -0.5)
            chains.append(dict(
                i=i, h=h, dmat=dmat, s_decay=math.exp(valid_rows * log_gamma),
                q_bf=q.astype(BF16), k_bf=k.astype(BF16), v_bf=v.astype(BF16),
                q_dec=(q * q_scale).astype(BF16), k_dec=(k * k_scale).astype(BF16)))

    for ch in chains:
        ch["inner"] = (_dot_nt(ch["q_bf"], ch["k_bf"]) * ch["dmat"]).astype(BF16)
    for ch in chains:
        s = s_scr[ch["i"], ch["h"]]
        ch["o"] = _dot(ch["q_dec"], s.astype(BF16)) + _dot(ch["inner"], ch["v_bf"])
        s_scr[ch["i"], ch["h"]] = s * ch["s_decay"] + _dot_tn(ch["k_dec"], ch["v_bf"])
    for ch in chains:
        i, h, o = ch["i"], ch["h"], ch["o"]
        mu = jnp.mean(o, axis=-1, keepdims=True)
        oc = o - mu
        var = jnp.mean(oc * oc, axis=-1, keepdims=True)
        o = oc * lax.rsqrt(var + NORM_EPS)
        o = o * gnw_ref[:, h * dv:(h + 1) * dv] + gnb_ref[:, h * dv:(h + 1) * dv]
        o = o * _silu(g_ref[i, :, h * dv:(h + 1) * dv])
        o_ref[i, :, h * dv:(h + 1) * dv] = o.astype(o_ref.dtype)

    @pl.when(n == pl.num_programs(1) - 1)
    def _():
        snew_ref[...] = s_scr[...]


def _ret(proj, col0, s0, gn_w, gn_b, *, chunk, valid_rows, nb, pos0, heads, dk, dv):
    b, t, _ = proj.shape
    width = heads * dv
    n_chunks = t // chunk
    blk0 = col0 // width
    half = dk // 2
    inv_freq = ROPE_BASE ** (-jnp.arange(half, dtype=F32) / half)
    inv_freq2 = jnp.concatenate([inv_freq, inv_freq]).reshape(1, dk)
    col_spec = lambda kk: pl.BlockSpec((nb, chunk, width), lambda i, n: (i, n, blk0 + kk))
    return pl.pallas_call(
        functools.partial(_ret_kernel, valid_rows=valid_rows, pos0=pos0, heads=heads, dk=dk, dv=dv),
        grid=(b // nb, n_chunks),
        in_specs=[
            col_spec(0), col_spec(1), col_spec(2), col_spec(3),
            pl.BlockSpec((nb, heads, dk, dv), lambda i, n: (i, 0, 0, 0)),
            pl.BlockSpec((1, dk), lambda i, n: (0, 0)),
            pl.BlockSpec((1, width), lambda i, n: (0, 0)),
            pl.BlockSpec((1, width), lambda i, n: (0, 0)),
        ],
        out_specs=[
            pl.BlockSpec((nb, chunk, width), lambda i, n: (i, n, 0)),
            pl.BlockSpec((nb, heads, dk, dv), lambda i, n: (i, 0, 0, 0)),
        ],
        out_shape=[
            jax.ShapeDtypeStruct((b, t, width), BF16),
            jax.ShapeDtypeStruct((b, heads, dk, dv), F32),
        ],
        scratch_shapes=[pltpu.VMEM((nb, heads, dk, dv), F32)],
        compiler_params=_params("parallel", "arbitrary"),
        name="ret_mixer",
    )(proj, proj, proj, proj, s0, inv_freq2, gn_w.reshape(1, width), gn_b.reshape(1, width))


def _trunk(x, mod, mod_f, rows_per_group, seq_rows, valid_rows, nb, pos0, to_seq, from_seq,
           conv_init, s_gdn0, s_ret0, w, tm, tf):
    heads, dk, dv = s_gdn0.shape[1:]
    x1, h2 = _ffn(x, mod, (0, 1, 2), w["norm_ffn1"], w["w1_gate"], w["w1_up"], w["w1_down"],
                  mod, (3, 4), w["norm_mix"], rows_per_group=rows_per_group, tm=tm, tf=tf,
                  emit_x=True, hn_dtype=BF16)
    conv_ch = heads * (2 * dk + dv)
    width = heads * dv
    ba0 = conv_ch + width
    tn = PROJ_COLS
    proj_ba = to_seq(_proj(h2, w["w_in_t"], lambda j: ba0, LANES, tm, LANES))
    chunk = min(CHUNK, seq_rows)
    nb_gdn, nb_ret = nb
    gdn_args = dict(chunk=chunk, valid_rows=valid_rows, nb=nb_gdn, heads=heads, dk=dk, dv=dv)
    if conv_init is None:
        qkv, conv_new = _proj_conv(h2, w["w_in_t"], w["conv_w"], conv_ch, seq_rows, tm, tn)
        skip = lambda j: conv_ch + j * tn + jnp.where(j > 0, 2 * heads, 0)
        rest = to_seq(_proj(h2, w["w_in_t"], skip, 5 * width, tm, tn))
        o_gdn, s_gdn = _gdn((to_seq(qkv), 0), (rest, 0), proj_ba, s_gdn0, w["a_log"], w["dt_bias"],
                            w["gdn_norm_w"], **gdn_args)
        ret_src, ret_col = rest, width
    else:
        skip = lambda j: j * tn + jnp.where(j * tn >= ba0, 2 * heads, 0)
        proj = to_seq(_proj(h2, w["w_in_t"], skip, ba0 + 4 * width, tm, tn))
        o_gdn, s_gdn, conv_new = _gdn((proj, 0), (proj, conv_ch), proj_ba, s_gdn0, w["a_log"], w["dt_bias"],
                                      w["gdn_norm_w"], (conv_init, w["conv_w"]), **gdn_args)
        ret_src, ret_col = proj, ba0
    o_ret, s_ret = _ret(ret_src, ret_col, s_ret0, w["ret_gn_w"], w["ret_gn_b"],
                        chunk=chunk, valid_rows=valid_rows, nb=nb_ret, pos0=pos0, heads=heads, dk=dk, dv=dv)
    o_gdn = from_seq(o_gdn)
    o_ret = from_seq(o_ret)
    x2 = _outproj(x1, mod, 5, o_gdn, o_ret, w["w_out"], rows_per_group=rows_per_group, tm=min(tm, OUTPROJ_ROWS))
    (y,) = _ffn(x2, mod, (6, 7, 8), w["norm_ffn2"], w["w2_gate"], w["w2_up"], w["w2_down"],
                mod_f, (0, 1), w["norm_final"], rows_per_group=rows_per_group, tm=tm, tf=tf,
                emit_x=False, hn_dtype=F32)
    return y, s_gdn, conv_new[:, SUBLANES - (CONV_WIDTH - 1):, :], s_ret


def kernel(x_prompt, x_sample, state_gdn, state_conv, state_ret, c_prompt, c_sample, w_ada, b_ada, norm_ffn1, w1_gate, w1_up, w1_down, norm_mix, w_in, conv_w, a_log, dt_bias, gdn_norm_w, ret_gn_w, ret_gn_b, w_out, norm_ffn2, w2_gate, w2_up, w2_down, w_ada_final, b_ada_final, norm_final):
    bp, tp, d = x_prompt.shape
    bs, ts, _ = x_sample.shape
    depth, _, heads, dk, dv = state_gdn.shape
    assert depth == 1, "single-layer trunk"
    assert bs == ROW_TILE, "time-major sample rows must align adaLN vectors with row tiles"
    conv_ch = state_conv.shape[-1]
    width = heads * dv
    ba0 = conv_ch + width
    ret0 = ba0 + 2 * heads

    n_c = bs + bp
    n_c_pad = -(-n_c // SUBLANES) * SUBLANES
    c_all = jnp.concatenate([c_sample, c_prompt, jnp.zeros((n_c_pad - n_c, d), F32)], axis=0)
    ada = _ada_proj(c_all, w_ada[0], b_ada[0], tn=ADA_COLS)
    ada_f = _ada_proj(c_all, w_ada_final, b_ada_final, tn=ADA_COLS)
    mod_s, mod_p = ada, ada[bs:bs + bp].reshape(bp, 1, N_ADA * d)
    modf_s, modf_p = ada_f, ada_f[bs:bs + bp].reshape(bp, 1, 2 * d)

    assert ba0 % PROJ_COLS == 0 and ret0 % SUBLANES == 0 and w_in.shape[-1] == ret0 + 4 * width
    w = dict(norm_ffn1=norm_ffn1[0], w1_gate=w1_gate[0], w1_up=w1_up[0], w1_down=w1_down[0],
             norm_mix=norm_mix[0], w_in_t=w_in[0].T, conv_w=conv_w[0], a_log=a_log[0],
             dt_bias=dt_bias[0], gdn_norm_w=gdn_norm_w[0], ret_gn_w=ret_gn_w[0], ret_gn_b=ret_gn_b[0],
             w_out=w_out[0], norm_ffn2=norm_ffn2[0], w2_gate=w2_gate[0], w2_up=w2_up[0],
             w2_down=w2_down[0], norm_final=norm_final)

    zeros_state = jnp.zeros((bp, heads, dk, dv), F32)
    y_p, gdn_p, conv_p, ret_p = _trunk(
        x_prompt.reshape(bp * tp, d), mod_p, modf_p, tp, tp, min(CHUNK, tp), (bp, SEQS_PER_STEP_FULL), 0,
        lambda a: a.reshape(bp, tp, a.shape[-1]), lambda a: a.reshape(bp * tp, a.shape[-1]),
        None, zeros_state, zeros_state, w, tm=FFN_ROWS, tf=FFN_COLS)

    ts_pad = -(-ts // SUBLANES) * SUBLANES
    to_seq = lambda a: jnp.pad(a.reshape(ts, bs, a.shape[-1]).transpose(1, 0, 2),
                               ((0, 0), (0, ts_pad - ts), (0, 0)))
    from_seq = lambda a: a[:, :ts].transpose(1, 0, 2).reshape(ts * bs, a.shape[-1])
    conv_init = jnp.pad(state_conv[0], ((0, 0), (SUBLANES - (CONV_WIDTH - 1), 0), (0, 0)))
    y_s, gdn_s, conv_s, ret_s = _trunk(
        x_sample.transpose(1, 0, 2).reshape(ts * bs, d), mod_s, modf_s, None, ts_pad, ts,
        (SEQS_PER_STEP_SHORT, SEQS_PER_STEP_SHORT), PAST_LEN,
        to_seq, from_seq, conv_init, state_gdn[0], state_ret[0], w, tm=ts * bs, tf=FFN_COLS_SMALL)
    y_s = y_s.reshape(ts, bs, d).transpose(1, 0, 2)

    return (y_p.reshape(bp, tp, d), y_s, gdn_p[None], conv_p[None], ret_p[None],
            gdn_s[None], conv_s[None], ret_s[None])
```

```python
import functools
import math

import jax
import jax.numpy as jnp
from jax import lax
from jax.experimental import pallas as pl
from jax.experimental.pallas import tpu as pltpu

F32 = jnp.float32
BF16 = jnp.bfloat16

LANES = 128
SUBLANES = 8
MXU_COLS = 256
VMEM_LIMIT_BYTES = 60 * 1024 * 1024

PAST_LEN = 16384
CONV_WIDTH = 4
CHUNK = 64
ROPE_BASE = 10000.0
NORM_EPS = 1e-6
L2_EPS = 1e-6
MACARON_WEIGHT = 0.5
N_ADA = 9
FFN_ROWS = 1024
FFN_COLS = 256
FFN_COLS_SMALL = 512
OUTPROJ_ROWS = 512
ADA_COLS = 512
SEQS_PER_STEP_FULL = 2
SEQS_PER_STEP_SHORT = 8
ROW_TILE = 128
NORM_ROWS = 16
FFN_SUB_ROWS = 512
PROJ_COLS = 1024
HISTORY_ROWS = SUBLANES
CONV_SUB_ROWS = 256


def _sigmoid(x):
    return 1.0 / (1.0 + jnp.exp(-x))


def _silu(x):
    return x * _sigmoid(x)


def _softplus(x):
    return jnp.maximum(x, 0.0) + jnp.log(1.0 + jnp.exp(-jnp.abs(x)))


def _dot(a, b):
    return jnp.dot(a, b, preferred_element_type=F32)


def _dot_nt(a, b):
    return lax.dot_general(a, b, (((1,), (1,)), ((), ())), preferred_element_type=F32)


def _dot_tn(a, b):
    return lax.dot_general(a, b, (((0,), (0,)), ((), ())), preferred_element_type=F32)


def _split3(x):
    hi = x.astype(BF16)
    r = x - hi.astype(F32)
    mid = r.astype(BF16)
    lo = (r - mid.astype(F32)).astype(BF16)
    return hi, mid, lo


def _rms_mod(x, gain, shift, scale):
    y = x * lax.rsqrt(jnp.mean(x * x, axis=-1, keepdims=True) + NORM_EPS)
    return (y * gain) * (1.0 + scale) + shift


def _mod_rows(ref, g):
    if ref.shape[0] == 1:
        return ref[...]
    return ref[g * NORM_ROWS:(g + 1) * NORM_ROWS, :]


def _for_row_groups(n_rows, fn):
    def tile(t, carry):
        base = pl.multiple_of(t * ROW_TILE, ROW_TILE)
        for g in range(ROW_TILE // NORM_ROWS):
            fn(pl.ds(base + g * NORM_ROWS, NORM_ROWS), g)
        return carry
    lax.fori_loop(0, n_rows // ROW_TILE, tile, 0)


def _params(*sem):
    return pltpu.CompilerParams(dimension_semantics=sem, vmem_limit_bytes=VMEM_LIMIT_BYTES)


def _ada_kernel(c_ref, w_ref, b_ref, o_ref):
    a = _silu(c_ref[...]).astype(BF16)
    o_ref[...] = _dot(a, w_ref[...].astype(BF16)) + b_ref[...]


def _ada_proj(c, w, b, tn):
    m, d = c.shape
    n = w.shape[1]
    return pl.pallas_call(
        _ada_kernel,
        grid=(n // tn,),
        in_specs=[
            pl.BlockSpec((m, d), lambda j: (0, 0)),
            pl.BlockSpec((d, tn), lambda j: (0, j)),
            pl.BlockSpec((1, tn), lambda j: (0, j)),
        ],
        out_specs=pl.BlockSpec((m, tn), lambda j: (0, j)),
        out_shape=jax.ShapeDtypeStruct((m, n), F32),
        compiler_params=_params("parallel"),
        name="ada_proj",
    )(c, w, b.reshape(1, n))


def _ffn_kernel(x_ref, sh_ref, sc_ref, gt_ref, gain_ref, wg_ref, wu_ref, wd_ref,
                gain2_ref, sh2_ref, sc2_ref, *rest, emit_x, emit_w):
    rest = list(rest)
    xo_ref = rest.pop(0) if emit_x else None
    hn_hbm = rest.pop(0)
    w_outs = [rest.pop(0) for _ in range(3)] if emit_w else None
    h_scr, stage_scr, stage_sem = rest[:3]
    rest = rest[3:]
    acc_ref = xo_ref if emit_x else rest.pop(0)
    i, j = pl.program_id(0), pl.program_id(1)
    n_f = pl.num_programs(1)
    tm = x_ref.shape[0]
    tf = wg_ref.shape[1]

    @pl.when(j == 0)
    def _():
        def group(rows, g):
            h = _rms_mod(x_ref[rows, :], gain_ref[...], _mod_rows(sh_ref, g), _mod_rows(sc_ref, g))
            h_scr[rows, :] = h.astype(BF16)
            acc_ref[rows, :] = jnp.zeros((NORM_ROWS, acc_ref.shape[1]), F32)
        _for_row_groups(tm, group)

    if emit_w:
        wcast_g, wcast_u, wcast_d, w_sem = rest
        step = i * n_f + j
        slot = j % 2

        def w_copies(tile):
            cols = pl.ds(pl.multiple_of(tile * tf, tf), tf)
            return [pltpu.make_async_copy(wcast_g.at[slot], w_outs[0].at[:, cols], w_sem.at[slot, 0]),
                    pltpu.make_async_copy(wcast_u.at[slot], w_outs[1].at[:, cols], w_sem.at[slot, 1]),
                    pltpu.make_async_copy(wcast_d.at[slot], w_outs[2].at[cols, :], w_sem.at[slot, 2])]

        @pl.when((step >= 2) & (step < n_f + 2))
        def _():
            for copy in w_copies((step - 2) % n_f):
                copy.wait()

        wcast_g[slot] = wg_ref[...].astype(BF16)
        wcast_u[slot] = wu_ref[...].astype(BF16)
        wcast_d[slot] = wd_ref[...].astype(BF16)
        wg, wu, wd = wcast_g[slot], wcast_u[slot], wcast_d[slot]
    else:
        wg = wg_ref[...].astype(BF16)
        wu = wu_ref[...].astype(BF16)
        wd = wd_ref[...].astype(BF16)
    sub = min(tm, FFN_SUB_ROWS)
    for r in range(tm // sub):
        rows = slice(r * sub, (r + 1) * sub)
        h = h_scr[rows, :]
        a = (_silu(_dot(h, wg)) * _dot(h, wu)).astype(BF16)
        acc_ref[rows, :] += _dot(a, wd)

    if emit_w:
        @pl.when(i == 0)
        def _():
            for copy in w_copies(j):
                copy.start()

    @pl.when(j == pl.num_programs(1) - 1)
    def _():
        i = pl.program_id(0)
        n_tiles = tm // ROW_TILE

        def hn_copy(t, block):
            row = pl.multiple_of(block * tm + t * ROW_TILE, ROW_TILE)
            return pltpu.make_async_copy(stage_scr.at[t], hn_hbm.at[pl.ds(row, ROW_TILE), :], stage_sem.at[t])

        def wait_block(block):
            for t in range(n_tiles):
                hn_copy(t, block).wait()

        @pl.when(i > 0)
        def _():
            wait_block(i - 1)

        def tile(t, carry):
            base = pl.multiple_of(t * ROW_TILE, ROW_TILE)
            for g in range(ROW_TILE // NORM_ROWS):
                rows = pl.ds(base + g * NORM_ROWS, NORM_ROWS)
                xo = x_ref[rows, :] + (MACARON_WEIGHT * _mod_rows(gt_ref, g)) * acc_ref[rows, :]
                if emit_x:
                    xo_ref[rows, :] = xo
                hn = _rms_mod(xo, gain2_ref[...], _mod_rows(sh2_ref, g), _mod_rows(sc2_ref, g))
                stage_scr[t, g * NORM_ROWS:(g + 1) * NORM_ROWS, :] = hn.astype(stage_scr.dtype)
            hn_copy(t, i).start()
            return carry
        lax.fori_loop(0, n_tiles, tile, 0)

        @pl.when(i == pl.num_programs(0) - 1)
        def _():
            wait_block(i)


def _mod_spec(mod, k, d, rows_per_group, tm):
    if mod.ndim == 3:
        blocks_per_group = rows_per_group // tm
        return pl.BlockSpec((None, 1, d), lambda i, j: (i // blocks_per_group, 0, k))
    return pl.BlockSpec((ROW_TILE, d), lambda i, j: (0, k))


def _ffn(x, mod, ks, gain, wg, wu, wd, mod2, ks2, gain2, *, rows_per_group, tm, tf, emit_x, hn_dtype,
         emit_w=False):
    m, d = x.shape
    f = wg.shape[1]
    assert m % tm == 0 and f % tf == 0 and tm % ROW_TILE == 0 and d % LANES == 0
    assert mod.ndim == 2 or rows_per_group % tm == 0
    assert not emit_w or (m // tm >= 2 and (f // tf) % 2 == 0)
    k_sh, k_sc, k_gt = ks
    k_sh2, k_sc2 = ks2
    vec = lambda: pl.BlockSpec((1, d), lambda i, j: (0, 0))
    row_block = lambda: pl.BlockSpec((tm, d), lambda i, j: (i, 0))
    out_shape = [jax.ShapeDtypeStruct((m, d), hn_dtype)]
    out_specs = [pl.BlockSpec(memory_space=pl.ANY)]
    if emit_x:
        out_shape = [jax.ShapeDtypeStruct((m, d), F32)] + out_shape
        out_specs = [pl.BlockSpec((tm, d), lambda i, j: (i, 0), pipeline_mode=pl.Buffered(1))] + out_specs
    scratch = [pltpu.VMEM((tm, d), BF16), pltpu.VMEM((tm // ROW_TILE, ROW_TILE, d), hn_dtype),
               pltpu.SemaphoreType.DMA((tm // ROW_TILE,))]
    if not emit_x:
        scratch.append(pltpu.VMEM((tm, d), F32))
    if emit_w:
        out_shape += [jax.ShapeDtypeStruct(w.shape, BF16) for w in (wg, wu, wd)]
        out_specs += [pl.BlockSpec(memory_space=pl.ANY)] * 3
        scratch += [pltpu.VMEM((2, d, tf), BF16), pltpu.VMEM((2, d, tf), BF16), pltpu.VMEM((2, tf, d), BF16),
                    pltpu.SemaphoreType.DMA((2, 3))]
    return pl.pallas_call(
        functools.partial(_ffn_kernel, emit_x=emit_x, emit_w=emit_w),
        grid=(m // tm, f // tf),
        in_specs=[
            row_block(),
            _mod_spec(mod, k_sh, d, rows_per_group, tm),
            _mod_spec(mod, k_sc, d, rows_per_group, tm),
            _mod_spec(mod, k_gt, d, rows_per_group, tm),
            vec(),
            pl.BlockSpec((d, tf), lambda i, j: (0, j)),
            pl.BlockSpec((d, tf), lambda i, j: (0, j)),
            pl.BlockSpec((tf, d), lambda i, j: (j, 0)),
            vec(),
            _mod_spec(mod2, k_sh2, d, rows_per_group, tm),
            _mod_spec(mod2, k_sc2, d, rows_per_group, tm),
        ],
        out_specs=out_specs,
        out_shape=out_shape,
        scratch_shapes=scratch,
        compiler_params=_params("arbitrary", "arbitrary"),
        name="ffn",
    )(x, mod, mod, mod, gain.reshape(1, d), wg, wu, wd, gain2.reshape(1, d), mod2, mod2)


def _proj_kernel(h_ref, wt_ref, o_ref, wb_scr):
    @pl.when(pl.program_id(1) == 0)
    def _():
        wb_scr[...] = wt_ref[...].astype(BF16)

    o_ref[...] = _dot_nt(h_ref[...], wb_scr[...])


def _proj(h, wt, row_of, n_cols, tm, tn):
    m, d = h.shape
    return pl.pallas_call(
        _proj_kernel,
        grid=(n_cols // tn, m // tm),
        in_specs=[
            pl.BlockSpec((tm, d), lambda j, i: (i, 0)),
            pl.BlockSpec((pl.Element(tn), pl.Element(d)),
                         lambda j, i: (pl.multiple_of(row_of(j), SUBLANES), 0)),
        ],
        out_specs=pl.BlockSpec((tm, tn), lambda j, i: (i, j)),
        out_shape=jax.ShapeDtypeStruct((m, n_cols), F32),
        scratch_shapes=[pltpu.VMEM((tn, d), BF16)],
        compiler_params=_params("parallel", "arbitrary"),
        name="in_proj",
    )(h, wt)


def _proj_conv_kernel(h_ref, wt_ref, convw_ref, o_ref, cnew_ref, wb_scr, tail_scr, *, blocks_per_seq):
    i = pl.program_id(1)

    @pl.when(i == 0)
    def _():
        wb_scr[...] = wt_ref[...].astype(BF16)

    h = h_ref[...]
    tm, tn = o_ref.shape
    first = i % blocks_per_seq == 0
    for cg in range(tn // MXU_COLS):
        cols = slice(cg * MXU_COLS, (cg + 1) * MXU_COLS)
        raw = _dot_nt(h, wb_scr[cols, :])
        w = convw_ref[:, cols]
        prev = jnp.where(first, 0.0, tail_scr[:, cols])
        for r in range(tm // CONV_SUB_ROWS):
            rows = slice(r * CONV_SUB_ROWS, (r + 1) * CONV_SUB_ROWS)
            x = raw[rows, :]
            y, _ = _causal_conv_silu(prev, x, w)
            o_ref[rows, cols] = y
            prev = x[CONV_SUB_ROWS - HISTORY_ROWS:, :]
        tail_scr[:, cols] = prev
        cnew_ref[:, cols] = prev


def _proj_conv(h, wt, conv_w, n_cols, rows_per_seq, tm, tn):
    m, d = h.shape
    blocks_per_seq = rows_per_seq // tm
    return pl.pallas_call(
        functools.partial(_proj_conv_kernel, blocks_per_seq=blocks_per_seq),
        grid=(n_cols // tn, m // tm),
        in_specs=[
            pl.BlockSpec((tm, d), lambda j, i: (i, 0)),
            pl.BlockSpec((tn, d), lambda j, i: (j, 0)),
            pl.BlockSpec((CONV_WIDTH, tn), lambda j, i: (0, j)),
        ],
        out_specs=[
            pl.BlockSpec((tm, tn), lambda j, i: (i, j)),
            pl.BlockSpec((None, HISTORY_ROWS, tn), lambda j, i: (i // blocks_per_seq, 0, j)),
        ],
        out_shape=[
            jax.ShapeDtypeStruct((m, n_cols), F32),
            jax.ShapeDtypeStruct((m // rows_per_seq, HISTORY_ROWS, n_cols), F32),
        ],
        scratch_shapes=[pltpu.VMEM((tn, d), BF16), pltpu.VMEM((HISTORY_ROWS, tn), F32)],
        compiler_params=_params("parallel", "arbitrary"),
        name="in_proj_conv",
    )(h, wt, conv_w)


def _outproj_kernel(x_ref, gt_ref, og_ref, or_ref, w_ref, o_ref, wb_scr):
    @pl.when(pl.program_id(0) == 0)
    def _():
        wb_scr[...] = w_ref[...].astype(BF16)

    kw = og_ref.shape[1]
    tm = x_ref.shape[0]
    mixed = _dot(og_ref[...], wb_scr[:kw, :]) + _dot(or_ref[...], wb_scr[kw:, :])
    for r in range(tm // ROW_TILE):
        rows = slice(r * ROW_TILE, (r + 1) * ROW_TILE)
        o_ref[rows, :] = x_ref[rows, :] + gt_ref[...] * mixed[rows, :]


def _outproj(x, mod, k_gt, og, orr, w_out, *, rows_per_group, tm):
    m, d = x.shape
    kw = og.shape[1]
    if mod.ndim == 3:
        blocks_per_group = rows_per_group // tm
        gt_spec = pl.BlockSpec((None, 1, d), lambda i: (i // blocks_per_group, 0, k_gt))
    else:
        gt_spec = pl.BlockSpec((ROW_TILE, d), lambda i: (0, k_gt))
    return pl.pallas_call(
        _outproj_kernel,
        grid=(m // tm,),
        in_specs=[
            pl.BlockSpec((tm, d), lambda i: (i, 0)),
            gt_spec,
            pl.BlockSpec((tm, kw), lambda i: (i, 0)),
            pl.BlockSpec((tm, kw), lambda i: (i, 0)),
            pl.BlockSpec(w_out.shape, lambda i: (0, 0), pipeline_mode=pl.Buffered(1)),
        ],
        out_specs=pl.BlockSpec((tm, d), lambda i: (i, 0)),
        out_shape=jax.ShapeDtypeStruct((m, d), F32),
        scratch_shapes=[pltpu.VMEM(w_out.shape, BF16)],
        compiler_params=_params("arbitrary"),
        name="out_proj",
    )(x, mod, og, orr, w_out)


SOLVE_BLOCK = 16


def _nilpotent_apply(chains, index):
    power = 1
    while power < index:
        square = 2 * power < index
        for ch in chains:
            right = jnp.concatenate([ch["sol"], ch["nil"]], axis=1) if square else ch["sol"]
            ch["prod"] = _dot(ch["nil"].astype(BF16), right.astype(BF16))
        for ch in chains:
            width = ch["sol"].shape[1]
            upd = ch["prod"][:, :width]
            ch["sol"] = ch["sol"] - upd if power == 1 else ch["sol"] + upd
            if square:
                ch["nil"] = ch["prod"][:, width:]
        power *= 2


def _causal_conv_silu(prev, x, w):
    xp = jnp.concatenate([prev, x], axis=0)
    acc = x * w[CONV_WIDTH - 1:CONV_WIDTH, :]
    for sft in range(1, CONV_WIDTH):
        tap = CONV_WIDTH - 1 - sft
        acc = acc + pltpu.roll(xp, sft, 0)[HISTORY_ROWS:, :] * w[tap:tap + 1, :]
    return _silu(acc), xp


def _gdn_kernel(*refs, valid_rows, heads, dk, dv, conv_here):
    if conv_here:
        (qkv_ref, z_ref, ba_ref, s0_ref, alog_ref, dtb_ref, nw_ref, cinit_ref, convw_ref,
         o_ref, snew_ref, cnew_ref, s_scr, tail_scr) = refs
    else:
        qkv_ref, z_ref, ba_ref, s0_ref, alog_ref, dtb_ref, nw_ref, o_ref, snew_ref, s_scr = refs
    n = pl.program_id(1)
    last = pl.num_programs(1) - 1
    nb, c, _ = qkv_ref.shape

    @pl.when(n == 0)
    def _():
        s_scr[...] = s0_ref[...]
        if conv_here:
            tail_scr[...] = cinit_ref[...]

    row = lax.broadcasted_iota(jnp.int32, (c, c), 0)
    col = lax.broadcasted_iota(jnp.int32, (c, c), 1)
    causal = row >= col
    strict = row > col
    diag = row == col
    tril = jnp.where(causal, 1.0, 0.0).astype(BF16)
    nw = nw_ref[...]
    neg_a = -jnp.exp(alog_ref[...])
    dtb = dtb_ref[...]

    chains = []
    for i in range(nb):
        if conv_here:
            x = qkv_ref[i]
            qkv, xp = _causal_conv_silu(tail_scr[i], x, convw_ref[...])
            tail_scr[i] = x[c - HISTORY_ROWS:, :]

            @pl.when(n == last)
            def _(i=i, xp=xp):
                cnew_ref[i] = xp[valid_rows:valid_rows + HISTORY_ROWS, :]
        else:
            qkv = qkv_ref[i]

        ba = ba_ref[i]
        beta_all = _sigmoid(ba)
        g_all = neg_a * _softplus(ba + dtb)
        if valid_rows < c:
            live = lax.broadcasted_iota(jnp.int32, ba.shape, 0) < valid_rows
            beta_all = jnp.where(live, beta_all, 0.0)
            g_all = jnp.where(live, g_all, 0.0)
        g_hi, g_mid, g_lo = _split3(g_all)
        gc_all = _dot(tril, g_hi) + (_dot(tril, g_mid) + _dot(tril, g_lo))
        gc_last_all = gc_all[c - 1:c, :]
        eg_all = jnp.exp(gc_all)
        kdec_all = jnp.exp(gc_last_all - gc_all)
        gl_all = jnp.exp(gc_last_all)

        for h in range(heads):
            beta = beta_all[:, h:h + 1]
            gc = gc_all[:, heads + h:heads + h + 1]
            eg = eg_all[:, heads + h:heads + h + 1]
            q = qkv[:, h * dk:(h + 1) * dk]
            k = qkv[:, heads * dk + h * dk:heads * dk + (h + 1) * dk]
            v = qkv[:, 2 * heads * dk + h * dv:2 * heads * dk + (h + 1) * dv]
            q = q * (lax.rsqrt(jnp.sum(q * q, axis=-1, keepdims=True) + L2_EPS) * (dk ** -0.5))
            k = k * lax.rsqrt(jnp.sum(k * k, axis=-1, keepdims=True) + L2_EPS)
            kb = k * beta
            gc_col = jnp.broadcast_to(gc, (c, c))
            gc_row = jnp.sum(jnp.where(diag, gc_col, 0.0), axis=0, keepdims=True)
            decay = jnp.where(causal, jnp.exp(gc_col - gc_row), 0.0)
            chains.append(dict(
                i=i, h=h, decay=decay,
                kbq=jnp.concatenate([kb, q], axis=0).astype(BF16), k_bf=k.astype(BF16),
                rhs=jnp.concatenate([v * beta, kb * eg], axis=1),
                q_dec=q * eg,
                k_dec=(k * kdec_all[:, heads + h:heads + h + 1]).astype(BF16),
                gl=gl_all[:, heads + h:heads + h + 1]))

    for ch in chains:
        ch["kq"] = _dot_nt(ch["kbq"], ch["k_bf"])
    for ch in chains:
        ch["a"] = jnp.where(strict, ch["kq"][:c] * ch["decay"], 0.0)
        ch["qk"] = (ch["kq"][c:] * ch["decay"]).astype(BF16)

    if valid_rows <= SOLVE_BLOCK:
        for ch in chains:
            ch["nil"], ch["sol"] = ch["a"], ch["rhs"]
        _nilpotent_apply(chains, valid_rows)
    else:
        same_block = (row // SOLVE_BLOCK) == (col // SOLVE_BLOCK)
        eye = jnp.where(diag, 1.0, 0.0)
        for ch in chains:
            ch["nil"] = jnp.where(same_block, ch["a"], 0.0)
            ch["sol"] = eye
        _nilpotent_apply(chains, SOLVE_BLOCK)
        for ch in chains:
            off_block = jnp.where(same_block, 0.0, ch["a"])
            right = jnp.concatenate([ch["rhs"], off_block], axis=1)
            ch["prod"] = _dot(ch["sol"].astype(BF16), right.astype(BF16))
        for ch in chains:
            ch["sol"] = ch["prod"][:, :dv + dk]
            ch["nil"] = ch["prod"][:, dv + dk:]
        _nilpotent_apply(chains, c // SOLVE_BLOCK)

    for ch in chains:
        ch["s"] = s_scr[ch["i"], ch["h"]]
        lhs = jnp.concatenate([ch["sol"][:, dv:], ch["q_dec"]], axis=0).astype(BF16)
        ch["ws_qs"] = _dot(lhs, ch["s"].astype(BF16))
    for ch in chains:
        ch["v_new"] = (ch["sol"][:, :dv] - ch["ws_qs"][:c]).astype(BF16)
    for ch in chains:
        s_scr[ch["i"], ch["h"]] = ch["s"] * ch["gl"] + _dot_tn(ch["k_dec"], ch["v_new"])
        ch["o"] = ch["ws_qs"][c:] + _dot(ch["qk"], ch["v_new"])
    for ch in chains:
        i, h, o = ch["i"], ch["h"], ch["o"]
        o = o * lax.rsqrt(jnp.mean(o * o, axis=-1, keepdims=True) + NORM_EPS) * nw
        o = o * _silu(z_ref[i, :, h * dv:(h + 1) * dv])
        o_ref[i, :, h * dv:(h + 1) * dv] = o.astype(o_ref.dtype)

    @pl.when(n == last)
    def _():
        snew_ref[...] = s_scr[...]


def _gdn(qkv_src, z_src, proj_ba, s0, a_log, dt_bias, norm_w, conv=None, *, chunk, valid_rows, nb, heads, dk, dv):
    (qkv_arr, qkv_col), (z_arr, z_col) = qkv_src, z_src
    b, t, _ = qkv_arr.shape
    assert valid_rows <= SOLVE_BLOCK or (valid_rows == chunk and chunk % SOLVE_BLOCK == 0)
    conv_ch = heads * (2 * dk + dv)
    width = heads * dv
    n_chunks = t // chunk
    qkv_blk = qkv_col // conv_ch
    z_blk = z_col // width
    lane_vec = lambda x: jnp.zeros((1, LANES), F32).at[0, heads:2 * heads].set(x.astype(F32))
    state_spec = pl.BlockSpec((nb, heads, dk, dv), lambda i, n: (i, 0, 0, 0))
    hist_spec = pl.BlockSpec((nb, HISTORY_ROWS, conv_ch), lambda i, n: (i, 0, 0))
    lane_spec = pl.BlockSpec((1, LANES), lambda i, n: (0, 0))
    in_specs = [
        pl.BlockSpec((nb, chunk, conv_ch), lambda i, n: (i, n, qkv_blk)),
        pl.BlockSpec((nb, chunk, width), lambda i, n: (i, n, z_blk)),
        pl.BlockSpec((nb, chunk, LANES), lambda i, n: (i, n, 0)),
        state_spec, lane_spec, lane_spec,
        pl.BlockSpec((1, dv), lambda i, n: (0, 0)),
    ]
    args = [qkv_arr, z_arr, proj_ba, s0, lane_vec(a_log), lane_vec(dt_bias), norm_w.reshape(1, dv)]
    out_specs = [pl.BlockSpec((nb, chunk, width), lambda i, n: (i, n, 0)), state_spec]
    out_shape = [jax.ShapeDtypeStruct((b, t, width), BF16), jax.ShapeDtypeStruct((b, heads, dk, dv), F32)]
    scratch = [pltpu.VMEM((nb, heads, dk, dv), F32)]
    if conv is not None:
        conv_init, conv_w = conv
        in_specs += [hist_spec, pl.BlockSpec((CONV_WIDTH, conv_ch), lambda i, n: (0, 0))]
        args += [conv_init, conv_w]
        out_specs.append(hist_spec)
        out_shape.append(jax.ShapeDtypeStruct((b, HISTORY_ROWS, conv_ch), F32))
        scratch.append(pltpu.VMEM((nb, HISTORY_ROWS, conv_ch), F32))
    return pl.pallas_call(
        functools.partial(_gdn_kernel, valid_rows=valid_rows, heads=heads, dk=dk, dv=dv,
                          conv_here=conv is not None),
        grid=(b // nb, n_chunks),
        in_specs=in_specs,
        out_specs=out_specs,
        out_shape=out_shape,
        scratch_shapes=scratch,
        compiler_params=_params("parallel", "arbitrary"),
        name="gdn_mixer",
    )(*args)


def _ret_kernel(q_ref, k_ref, v_ref, g_ref, s0_ref, invf_ref, gnw_ref, gnb_ref,
                o_ref, snew_ref, s_scr, *, valid_rows, pos0, heads, dk, dv):
    n = pl.program_id(1)
    nb, c, _ = q_ref.shape
    half = dk // 2

    @pl.when(n == 0)
    def _():
        s_scr[...] = s0_ref[...]

    t_idx = lax.broadcasted_iota(jnp.int32, (c, dk), 0)
    pos = (pos0 + n * c + t_idx).astype(F32)
    ang = pos * invf_ref[...]
    cos2 = jnp.cos(ang)
    lane = lax.broadcasted_iota(jnp.int32, (c, dk), 1)
    sin2 = jnp.where(lane < half, -jnp.sin(ang), jnp.sin(ang))

    row = lax.broadcasted_iota(jnp.int32, (c, c), 0)
    col = lax.broadcasted_iota(jnp.int32, (c, c), 1)
    rel = (row - col).astype(F32)
    idx = lax.broadcasted_iota(jnp.int32, (c, 1), 0).astype(F32)
    live = lax.broadcasted_iota(jnp.int32, (c, 1), 0) < valid_rows

    chains = []
    for h in range(heads):
        log_gamma = math.log(1.0 - 2.0 ** (-5.0 - h))
        dmat = jnp.where(rel >= 0, jnp.exp(jnp.maximum(rel, 0.0) * log_gamma), 0.0)
        q_scale = jnp.exp((idx + 1.0) * log_gamma)
        k_scale = jnp.exp((valid_rows - 1.0 - idx) * log_gamma)
        for i in range(nb):
            q = q_ref[i, :, h * dk:(h + 1) * dk]
            k = k_ref[i, :, h * dk:(h + 1) * dk]
            v = v_ref[i, :, h * dv:(h + 1) * dv]
            if valid_rows < c:
                v = jnp.where(live, v, 0.0)
            q = q * cos2 + pltpu.roll(q, half, 1) * sin2
            k = (k * cos2 + pltpu.roll(k, half, 1) * sin2) * (dk ** -0.5)
            chains.append(dict(
                i=i, h=h, dmat=dmat, s_decay=math.exp(valid_rows * log_gamma),
                q_bf=q.astype(BF16), k_bf=k.astype(BF16), v_bf=v.astype(BF16),
                q_dec=(q * q_scale).astype(BF16), k_dec=(k * k_scale).astype(BF16)))

    for ch in chains:
        ch["inner"] = (_dot_nt(ch["q_bf"], ch["k_bf"]) * ch["dmat"]).astype(BF16)
    for ch in chains:
        s = s_scr[ch["i"], ch["h"]]
        ch["o"] = _dot(ch["q_dec"], s.astype(BF16)) + _dot(ch["inner"], ch["v_bf"])
        s_scr[ch["i"], ch["h"]] = s * ch["s_decay"] + _dot_tn(ch["k_dec"], ch["v_bf"])
    for ch in chains:
        i, h, o = ch["i"], ch["h"], ch["o"]
        mu = jnp.mean(o, axis=-1, keepdims=True)
        oc = o - mu
        var = jnp.mean(oc * oc, axis=-1, keepdims=True)
        o = oc * lax.rsqrt(var + NORM_EPS)
        o = o * gnw_ref[:, h * dv:(h + 1) * dv] + gnb_ref[:, h * dv:(h + 1) * dv]
        o = o * _silu(g_ref[i, :, h * dv:(h + 1) * dv])
        o_ref[i, :, h * dv:(h + 1) * dv] = o.astype(o_ref.dtype)

    @pl.when(n == pl.num_programs(1) - 1)
    def _():
        snew_ref[...] = s_scr[...]


def _ret(proj, col0, s0, gn_w, gn_b, *, chunk, valid_rows, nb, pos0, heads, dk, dv):
    b, t, _ = proj.shape
    width = heads * dv
    n_chunks = t // chunk
    blk0 = col0 // width
    half = dk // 2
    inv_freq = ROPE_BASE ** (-jnp.arange(half, dtype=F32) / half)
    inv_freq2 = jnp.concatenate([inv_freq, inv_freq]).reshape(1, dk)
    col_spec = lambda kk: pl.BlockSpec((nb, chunk, width), lambda i, n: (i, n, blk0 + kk))
    return pl.pallas_call(
        functools.partial(_ret_kernel, valid_rows=valid_rows, pos0=pos0, heads=heads, dk=dk, dv=dv),
        grid=(b // nb, n_chunks),
        in_specs=[
            col_spec(0), col_spec(1), col_spec(2), col_spec(3),
            pl.BlockSpec((nb, heads, dk, dv), lambda i, n: (i, 0, 0, 0)),
            pl.BlockSpec((1, dk), lambda i, n: (0, 0)),
            pl.BlockSpec((1, width), lambda i, n: (0, 0)),
            pl.BlockSpec((1, width), lambda i, n: (0, 0)),
        ],
        out_specs=[
            pl.BlockSpec((nb, chunk, width), lambda i, n: (i, n, 0)),
            pl.BlockSpec((nb, heads, dk, dv), lambda i, n: (i, 0, 0, 0)),
        ],
        out_shape=[
            jax.ShapeDtypeStruct((b, t, width), BF16),
            jax.ShapeDtypeStruct((b, heads, dk, dv), F32),
        ],
        scratch_shapes=[pltpu.VMEM((nb, heads, dk, dv), F32)],
        compiler_params=_params("parallel", "arbitrary"),
        name="ret_mixer",
    )(proj, proj, proj, proj, s0, inv_freq2, gn_w.reshape(1, width), gn_b.reshape(1, width))


def _trunk(x, mod, mod_f, rows_per_group, seq_rows, valid_rows, nb, pos0, to_seq, from_seq,
           conv_init, s_gdn0, s_ret0, w, tm, tf):
    heads, dk, dv = s_gdn0.shape[1:]
    emit_w = conv_init is None
    x1, h2, *w1_bf = _ffn(x, mod, (0, 1, 2), w["norm_ffn1"], w["w1_gate"], w["w1_up"], w["w1_down"],
                          mod, (3, 4), w["norm_mix"], rows_per_group=rows_per_group, tm=tm, tf=tf,
                          emit_x=True, hn_dtype=BF16, emit_w=emit_w)
    conv_ch = heads * (2 * dk + dv)
    width = heads * dv
    ba0 = conv_ch + width
    tn = PROJ_COLS
    proj_ba = to_seq(_proj(h2, w["w_in_t"], lambda j: ba0, LANES, tm, LANES))
    chunk = min(CHUNK, seq_rows)
    nb_gdn, nb_ret = nb
    gdn_args = dict(chunk=chunk, valid_rows=valid_rows, nb=nb_gdn, heads=heads, dk=dk, dv=dv)
    if conv_init is None:
        qkv, conv_new = _proj_conv(h2, w["w_in_t"], w["conv_w"], conv_ch, seq_rows, tm, tn)
        skip = lambda j: conv_ch + j * tn + jnp.where(j > 0, 2 * heads, 0)
        rest = to_seq(_proj(h2, w["w_in_t"], skip, 5 * width, tm, tn))
        o_gdn, s_gdn = _gdn((to_seq(qkv), 0), (rest, 0), proj_ba, s_gdn0, w["a_log"], w["dt_bias"],
                            w["gdn_norm_w"], **gdn_args)
        ret_src, ret_col = rest, width
    else:
        skip = lambda j: j * tn + jnp.where(j * tn >= ba0, 2 * heads, 0)
        proj = to_seq(_proj(h2, w["w_in_t"], skip, ba0 + 4 * width, tm, tn))
        o_gdn, s_gdn, conv_new = _gdn((proj, 0), (proj, conv_ch), proj_ba, s_gdn0, w["a_log"], w["dt_bias"],
                                      w["gdn_norm_w"], (conv_init, w["conv_w"]), **gdn_args)
        ret_src, ret_col = proj, ba0
    o_ret, s_ret = _ret(ret_src, ret_col, s_ret0, w["ret_gn_w"], w["ret_gn_b"],
                        chunk=chunk, valid_rows=valid_rows, nb=nb_ret, pos0=pos0, heads=heads, dk=dk, dv=dv)
    o_gdn = from_seq(o_gdn)
    o_ret = from_seq(o_ret)
    x2 = _outproj(x1, mod, 5, o_gdn, o_ret, w["w_out"], rows_per_group=rows_per_group, tm=min(tm, OUTPROJ_ROWS))
    y, *w2_bf = _ffn(x2, mod, (6, 7, 8), w["norm_ffn2"], w["w2_gate"], w["w2_up"], w["w2_down"],
                     mod_f, (0, 1), w["norm_final"], rows_per_group=rows_per_group, tm=tm, tf=tf,
                     emit_x=False, hn_dtype=F32, emit_w=emit_w)
    w_bf = dict(zip(("w1_gate", "w1_up", "w1_down", "w2_gate", "w2_up", "w2_down"), w1_bf + w2_bf))
    return y, s_gdn, conv_new[:, SUBLANES - (CONV_WIDTH - 1):, :], s_ret, w_bf


def kernel(x_prompt, x_sample, state_gdn, state_conv, state_ret, c_prompt, c_sample, w_ada, b_ada, norm_ffn1, w1_gate, w1_up, w1_down, norm_mix, w_in, conv_w, a_log, dt_bias, gdn_norm_w, ret_gn_w, ret_gn_b, w_out, norm_ffn2, w2_gate, w2_up, w2_down, w_ada_final, b_ada_final, norm_final):
    bp, tp, d = x_prompt.shape
    bs, ts, _ = x_sample.shape
    depth, _, heads, dk, dv = state_gdn.shape
    assert depth == 1, "single-layer trunk"
    assert bs == ROW_TILE, "time-major sample rows must align adaLN vectors with row tiles"
    conv_ch = state_conv.shape[-1]
    width = heads * dv
    ba0 = conv_ch + width
    ret0 = ba0 + 2 * heads

    n_c = bs + bp
    n_c_pad = -(-n_c // SUBLANES) * SUBLANES
    c_all = jnp.concatenate([c_sample, c_prompt, jnp.zeros((n_c_pad - n_c, d), F32)], axis=0)
    ada = _ada_proj(c_all, w_ada[0], b_ada[0], tn=ADA_COLS)
    ada_f = _ada_proj(c_all, w_ada_final, b_ada_final, tn=ADA_COLS)
    mod_s, mod_p = ada, ada[bs:bs + bp].reshape(bp, 1, N_ADA * d)
    modf_s, modf_p = ada_f, ada_f[bs:bs + bp].reshape(bp, 1, 2 * d)

    assert ba0 % PROJ_COLS == 0 and ret0 % SUBLANES == 0 and w_in.shape[-1] == ret0 + 4 * width
    w = dict(norm_ffn1=norm_ffn1[0], w1_gate=w1_gate[0], w1_up=w1_up[0], w1_down=w1_down[0],
             norm_mix=norm_mix[0], w_in_t=w_in[0].T, conv_w=conv_w[0], a_log=a_log[0],
             dt_bias=dt_bias[0], gdn_norm_w=gdn_norm_w[0], ret_gn_w=ret_gn_w[0], ret_gn_b=ret_gn_b[0],
             w_out=w_out[0], norm_ffn2=norm_ffn2[0], w2_gate=w2_gate[0], w2_up=w2_up[0],
             w2_down=w2_down[0], norm_final=norm_final)

    zeros_state = jnp.zeros((bp, heads, dk, dv), F32)
    y_p, gdn_p, conv_p, ret_p, w_bf = _trunk(
        x_prompt.reshape(bp * tp, d), mod_p, modf_p, tp, tp, min(CHUNK, tp), (bp, SEQS_PER_STEP_FULL), 0,
        lambda a: a.reshape(bp, tp, a.shape[-1]), lambda a: a.reshape(bp * tp, a.shape[-1]),
        None, zeros_state, zeros_state, w, tm=FFN_ROWS, tf=FFN_COLS)

    ts_pad = -(-ts // SUBLANES) * SUBLANES
    to_seq = lambda a: jnp.pad(a.reshape(ts, bs, a.shape[-1]).transpose(1, 0, 2),
                               ((0, 0), (0, ts_pad - ts), (0, 0)))
    from_seq = lambda a: a[:, :ts].transpose(1, 0, 2).reshape(ts * bs, a.shape[-1])
    conv_init = jnp.pad(state_conv[0], ((0, 0), (SUBLANES - (CONV_WIDTH - 1), 0), (0, 0)))
    y_s, gdn_s, conv_s, ret_s, _ = _trunk(
        x_sample.transpose(1, 0, 2).reshape(ts * bs, d), mod_s, modf_s, None, ts_pad, ts,
        (SEQS_PER_STEP_SHORT, SEQS_PER_STEP_SHORT), PAST_LEN,
        to_seq, from_seq, conv_init, state_gdn[0], state_ret[0], {**w, **w_bf}, tm=ts * bs, tf=FFN_COLS_SMALL)
    y_s = y_s.reshape(ts, bs, d).transpose(1, 0, 2)

    return (y_p.reshape(bp, tp, d), y_s, gdn_p[None], conv_p[None], ret_p[None],
            gdn_s[None], conv_s[None], ret_s[None])
```

```python
import functools
import math

import jax
import jax.numpy as jnp
from jax import lax
from jax.experimental import pallas as pl
from jax.experimental.pallas import tpu as pltpu

F32 = jnp.float32
BF16 = jnp.bfloat16

LANES = 128
SUBLANES = 8
MXU_COLS = 256
VMEM_LIMIT_BYTES = 60 * 1024 * 1024

PAST_LEN = 16384
CONV_WIDTH = 4
CHUNK = 64
ROPE_BASE = 10000.0
NORM_EPS = 1e-6
L2_EPS = 1e-6
MACARON_WEIGHT = 0.5
N_ADA = 9
FFN_ROWS = 1024
FFN_COLS = 256
FFN_COLS_SMALL = 512
OUTPROJ_ROWS = 512
ADA_COLS = 512
SEQS_PER_STEP_FULL = 2
SEQS_PER_STEP_SHORT = 8
ROW_TILE = 128
NORM_ROWS = 16
FFN_SUB_ROWS = 512
PROJ_COLS = 1024
HISTORY_ROWS = SUBLANES
CONV_SUB_ROWS = 256


def _sigmoid(x):
    return 1.0 / (1.0 + jnp.exp(-x))


def _silu(x):
    return x * _sigmoid(x)


def _softplus(x):
    return jnp.maximum(x, 0.0) + jnp.log(1.0 + jnp.exp(-jnp.abs(x)))


def _dot(a, b):
    return jnp.dot(a, b, preferred_element_type=F32)


def _dot_nt(a, b):
    return lax.dot_general(a, b, (((1,), (1,)), ((), ())), preferred_element_type=F32)


def _dot_tn(a, b):
    return lax.dot_general(a, b, (((0,), (0,)), ((), ())), preferred_element_type=F32)


def _split3(x):
    hi = x.astype(BF16)
    r = x - hi.astype(F32)
    mid = r.astype(BF16)
    lo = (r - mid.astype(F32)).astype(BF16)
    return hi, mid, lo


def _rms_mod(x, gain, shift, scale):
    y = x * lax.rsqrt(jnp.mean(x * x, axis=-1, keepdims=True) + NORM_EPS)
    return (y * gain) * (1.0 + scale) + shift


def _mod_rows(ref, g):
    if ref.shape[0] == 1:
        return ref[...]
    return ref[g * NORM_ROWS:(g + 1) * NORM_ROWS, :]


def _for_row_groups(n_rows, fn):
    def tile(t, carry):
        base = pl.multiple_of(t * ROW_TILE, ROW_TILE)
        for g in range(ROW_TILE // NORM_ROWS):
            fn(pl.ds(base + g * NORM_ROWS, NORM_ROWS), g)
        return carry
    lax.fori_loop(0, n_rows // ROW_TILE, tile, 0)


def _params(*sem):
    return pltpu.CompilerParams(dimension_semantics=sem, vmem_limit_bytes=VMEM_LIMIT_BYTES)


def _ada_kernel(c_ref, w_ref, b_ref, o_ref):
    a = _silu(c_ref[...]).astype(BF16)
    o_ref[...] = _dot(a, w_ref[...].astype(BF16)) + b_ref[...]


def _ada_proj(c, w, b, tn):
    m, d = c.shape
    n = w.shape[1]
    return pl.pallas_call(
        _ada_kernel,
        grid=(n // tn,),
        in_specs=[
            pl.BlockSpec((m, d), lambda j: (0, 0)),
            pl.BlockSpec((d, tn), lambda j: (0, j)),
            pl.BlockSpec((1, tn), lambda j: (0, j)),
        ],
        out_specs=pl.BlockSpec((m, tn), lambda j: (0, j)),
        out_shape=jax.ShapeDtypeStruct((m, n), F32),
        compiler_params=_params("parallel"),
        name="ada_proj",
    )(c, w, b.reshape(1, n))


def _ffn_kernel(x_ref, sh_ref, sc_ref, gt_ref, gain_ref, wg_ref, wu_ref, wd_ref,
                gain2_ref, sh2_ref, sc2_ref, *rest, emit_x):
    if emit_x:
        xo_ref, hn_hbm, h_scr, stage_scr, stage_sem = rest
        acc_ref = xo_ref
    else:
        hn_hbm, h_scr, stage_scr, stage_sem, acc_ref = rest
    j = pl.program_id(1)
    tm = x_ref.shape[0]

    @pl.when(j == 0)
    def _():
        def group(rows, g):
            h = _rms_mod(x_ref[rows, :], gain_ref[...], _mod_rows(sh_ref, g), _mod_rows(sc_ref, g))
            h_scr[rows, :] = h.astype(BF16)
            acc_ref[rows, :] = jnp.zeros((NORM_ROWS, acc_ref.shape[1]), F32)
        _for_row_groups(tm, group)

    wg = wg_ref[...].astype(BF16)
    wu = wu_ref[...].astype(BF16)
    wd = wd_ref[...].astype(BF16)
    sub = min(tm, FFN_SUB_ROWS)
    for r in range(tm // sub):
        rows = slice(r * sub, (r + 1) * sub)
        h = h_scr[rows, :]
        a = (_silu(_dot(h, wg)) * _dot(h, wu)).astype(BF16)
        acc_ref[rows, :] += _dot(a, wd)

    @pl.when(j == pl.num_programs(1) - 1)
    def _():
        i = pl.program_id(0)
        n_tiles = tm // ROW_TILE

        def hn_copy(t, block):
            row = pl.multiple_of(block * tm + t * ROW_TILE, ROW_TILE)
            return pltpu.make_async_copy(stage_scr.at[t], hn_hbm.at[pl.ds(row, ROW_TILE), :], stage_sem.at[t])

        def wait_block(block):
            for t in range(n_tiles):
                hn_copy(t, block).wait()

        @pl.when(i > 0)
        def _():
            wait_block(i - 1)

        def tile(t, carry):
            base = pl.multiple_of(t * ROW_TILE, ROW_TILE)
            for g in range(ROW_TILE // NORM_ROWS):
                rows = pl.ds(base + g * NORM_ROWS, NORM_ROWS)
                xo = x_ref[rows, :] + (MACARON_WEIGHT * _mod_rows(gt_ref, g)) * acc_ref[rows, :]
                if emit_x:
                    xo_ref[rows, :] = xo
                hn = _rms_mod(xo, gain2_ref[...], _mod_rows(sh2_ref, g), _mod_rows(sc2_ref, g))
                stage_scr[t, g * NORM_ROWS:(g + 1) * NORM_ROWS, :] = hn.astype(stage_scr.dtype)
            hn_copy(t, i).start()
            return carry
        lax.fori_loop(0, n_tiles, tile, 0)

        @pl.when(i == pl.num_programs(0) - 1)
        def _():
            wait_block(i)


def _mod_spec(mod, k, d, rows_per_group, tm):
    if mod.ndim == 3:
        blocks_per_group = rows_per_group // tm
        return pl.BlockSpec((None, 1, d), lambda i, j: (i // blocks_per_group, 0, k))
    return pl.BlockSpec((ROW_TILE, d), lambda i, j: (0, k))


def _ffn(x, mod, ks, gain, wg, wu, wd, mod2, ks2, gain2, *, rows_per_group, tm, tf, emit_x, hn_dtype):
    m, d = x.shape
    f = wg.shape[1]
    assert m % tm == 0 and f % tf == 0 and tm % ROW_TILE == 0 and d % LANES == 0
    assert mod.ndim == 2 or rows_per_group % tm == 0
    k_sh, k_sc, k_gt = ks
    k_sh2, k_sc2 = ks2
    vec = lambda: pl.BlockSpec((1, d), lambda i, j: (0, 0))
    row_block = lambda: pl.BlockSpec((tm, d), lambda i, j: (i, 0))
    out_shape = [jax.ShapeDtypeStruct((m, d), hn_dtype)]
    out_specs = [pl.BlockSpec(memory_space=pl.ANY)]
    if emit_x:
        out_shape = [jax.ShapeDtypeStruct((m, d), F32)] + out_shape
        out_specs = [row_block()] + out_specs
    return pl.pallas_call(
        functools.partial(_ffn_kernel, emit_x=emit_x),
        grid=(m // tm, f // tf),
        in_specs=[
            row_block(),
            _mod_spec(mod, k_sh, d, rows_per_group, tm),
            _mod_spec(mod, k_sc, d, rows_per_group, tm),
            _mod_spec(mod, k_gt, d, rows_per_group, tm),
            vec(),
            pl.BlockSpec((d, tf), lambda i, j: (0, j)),
            pl.BlockSpec((d, tf), lambda i, j: (0, j)),
            pl.BlockSpec((tf, d), lambda i, j: (j, 0)),
            vec(),
            _mod_spec(mod2, k_sh2, d, rows_per_group, tm),
            _mod_spec(mod2, k_sc2, d, rows_per_group, tm),
        ],
        out_specs=out_specs,
        out_shape=out_shape,
        scratch_shapes=[pltpu.VMEM((tm, d), BF16), pltpu.VMEM((tm // ROW_TILE, ROW_TILE, d), hn_dtype),
                        pltpu.SemaphoreType.DMA((tm // ROW_TILE,))]
        + ([] if emit_x else [pltpu.VMEM((tm, d), F32)]),
        compiler_params=_params("arbitrary", "arbitrary"),
        name="ffn",
    )(x, mod, mod, mod, gain.reshape(1, d), wg, wu, wd, gain2.reshape(1, d), mod2, mod2)


def _proj_kernel(h_ref, wt_ref, o_ref, wb_scr):
    @pl.when(pl.program_id(1) == 0)
    def _():
        wb_scr[...] = wt_ref[...].astype(BF16)

    o_ref[...] = _dot_nt(h_ref[...], wb_scr[...])


def _proj(h, wt, row_of, n_cols, tm, tn):
    m, d = h.shape
    return pl.pallas_call(
        _proj_kernel,
        grid=(n_cols // tn, m // tm),
        in_specs=[
            pl.BlockSpec((tm, d), lambda j, i: (i, 0)),
            pl.BlockSpec((pl.Element(tn), pl.Element(d)),
                         lambda j, i: (pl.multiple_of(row_of(j), SUBLANES), 0)),
        ],
        out_specs=pl.BlockSpec((tm, tn), lambda j, i: (i, j)),
        out_shape=jax.ShapeDtypeStruct((m, n_cols), F32),
        scratch_shapes=[pltpu.VMEM((tn, d), BF16)],
        compiler_params=_params("parallel", "arbitrary"),
        name="in_proj",
    )(h, wt)


def _proj_conv_kernel(h_ref, wt_ref, convw_ref, o_ref, cnew_ref, wb_scr, tail_scr, *, blocks_per_seq):
    i = pl.program_id(1)

    @pl.when(i == 0)
    def _():
        wb_scr[...] = wt_ref[...].astype(BF16)

    h = h_ref[...]
    tm, tn = o_ref.shape
    first = i % blocks_per_seq == 0
    for cg in range(tn // MXU_COLS):
        cols = slice(cg * MXU_COLS, (cg + 1) * MXU_COLS)
        raw = _dot_nt(h, wb_scr[cols, :])
        w = convw_ref[:, cols]
        prev = jnp.where(first, 0.0, tail_scr[:, cols])
        for r in range(tm // CONV_SUB_ROWS):
            rows = slice(r * CONV_SUB_ROWS, (r + 1) * CONV_SUB_ROWS)
            x = raw[rows, :]
            y, _ = _causal_conv_silu(prev, x, w)
            o_ref[rows, cols] = y
            prev = x[CONV_SUB_ROWS - HISTORY_ROWS:, :]
        tail_scr[:, cols] = prev
        cnew_ref[:, cols] = prev


def _proj_conv(h, wt, conv_w, n_cols, rows_per_seq, tm, tn):
    m, d = h.shape
    blocks_per_seq = rows_per_seq // tm
    return pl.pallas_call(
        functools.partial(_proj_conv_kernel, blocks_per_seq=blocks_per_seq),
        grid=(n_cols // tn, m // tm),
        in_specs=[
            pl.BlockSpec((tm, d), lambda j, i: (i, 0)),
            pl.BlockSpec((tn, d), lambda j, i: (j, 0)),
            pl.BlockSpec((CONV_WIDTH, tn), lambda j, i: (0, j)),
        ],
        out_specs=[
            pl.BlockSpec((tm, tn), lambda j, i: (i, j)),
            pl.BlockSpec((None, HISTORY_ROWS, tn), lambda j, i: (i // blocks_per_seq, 0, j)),
        ],
        out_shape=[
            jax.ShapeDtypeStruct((m, n_cols), F32),
            jax.ShapeDtypeStruct((m // rows_per_seq, HISTORY_ROWS, n_cols), F32),
        ],
        scratch_shapes=[pltpu.VMEM((tn, d), BF16), pltpu.VMEM((HISTORY_ROWS, tn), F32)],
        compiler_params=_params("parallel", "arbitrary"),
        name="in_proj_conv",
    )(h, wt, conv_w)


def _outproj_kernel(x_ref, gt_ref, og_ref, or_ref, w_ref, o_ref, wb_scr):
    @pl.when(pl.program_id(0) == 0)
    def _():
        wb_scr[...] = w_ref[...].astype(BF16)

    kw = og_ref.shape[1]
    tm = x_ref.shape[0]
    mixed = _dot(og_ref[...], wb_scr[:kw, :]) + _dot(or_ref[...], wb_scr[kw:, :])
    for r in range(tm // ROW_TILE):
        rows = slice(r * ROW_TILE, (r + 1) * ROW_TILE)
        o_ref[rows, :] = x_ref[rows, :] + gt_ref[...] * mixed[rows, :]


def _outproj(x, mod, k_gt, og, orr, w_out, *, rows_per_group, tm):
    m, d = x.shape
    kw = og.shape[1]
    if mod.ndim == 3:
        blocks_per_group = rows_per_group // tm
        gt_spec = pl.BlockSpec((None, 1, d), lambda i: (i // blocks_per_group, 0, k_gt))
    else:
        gt_spec = pl.BlockSpec((ROW_TILE, d), lambda i: (0, k_gt))
    return pl.pallas_call(
        _outproj_kernel,
        grid=(m // tm,),
        in_specs=[
            pl.BlockSpec((tm, d), lambda i: (i, 0)),
            gt_spec,
            pl.BlockSpec((tm, kw), lambda i: (i, 0)),
            pl.BlockSpec((tm, kw), lambda i: (i, 0)),
            pl.BlockSpec(w_out.shape, lambda i: (0, 0), pipeline_mode=pl.Buffered(1)),
        ],
        out_specs=pl.BlockSpec((tm, d), lambda i: (i, 0)),
        out_shape=jax.ShapeDtypeStruct((m, d), F32),
        scratch_shapes=[pltpu.VMEM(w_out.shape, BF16)],
        compiler_params=_params("arbitrary"),
        name="out_proj",
    )(x, mod, og, orr, w_out)


SOLVE_BLOCK = 16


def _nilpotent_apply(chains, index):
    power = 1
    while power < index:
        square = 2 * power < index
        for ch in chains:
            right = jnp.concatenate([ch["sol"], ch["nil"]], axis=1) if square else ch["sol"]
            ch["prod"] = _dot(ch["nil"].astype(BF16), right.astype(BF16))
        for ch in chains:
            width = ch["sol"].shape[1]
            upd = ch["prod"][:, :width]
            ch["sol"] = ch["sol"] - upd if power == 1 else ch["sol"] + upd
            if square:
                ch["nil"] = ch["prod"][:, width:]
        power *= 2


def _causal_conv_silu(prev, x, w):
    xp = jnp.concatenate([prev, x], axis=0)
    acc = x * w[CONV_WIDTH - 1:CONV_WIDTH, :]
    for sft in range(1, CONV_WIDTH):
        tap = CONV_WIDTH - 1 - sft
        acc = acc + pltpu.roll(xp, sft, 0)[HISTORY_ROWS:, :] * w[tap:tap + 1, :]
    return _silu(acc), xp


def _gdn_kernel(*refs, valid_rows, heads, dk, dv, conv_here):
    if conv_here:
        (qkv_ref, z_ref, ba_ref, s0_ref, alog_ref, dtb_ref, nw_ref, cinit_ref, convw_ref,
         o_ref, snew_ref, cnew_ref, s_scr, tail_scr) = refs
    else:
        qkv_ref, z_ref, ba_ref, s0_ref, alog_ref, dtb_ref, nw_ref, o_ref, snew_ref, s_scr = refs
    n = pl.program_id(1)
    last = pl.num_programs(1) - 1
    nb, c, _ = qkv_ref.shape

    @pl.when(n == 0)
    def _():
        s_scr[...] = s0_ref[...]
        if conv_here:
            tail_scr[...] = cinit_ref[...]

    row = lax.broadcasted_iota(jnp.int32, (c, c), 0)
    col = lax.broadcasted_iota(jnp.int32, (c, c), 1)
    causal = row >= col
    strict = row > col
    diag = row == col
    tril = jnp.where(causal, 1.0, 0.0).astype(BF16)
    nw = nw_ref[...]
    neg_a = -jnp.exp(alog_ref[...])
    dtb = dtb_ref[...]

    chains = []
    for i in range(nb):
        if conv_here:
            x = qkv_ref[i]
            qkv, xp = _causal_conv_silu(tail_scr[i], x, convw_ref[...])
            tail_scr[i] = x[c - HISTORY_ROWS:, :]

            @pl.when(n == last)
            def _(i=i, xp=xp):
                cnew_ref[i] = xp[valid_rows:valid_rows + HISTORY_ROWS, :]
        else:
            qkv = qkv_ref[i]

        ba = ba_ref[i]
        beta_all = _sigmoid(ba)
        g_all = neg_a * _softplus(ba + dtb)
        if valid_rows < c:
            live = lax.broadcasted_iota(jnp.int32, ba.shape, 0) < valid_rows
            beta_all = jnp.where(live, beta_all, 0.0)
            g_all = jnp.where(live, g_all, 0.0)
        g_hi, g_mid, g_lo = _split3(g_all)
        gc_all = _dot(tril, g_hi) + (_dot(tril, g_mid) + _dot(tril, g_lo))
        gc_last_all = gc_all[c - 1:c, :]
        eg_all = jnp.exp(gc_all)
        kdec_all = jnp.exp(gc_last_all - gc_all)
        gl_all = jnp.exp(gc_last_all)

        for h in range(heads):
            beta = beta_all[:, h:h + 1]
            gc = gc_all[:, heads + h:heads + h + 1]
            eg = eg_all[:, heads + h:heads + h + 1]
            q = qkv[:, h * dk:(h + 1) * dk]
            k = qkv[:, heads * dk + h * dk:heads * dk + (h + 1) * dk]
            v = qkv[:, 2 * heads * dk + h * dv:2 * heads * dk + (h + 1) * dv]
            q = q * (lax.rsqrt(jnp.sum(q * q, axis=-1, keepdims=True) + L2_EPS) * (dk ** -0.5))
            k = k * lax.rsqrt(jnp.sum(k * k, axis=-1, keepdims=True) + L2_EPS)
            kb = k * beta
            gc_col = jnp.broadcast_to(gc, (c, c))
            gc_row = jnp.sum(jnp.where(diag, gc_col, 0.0), axis=0, keepdims=True)
            decay = jnp.where(causal, jnp.exp(gc_col - gc_row), 0.0)
            chains.append(dict(
                i=i, h=h, decay=decay,
                kbq=jnp.concatenate([kb, q], axis=0).astype(BF16), k_bf=k.astype(BF16),
                rhs=jnp.concatenate([v * beta, kb * eg], axis=1),
                q_dec=q * eg,
                k_dec=(k * kdec_all[:, heads + h:heads + h + 1]).astype(BF16),
                gl=gl_all[:, heads + h:heads + h + 1]))

    for ch in chains:
        ch["kq"] = _dot_nt(ch["kbq"], ch["k_bf"])
    for ch in chains:
        ch["a"] = jnp.where(strict, ch["kq"][:c] * ch["decay"], 0.0)
        ch["qk"] = (ch["kq"][c:] * ch["decay"]).astype(BF16)

    if valid_rows <= SOLVE_BLOCK:
        for ch in chains:
            ch["nil"], ch["sol"] = ch["a"], ch["rhs"]
        _nilpotent_apply(chains, valid_rows)
    else:
        same_block = (row // SOLVE_BLOCK) == (col // SOLVE_BLOCK)
        eye = jnp.where(diag, 1.0, 0.0)
        for ch in chains:
            ch["nil"] = jnp.where(same_block, ch["a"], 0.0)
            ch["sol"] = eye
        _nilpotent_apply(chains, SOLVE_BLOCK)
        for ch in chains:
            off_block = jnp.where(same_block, 0.0, ch["a"])
            right = jnp.concatenate([ch["rhs"], off_block], axis=1)
            ch["prod"] = _dot(ch["sol"].astype(BF16), right.astype(BF16))
        for ch in chains:
            ch["sol"] = ch["prod"][:, :dv + dk]
            ch["nil"] = ch["prod"][:, dv + dk:]
        _nilpotent_apply(chains, c // SOLVE_BLOCK)

    for ch in chains:
        ch["s"] = s_scr[ch["i"], ch["h"]]
        lhs = jnp.concatenate([ch["sol"][:, dv:], ch["q_dec"]], axis=0).astype(BF16)
        ch["ws_qs"] = _dot(lhs, ch["s"].astype(BF16))
    for ch in chains:
        ch["v_new"] = (ch["sol"][:, :dv] - ch["ws_qs"][:c]).astype(BF16)
    for ch in chains:
        s_scr[ch["i"], ch["h"]] = ch["s"] * ch["gl"] + _dot_tn(ch["k_dec"], ch["v_new"])
        ch["o"] = ch["ws_qs"][c:] + _dot(ch["qk"], ch["v_new"])
    for ch in chains:
        ch["ms"] = jnp.mean(ch["o"] * ch["o"], axis=-1, keepdims=True)
    for ch in chains:
        i, h, o = ch["i"], ch["h"], ch["o"]
        o = o * lax.rsqrt(ch["ms"] + NORM_EPS) * nw
        o = o * _silu(z_ref[i, :, h * dv:(h + 1) * dv])
        o_ref[i, :, h * dv:(h + 1) * dv] = o.astype(o_ref.dtype)

    @pl.when(n == last)
    def _():
        snew_ref[...] = s_scr[...]


def _gdn(qkv_src, z_src, proj_ba, s0, a_log, dt_bias, norm_w, conv=None, *, chunk, valid_rows, nb, heads, dk, dv):
    (qkv_arr, qkv_col), (z_arr, z_col) = qkv_src, z_src
    b, t, _ = qkv_arr.shape
    assert valid_rows <= SOLVE_BLOCK or (valid_rows == chunk and chunk % SOLVE_BLOCK == 0)
    conv_ch = heads * (2 * dk + dv)
    width = heads * dv
    n_chunks = t // chunk
    qkv_blk = qkv_col // conv_ch
    z_blk = z_col // width
    lane_vec = lambda x: jnp.zeros((1, LANES), F32).at[0, heads:2 * heads].set(x.astype(F32))
    state_spec = pl.BlockSpec((nb, heads, dk, dv), lambda i, n: (i, 0, 0, 0))
    hist_spec = pl.BlockSpec((nb, HISTORY_ROWS, conv_ch), lambda i, n: (i, 0, 0))
    lane_spec = pl.BlockSpec((1, LANES), lambda i, n: (0, 0))
    in_specs = [
        pl.BlockSpec((nb, chunk, conv_ch), lambda i, n: (i, n, qkv_blk)),
        pl.BlockSpec((nb, chunk, width), lambda i, n: (i, n, z_blk)),
        pl.BlockSpec((nb, chunk, LANES), lambda i, n: (i, n, 0)),
        state_spec, lane_spec, lane_spec,
        pl.BlockSpec((1, dv), lambda i, n: (0, 0)),
    ]
    args = [qkv_arr, z_arr, proj_ba, s0, lane_vec(a_log), lane_vec(dt_bias), norm_w.reshape(1, dv)]
    out_specs = [pl.BlockSpec((nb, chunk, width), lambda i, n: (i, n, 0)), state_spec]
    out_shape = [jax.ShapeDtypeStruct((b, t, width), BF16), jax.ShapeDtypeStruct((b, heads, dk, dv), F32)]
    scratch = [pltpu.VMEM((nb, heads, dk, dv), F32)]
    if conv is not None:
        conv_init, conv_w = conv
        in_specs += [hist_spec, pl.BlockSpec((CONV_WIDTH, conv_ch), lambda i, n: (0, 0))]
        args += [conv_init, conv_w]
        out_specs.append(hist_spec)
        out_shape.append(jax.ShapeDtypeStruct((b, HISTORY_ROWS, conv_ch), F32))
        scratch.append(pltpu.VMEM((nb, HISTORY_ROWS, conv_ch), F32))
    return pl.pallas_call(
        functools.partial(_gdn_kernel, valid_rows=valid_rows, heads=heads, dk=dk, dv=dv,
                          conv_here=conv is not None),
        grid=(b // nb, n_chunks),
        in_specs=in_specs,
        out_specs=out_specs,
        out_shape=out_shape,
        scratch_shapes=scratch,
        compiler_params=_params("parallel", "arbitrary"),
        name="gdn_mixer",
    )(*args)


def _ret_kernel(q_ref, k_ref, v_ref, g_ref, s0_ref, invf_ref, gnw_ref, gnb_ref,
                o_ref, snew_ref, s_scr, *, valid_rows, pos0, heads, dk, dv):
    n = pl.program_id(1)
    nb, c, _ = q_ref.shape
    half = dk // 2

    @pl.when(n == 0)
    def _():
        s_scr[...] = s0_ref[...]

    t_idx = lax.broadcasted_iota(jnp.int32, (c, dk), 0)
    pos = (pos0 + n * c + t_idx).astype(F32)
    ang = pos * invf_ref[...]
    cos2 = jnp.cos(ang)
    lane = lax.broadcasted_iota(jnp.int32, (c, dk), 1)
    sin2 = jnp.where(lane < half, -jnp.sin(ang), jnp.sin(ang))

    row = lax.broadcasted_iota(jnp.int32, (c, c), 0)
    col = lax.broadcasted_iota(jnp.int32, (c, c), 1)
    rel = (row - col).astype(F32)
    idx = lax.broadcasted_iota(jnp.int32, (c, 1), 0).astype(F32)
    live = lax.broadcasted_iota(jnp.int32, (c, 1), 0) < valid_rows

    chains = []
    for h in range(heads):
        log_gamma = math.log(1.0 - 2.0 ** (-5.0 - h))
        dmat = jnp.where(rel >= 0, jnp.exp(jnp.maximum(rel, 0.0) * log_gamma), 0.0)
        q_scale = jnp.exp((idx + 1.0) * log_gamma)
        k_scale = jnp.exp((valid_rows - 1.0 - idx) * log_gamma)
        for i in range(nb):
            q = q_ref[i, :, h * dk:(h + 1) * dk]
            k = k_ref[i, :, h * dk:(h + 1) * dk]
            v = v_ref[i, :, h * dv:(h + 1) * dv]
            if valid_rows < c:
                v = jnp.where(live, v, 0.0)
            q = q * cos2 + pltpu.roll(q, half, 1) * sin2
            k = (k * cos2 + pltpu.roll(k, half, 1) * sin2) * (dk ** -0.5)
            chains.append(dict(
                i=i, h=h, dmat=dmat, s_decay=math.exp(valid_rows * log_gamma),
                q_bf=q.astype(BF16), k_bf=k.astype(BF16), v_bf=v.astype(BF16),
                q_dec=(q * q_scale).astype(BF16), k_dec=(k * k_scale).astype(BF16)))

    for ch in chains:
        ch["inner"] = (_dot_nt(ch["q_bf"], ch["k_bf"]) * ch["dmat"]).astype(BF16)
    for ch in chains:
        s = s_scr[ch["i"], ch["h"]]
        ch["o"] = _dot(ch["q_dec"], s.astype(BF16)) + _dot(ch["inner"], ch["v_bf"])
        s_scr[ch["i"], ch["h"]] = s * ch["s_decay"] + _dot_tn(ch["k_dec"], ch["v_bf"])
    for ch in chains:
        ch["mu"] = jnp.mean(ch["o"], axis=-1, keepdims=True)
    for ch in chains:
        ch["oc"] = ch["o"] - ch["mu"]
        ch["var"] = jnp.mean(ch["oc"] * ch["oc"], axis=-1, keepdims=True)
    for ch in chains:
        i, h = ch["i"], ch["h"]
        o = ch["oc"] * lax.rsqrt(ch["var"] + NORM_EPS)
        o = o * gnw_ref[:, h * dv:(h + 1) * dv] + gnb_ref[:, h * dv:(h + 1) * dv]
        o = o * _silu(g_ref[i, :, h * dv:(h + 1) * dv])
        o_ref[i, :, h * dv:(h + 1) * dv] = o.astype(o_ref.dtype)

    @pl.when(n == pl.num_programs(1) - 1)
    def _():
        snew_ref[...] = s_scr[...]


def _ret(proj, col0, s0, gn_w, gn_b, *, chunk, valid_rows, nb, pos0, heads, dk, dv):
    b, t, _ = proj.shape
    width = heads * dv
    n_chunks = t // chunk
    blk0 = col0 // width
    half = dk // 2
    inv_freq = ROPE_BASE ** (-jnp.arange(half, dtype=F32) / half)
    inv_freq2 = jnp.concatenate([inv_freq, inv_freq]).reshape(1, dk)
    col_spec = lambda kk: pl.BlockSpec((nb, chunk, width), lambda i, n: (i, n, blk0 + kk))
    return pl.pallas_call(
        functools.partial(_ret_kernel, valid_rows=valid_rows, pos0=pos0, heads=heads, dk=dk, dv=dv),
        grid=(b // nb, n_chunks),
        in_specs=[
            col_spec(0), col_spec(1), col_spec(2), col_spec(3),
            pl.BlockSpec((nb, heads, dk, dv), lambda i, n: (i, 0, 0, 0)),
            pl.BlockSpec((1, dk), lambda i, n: (0, 0)),
            pl.BlockSpec((1, width), lambda i, n: (0, 0)),
            pl.BlockSpec((1, width), lambda i, n: (0, 0)),
        ],
        out_specs=[
            pl.BlockSpec((nb, chunk, width), lambda i, n: (i, n, 0)),
            pl.BlockSpec((nb, heads, dk, dv), lambda i, n: (i, 0, 0, 0)),
        ],
        out_shape=[
            jax.ShapeDtypeStruct((b, t, width), BF16),
            jax.ShapeDtypeStruct((b, heads, dk, dv), F32),
        ],
        scratch_shapes=[pltpu.VMEM((nb, heads, dk, dv), F32)],
        compiler_params=_params("parallel", "arbitrary"),
        name="ret_mixer",
    )(proj, proj, proj, proj, s0, inv_freq2, gn_w.reshape(1, width), gn_b.reshape(1, width))


def _trunk(x, mod, mod_f, rows_per_group, seq_rows, valid_rows, nb, pos0, to_seq, from_seq,
           conv_init, s_gdn0, s_ret0, w, tm, tf):
    heads, dk, dv = s_gdn0.shape[1:]
    x1, h2 = _ffn(x, mod, (0, 1, 2), w["norm_ffn1"], w["w1_gate"], w["w1_up"], w["w1_down"],
                  mod, (3, 4), w["norm_mix"], rows_per_group=rows_per_group, tm=tm, tf=tf,
                  emit_x=True, hn_dtype=BF16)
    conv_ch = heads * (2 * dk + dv)
    width = heads * dv
    ba0 = conv_ch + width
    tn = PROJ_COLS
    proj_ba = to_seq(_proj(h2, w["w_in_t"], lambda j: ba0, LANES, tm, LANES))
    chunk = min(CHUNK, seq_rows)
    nb_gdn, nb_ret = nb
    gdn_args = dict(chunk=chunk, valid_rows=valid_rows, nb=nb_gdn, heads=heads, dk=dk, dv=dv)
    if conv_init is None:
        qkv, conv_new = _proj_conv(h2, w["w_in_t"], w["conv_w"], conv_ch, seq_rows, tm, tn)
        skip = lambda j: conv_ch + j * tn + jnp.where(j > 0, 2 * heads, 0)
        rest = to_seq(_proj(h2, w["w_in_t"], skip, 5 * width, tm, tn))
        o_gdn, s_gdn = _gdn((to_seq(qkv), 0), (rest, 0), proj_ba, s_gdn0, w["a_log"], w["dt_bias"],
                            w["gdn_norm_w"], **gdn_args)
        ret_src, ret_col = rest, width
    else:
        skip = lambda j: j * tn + jnp.where(j * tn >= ba0, 2 * heads, 0)
        proj = to_seq(_proj(h2, w["w_in_t"], skip, ba0 + 4 * width, tm, tn))
        o_gdn, s_gdn, conv_new = _gdn((proj, 0), (proj, conv_ch), proj_ba, s_gdn0, w["a_log"], w["dt_bias"],
                                      w["gdn_norm_w"], (conv_init, w["conv_w"]), **gdn_args)
        ret_src, ret_col = proj, ba0
    o_ret, s_ret = _ret(ret_src, ret_col, s_ret0, w["ret_gn_w"], w["ret_gn_b"],
                        chunk=chunk, valid_rows=valid_rows, nb=nb_ret, pos0=pos0, heads=heads, dk=dk, dv=dv)
    o_gdn = from_seq(o_gdn)
    o_ret = from_seq(o_ret)
    x2 = _outproj(x1, mod, 5, o_gdn, o_ret, w["w_out"], rows_per_group=rows_per_group, tm=min(tm, OUTPROJ_ROWS))
    (y,) = _ffn(x2, mod, (6, 7, 8), w["norm_ffn2"], w["w2_gate"], w["w2_up"], w["w2_down"],
                mod_f, (0, 1), w["norm_final"], rows_per_group=rows_per_group, tm=tm, tf=tf,
                emit_x=False, hn_dtype=F32)
    return y, s_gdn, conv_new[:, SUBLANES - (CONV_WIDTH - 1):, :], s_ret


def kernel(x_prompt, x_sample, state_gdn, state_conv, state_ret, c_prompt, c_sample, w_ada, b_ada, norm_ffn1, w1_gate, w1_up, w1_down, norm_mix, w_in, conv_w, a_log, dt_bias, gdn_norm_w, ret_gn_w, ret_gn_b, w_out, norm_ffn2, w2_gate, w2_up, w2_down, w_ada_final, b_ada_final, norm_final):
    bp, tp, d = x_prompt.shape
    bs, ts, _ = x_sample.shape
    depth, _, heads, dk, dv = state_gdn.shape
    assert depth == 1, "single-layer trunk"
    assert bs == ROW_TILE, "time-major sample rows must align adaLN vectors with row tiles"
    conv_ch = state_conv.shape[-1]
    width = heads * dv
    ba0 = conv_ch + width
    ret0 = ba0 + 2 * heads

    n_c = bs + bp
    n_c_pad = -(-n_c // SUBLANES) * SUBLANES
    c_all = jnp.concatenate([c_sample, c_prompt, jnp.zeros((n_c_pad - n_c, d), F32)], axis=0)
    ada = _ada_proj(c_all, w_ada[0], b_ada[0], tn=ADA_COLS)
    ada_f = _ada_proj(c_all, w_ada_final, b_ada_final, tn=ADA_COLS)
    mod_s, mod_p = ada, ada[bs:bs + bp].reshape(bp, 1, N_ADA * d)
    modf_s, modf_p = ada_f, ada_f[bs:bs + bp].reshape(bp, 1, 2 * d)

    assert ba0 % PROJ_COLS == 0 and ret0 % SUBLANES == 0 and w_in.shape[-1] == ret0 + 4 * width
    w = dict(norm_ffn1=norm_ffn1[0], w1_gate=w1_gate[0], w1_up=w1_up[0], w1_down=w1_down[0],
             norm_mix=norm_mix[0], w_in_t=w_in[0].T, conv_w=conv_w[0], a_log=a_log[0],
             dt_bias=dt_bias[0], gdn_norm_w=gdn_norm_w[0], ret_gn_w=ret_gn_w[0], ret_gn_b=ret_gn_b[0],
             w_out=w_out[0], norm_ffn2=norm_ffn2[0], w2_gate=w2_gate[0], w2_up=w2_up[0],
             w2_down=w2_down[0], norm_final=norm_final)

    zeros_state = jnp.zeros((bp, heads, dk, dv), F32)
    y_p, gdn_p, conv_p, ret_p = _trunk(
        x_prompt.reshape(bp * tp, d), mod_p, modf_p, tp, tp, min(CHUNK, tp), (bp, SEQS_PER_STEP_FULL), 0,
        lambda a: a.reshape(bp, tp, a.shape[-1]), lambda a: a.reshape(bp * tp, a.shape[-1]),
        None, zeros_state, zeros_state, w, tm=FFN_ROWS, tf=FFN_COLS)

    ts_pad = -(-ts // SUBLANES) * SUBLANES
    to_seq = lambda a: jnp.pad(a.reshape(ts, bs, a.shape[-1]).transpose(1, 0, 2),
                               ((0, 0), (0, ts_pad - ts), (0, 0)))
    from_seq = lambda a: a[:, :ts].transpose(1, 0, 2).reshape(ts * bs, a.shape[-1])
    conv_init = jnp.pad(state_conv[0], ((0, 0), (SUBLANES - (CONV_WIDTH - 1), 0), (0, 0)))
    y_s, gdn_s, conv_s, ret_s = _trunk(
        x_sample.transpose(1, 0, 2).reshape(ts * bs, d), mod_s, modf_s, None, ts_pad, ts,
        (SEQS_PER_STEP_SHORT, SEQS_PER_STEP_SHORT), PAST_LEN,
        to_seq, from_seq, conv_init, state_gdn[0], state_ret[0], w, tm=ts * bs, tf=FFN_COLS_SMALL)
    y_s = y_s.reshape(ts, bs, d).transpose(1, 0, 2)

    return (y_p.reshape(bp, tp, d), y_s, gdn_p[None], conv_p[None], ret_p[None],
            gdn_s[None], conv_s[None], ret_s[None])
```

```python
import functools
import math

import jax
import jax.numpy as jnp
from jax import lax
from jax.experimental import pallas as pl
from jax.experimental.pallas import tpu as pltpu

F32 = jnp.float32
BF16 = jnp.bfloat16

LANES = 128
SUBLANES = 8
MXU_COLS = 256
VMEM_LIMIT_BYTES = 60 * 1024 * 1024

PAST_LEN = 16384
CONV_WIDTH = 4
CHUNK = 64
ROPE_BASE = 10000.0
NORM_EPS = 1e-6
L2_EPS = 1e-6
MACARON_WEIGHT = 0.5
N_ADA = 9
FFN_ROWS = 1024
FFN_COLS = 256
FFN_COLS_SMALL = 512
OUTPROJ_ROWS = 512
ADA_COLS = 512
SEQS_PER_STEP_FULL = 2
SEQS_PER_STEP_SHORT = 8
ROW_TILE = 128
NORM_ROWS = 16
FFN_SUB_ROWS = 512
PROJ_COLS = 1024
HISTORY_ROWS = SUBLANES
CONV_SUB_ROWS = 256


def _sigmoid(x):
    return 1.0 / (1.0 + jnp.exp(-x))


def _silu(x):
    return x * _sigmoid(x)


def _softplus(x):
    return jnp.maximum(x, 0.0) + jnp.log(1.0 + jnp.exp(-jnp.abs(x)))


def _dot(a, b):
    return jnp.dot(a, b, preferred_element_type=F32)


def _dot_nt(a, b):
    return lax.dot_general(a, b, (((1,), (1,)), ((), ())), preferred_element_type=F32)


def _dot_tn(a, b):
    return lax.dot_general(a, b, (((0,), (0,)), ((), ())), preferred_element_type=F32)


def _split3(x):
    hi = x.astype(BF16)
    r = x - hi.astype(F32)
    mid = r.astype(BF16)
    lo = (r - mid.astype(F32)).astype(BF16)
    return hi, mid, lo


def _rms_mod(x, gain, shift, scale):
    y = x * lax.rsqrt(jnp.mean(x * x, axis=-1, keepdims=True) + NORM_EPS)
    return (y * gain) * (1.0 + scale) + shift


def _mod_rows(ref, g):
    if ref.shape[0] == 1:
        return ref[...]
    return ref[g * NORM_ROWS:(g + 1) * NORM_ROWS, :]


def _for_row_groups(n_rows, fn):
    def tile(t, carry):
        base = pl.multiple_of(t * ROW_TILE, ROW_TILE)
        for g in range(ROW_TILE // NORM_ROWS):
            fn(pl.ds(base + g * NORM_ROWS, NORM_ROWS), g)
        return carry
    lax.fori_loop(0, n_rows // ROW_TILE, tile, 0)


def _params(*sem):
    return pltpu.CompilerParams(dimension_semantics=sem, vmem_limit_bytes=VMEM_LIMIT_BYTES)


def _ada_kernel(c_ref, w_ref, b_ref, o_ref):
    a = _silu(c_ref[...]).astype(BF16)
    o_ref[...] = _dot(a, w_ref[...].astype(BF16)) + b_ref[...]


def _ada_proj(c, w, b, tn):
    m, d = c.shape
    n = w.shape[1]
    return pl.pallas_call(
        _ada_kernel,
        grid=(n // tn,),
        in_specs=[
            pl.BlockSpec((m, d), lambda j: (0, 0)),
            pl.BlockSpec((d, tn), lambda j: (0, j)),
            pl.BlockSpec((1, tn), lambda j: (0, j)),
        ],
        out_specs=pl.BlockSpec((m, tn), lambda j: (0, j)),
        out_shape=jax.ShapeDtypeStruct((m, n), F32),
        compiler_params=_params("parallel"),
        name="ada_proj",
    )(c, w, b.reshape(1, n))


def _ffn_kernel(x_ref, sh_ref, sc_ref, gt_ref, gain_ref, wg_ref, wu_ref, wd_ref,
                gain2_ref, sh2_ref, sc2_ref, *rest, emit_x):
    if emit_x:
        xo_ref, hn_hbm, h_scr, stage_scr, stage_sem = rest
        acc_ref = xo_ref
    else:
        hn_hbm, h_scr, stage_scr, stage_sem, acc_ref = rest
    j = pl.program_id(1)
    tm = x_ref.shape[0]

    @pl.when(j == 0)
    def _():
        def group(rows, g):
            h = _rms_mod(x_ref[rows, :], gain_ref[...], _mod_rows(sh_ref, g), _mod_rows(sc_ref, g))
            h_scr[rows, :] = h.astype(BF16)
            acc_ref[rows, :] = jnp.zeros((NORM_ROWS, acc_ref.shape[1]), F32)
        _for_row_groups(tm, group)

    wg = wg_ref[...].astype(BF16)
    wu = wu_ref[...].astype(BF16)
    wd = wd_ref[...].astype(BF16)
    sub = min(tm, FFN_SUB_ROWS)
    for r in range(tm // sub):
        rows = slice(r * sub, (r + 1) * sub)
        h = h_scr[rows, :]
        a = (_silu(_dot(h, wg)) * _dot(h, wu)).astype(BF16)
        acc_ref[rows, :] += _dot(a, wd)

    @pl.when(j == pl.num_programs(1) - 1)
    def _():
        i = pl.program_id(0)
        n_tiles = tm // ROW_TILE

        def hn_copy(t, block):
            row = pl.multiple_of(block * tm + t * ROW_TILE, ROW_TILE)
            return pltpu.make_async_copy(stage_scr.at[t], hn_hbm.at[pl.ds(row, ROW_TILE), :], stage_sem.at[t])

        def wait_block(block):
            for t in range(n_tiles):
                hn_copy(t, block).wait()

        @pl.when(i > 0)
        def _():
            wait_block(i - 1)

        def tile(t, carry):
            base = pl.multiple_of(t * ROW_TILE, ROW_TILE)
            for g in range(ROW_TILE // NORM_ROWS):
                rows = pl.ds(base + g * NORM_ROWS, NORM_ROWS)
                xo = x_ref[rows, :] + (MACARON_WEIGHT * _mod_rows(gt_ref, g)) * acc_ref[rows, :]
                if emit_x:
                    xo_ref[rows, :] = xo
                hn = _rms_mod(xo, gain2_ref[...], _mod_rows(sh2_ref, g), _mod_rows(sc2_ref, g))
                stage_scr[t, g * NORM_ROWS:(g + 1) * NORM_ROWS, :] = hn.astype(stage_scr.dtype)
            hn_copy(t, i).start()
            return carry
        lax.fori_loop(0, n_tiles, tile, 0)

        @pl.when(i == pl.num_programs(0) - 1)
        def _():
            wait_block(i)


def _mod_spec(mod, k, d, rows_per_group, tm):
    if mod.ndim == 3:
        blocks_per_group = rows_per_group // tm
        return pl.BlockSpec((None, 1, d), lambda i, j: (i // blocks_per_group, 0, k))
    return pl.BlockSpec((ROW_TILE, d), lambda i, j: (0, k))


def _ffn(x, mod, ks, gain, wg, wu, wd, mod2, ks2, gain2, *, rows_per_group, tm, tf, emit_x, hn_dtype):
    m, d = x.shape
    f = wg.shape[1]
    assert m % tm == 0 and f % tf == 0 and tm % ROW_TILE == 0 and d % LANES == 0
    assert mod.ndim == 2 or rows_per_group % tm == 0
    k_sh, k_sc, k_gt = ks
    k_sh2, k_sc2 = ks2
    vec = lambda: pl.BlockSpec((1, d), lambda i, j: (0, 0))
    row_block = lambda: pl.BlockSpec((tm, d), lambda i, j: (i, 0))
    out_shape = [jax.ShapeDtypeStruct((m, d), hn_dtype)]
    out_specs = [pl.BlockSpec(memory_space=pl.ANY)]
    if emit_x:
        out_shape = [jax.ShapeDtypeStruct((m, d), F32)] + out_shape
        out_specs = [row_block()] + out_specs
    return pl.pallas_call(
        functools.partial(_ffn_kernel, emit_x=emit_x),
        grid=(m // tm, f // tf),
        in_specs=[
            row_block(),
            _mod_spec(mod, k_sh, d, rows_per_group, tm),
            _mod_spec(mod, k_sc, d, rows_per_group, tm),
            _mod_spec(mod, k_gt, d, rows_per_group, tm),
            vec(),
            pl.BlockSpec((d, tf), lambda i, j: (0, j)),
            pl.BlockSpec((d, tf), lambda i, j: (0, j)),
            pl.BlockSpec((tf, d), lambda i, j: (j, 0)),
            vec(),
            _mod_spec(mod2, k_sh2, d, rows_per_group, tm),
            _mod_spec(mod2, k_sc2, d, rows_per_group, tm),
        ],
        out_specs=out_specs,
        out_shape=out_shape,
        scratch_shapes=[pltpu.VMEM((tm, d), BF16), pltpu.VMEM((tm // ROW_TILE, ROW_TILE, d), hn_dtype),
                        pltpu.SemaphoreType.DMA((tm // ROW_TILE,))]
        + ([] if emit_x else [pltpu.VMEM((tm, d), F32)]),
        compiler_params=_params("arbitrary", "arbitrary"),
        name="ffn",
    )(x, mod, mod, mod, gain.reshape(1, d), wg, wu, wd, gain2.reshape(1, d), mod2, mod2)


def _proj_kernel(h_ref, wt_ref, o_ref, wb_scr):
    @pl.when(pl.program_id(1) == 0)
    def _():
        wb_scr[...] = wt_ref[...].astype(BF16)

    o_ref[...] = _dot_nt(h_ref[...], wb_scr[...])


def _proj(h, wt, row_of, n_cols, tm, tn):
    m, d = h.shape
    return pl.pallas_call(
        _proj_kernel,
        grid=(n_cols // tn, m // tm),
        in_specs=[
            pl.BlockSpec((tm, d), lambda j, i: (i, 0)),
            pl.BlockSpec((pl.Element(tn), pl.Element(d)),
                         lambda j, i: (pl.multiple_of(row_of(j), SUBLANES), 0)),
        ],
        out_specs=pl.BlockSpec((tm, tn), lambda j, i: (i, j)),
        out_shape=jax.ShapeDtypeStruct((m, n_cols), F32),
        scratch_shapes=[pltpu.VMEM((tn, d), BF16)],
        compiler_params=_params("parallel", "arbitrary"),
        name="in_proj",
    )(h, wt)


def _proj_conv_kernel(h_ref, wt_ref, convw_ref, o_ref, cnew_ref, wb_scr, tail_scr, *, blocks_per_seq):
    i = pl.program_id(1)

    @pl.when(i == 0)
    def _():
        wb_scr[...] = wt_ref[...].astype(BF16)

    h = h_ref[...]
    tm, tn = o_ref.shape
    first = i % blocks_per_seq == 0
    for cg in range(tn // MXU_COLS):
        cols = slice(cg * MXU_COLS, (cg + 1) * MXU_COLS)
        raw = _dot_nt(h, wb_scr[cols, :])
        w = convw_ref[:, cols]
        prev = jnp.where(first, 0.0, tail_scr[:, cols])
        for r in range(tm // CONV_SUB_ROWS):
            rows = slice(r * CONV_SUB_ROWS, (r + 1) * CONV_SUB_ROWS)
            x = raw[rows, :]
            y, _ = _causal_conv_silu(prev, x, w)
            o_ref[rows, cols] = y
            prev = x[CONV_SUB_ROWS - HISTORY_ROWS:, :]
        tail_scr[:, cols] = prev
        cnew_ref[:, cols] = prev


def _proj_conv(h, wt, conv_w, n_cols, rows_per_seq, tm, tn):
    m, d = h.shape
    blocks_per_seq = rows_per_seq // tm
    return pl.pallas_call(
        functools.partial(_proj_conv_kernel, blocks_per_seq=blocks_per_seq),
        grid=(n_cols // tn, m // tm),
        in_specs=[
            pl.BlockSpec((tm, d), lambda j, i: (i, 0)),
            pl.BlockSpec((tn, d), lambda j, i: (j, 0)),
            pl.BlockSpec((CONV_WIDTH, tn), lambda j, i: (0, j)),
        ],
        out_specs=[
            pl.BlockSpec((tm, tn), lambda j, i: (i, j)),
            pl.BlockSpec((None, HISTORY_ROWS, tn), lambda j, i: (i // blocks_per_seq, 0, j)),
        ],
        out_shape=[
            jax.ShapeDtypeStruct((m, n_cols), F32),
            jax.ShapeDtypeStruct((m // rows_per_seq, HISTORY_ROWS, n_cols), F32),
        ],
        scratch_shapes=[pltpu.VMEM((tn, d), BF16), pltpu.VMEM((HISTORY_ROWS, tn), F32)],
        compiler_params=_params("parallel", "arbitrary"),
        name="in_proj_conv",
    )(h, wt, conv_w)


def _outproj_kernel(x_ref, gt_ref, og_ref, or_ref, w_ref, o_ref, wb_scr):
    @pl.when(pl.program_id(0) == 0)
    def _():
        wb_scr[...] = w_ref[...].astype(BF16)

    kw = og_ref.shape[1]
    tm = x_ref.shape[0]
    mixed = _dot(og_ref[...], wb_scr[:kw, :]) + _dot(or_ref[...], wb_scr[kw:, :])
    for r in range(tm // ROW_TILE):
        rows = slice(r * ROW_TILE, (r + 1) * ROW_TILE)
        o_ref[rows, :] = x_ref[rows, :] + gt_ref[...] * mixed[rows, :]


def _outproj(x, mod, k_gt, og, orr, w_out, *, rows_per_group, tm):
    m, d = x.shape
    kw = og.shape[1]
    if mod.ndim == 3:
        blocks_per_group = rows_per_group // tm
        gt_spec = pl.BlockSpec((None, 1, d), lambda i: (i // blocks_per_group, 0, k_gt))
    else:
        gt_spec = pl.BlockSpec((ROW_TILE, d), lambda i: (0, k_gt))
    return pl.pallas_call(
        _outproj_kernel,
        grid=(m // tm,),
        in_specs=[
            pl.BlockSpec((tm, d), lambda i: (i, 0)),
            gt_spec,
            pl.BlockSpec((tm, kw), lambda i: (i, 0)),
            pl.BlockSpec((tm, kw), lambda i: (i, 0)),
            pl.BlockSpec(w_out.shape, lambda i: (0, 0), pipeline_mode=pl.Buffered(1)),
        ],
        out_specs=pl.BlockSpec((tm, d), lambda i: (i, 0)),
        out_shape=jax.ShapeDtypeStruct((m, d), F32),
        scratch_shapes=[pltpu.VMEM(w_out.shape, BF16)],
        compiler_params=_params("arbitrary"),
        name="out_proj",
    )(x, mod, og, orr, w_out)


SOLVE_BLOCK = 16


def _nilpotent_apply(chains, index):
    power = 1
    while power < index:
        square = 2 * power < index
        for ch in chains:
            right = jnp.concatenate([ch["sol"], ch["nil"]], axis=1) if square else ch["sol"]
            ch["prod"] = _dot(ch["nil"].astype(BF16), right.astype(BF16))
        for ch in chains:
            width = ch["sol"].shape[1]
            upd = ch["prod"][:, :width]
            ch["sol"] = ch["sol"] - upd if power == 1 else ch["sol"] + upd
            if square:
                ch["nil"] = ch["prod"][:, width:]
        power *= 2


def _causal_conv_silu(prev, x, w):
    xp = jnp.concatenate([prev, x], axis=0)
    acc = x * w[CONV_WIDTH - 1:CONV_WIDTH, :]
    for sft in range(1, CONV_WIDTH):
        tap = CONV_WIDTH - 1 - sft
        acc = acc + pltpu.roll(xp, sft, 0)[HISTORY_ROWS:, :] * w[tap:tap + 1, :]
    return _silu(acc), xp


def _gdn_kernel(*refs, valid_rows, heads, dk, dv, conv_here):
    if conv_here:
        (qkv_ref, z_ref, ba_ref, s0_ref, alog_ref, dtb_ref, nw_ref, cinit_ref, convw_ref,
         o_ref, snew_ref, cnew_ref, s_scr, tail_scr) = refs
    else:
        qkv_ref, z_ref, ba_ref, s0_ref, alog_ref, dtb_ref, nw_ref, o_ref, snew_ref, s_scr = refs
    n = pl.program_id(1)
    last = pl.num_programs(1) - 1
    nb, c, _ = qkv_ref.shape

    @pl.when(n == 0)
    def _():
        s_scr[...] = s0_ref[...]
        if conv_here:
            tail_scr[...] = cinit_ref[...]

    row = lax.broadcasted_iota(jnp.int32, (c, c), 0)
    col = lax.broadcasted_iota(jnp.int32, (c, c), 1)
    causal = row >= col
    strict = row > col
    diag = row == col
    tril = jnp.where(causal, 1.0, 0.0).astype(BF16)
    nw = nw_ref[...]
    neg_a = -jnp.exp(alog_ref[...])
    dtb = dtb_ref[...]

    gates = []
    for i in range(nb):
        ba = ba_ref[i]
        beta_all = _sigmoid(ba)
        g_all = neg_a * _softplus(ba + dtb)
        if valid_rows < c:
            live = lax.broadcasted_iota(jnp.int32, ba.shape, 0) < valid_rows
            beta_all = jnp.where(live, beta_all, 0.0)
            g_all = jnp.where(live, g_all, 0.0)
        g_hi, g_mid, g_lo = _split3(g_all)
        gc_all = _dot(tril, g_hi) + (_dot(tril, g_mid) + _dot(tril, g_lo))
        gc_last_all = gc_all[c - 1:c, :]
        gates.append(dict(beta=beta_all, gc=gc_all, eg=jnp.exp(gc_all),
                          kdec=jnp.exp(gc_last_all - gc_all), gl=jnp.exp(gc_last_all)))

    chains = []
    for i in range(nb):
        if conv_here:
            x = qkv_ref[i]
            qkv, xp = _causal_conv_silu(tail_scr[i], x, convw_ref[...])
            tail_scr[i] = x[c - HISTORY_ROWS:, :]

            @pl.when(n == last)
            def _(i=i, xp=xp):
                cnew_ref[i] = xp[valid_rows:valid_rows + HISTORY_ROWS, :]
        else:
            qkv = qkv_ref[i]
        for h in range(heads):
            q = qkv[:, h * dk:(h + 1) * dk]
            k = qkv[:, heads * dk + h * dk:heads * dk + (h + 1) * dk]
            v = qkv[:, 2 * heads * dk + h * dv:2 * heads * dk + (h + 1) * dv]
            q = q * (lax.rsqrt(jnp.sum(q * q, axis=-1, keepdims=True) + L2_EPS) * (dk ** -0.5))
            k = k * lax.rsqrt(jnp.sum(k * k, axis=-1, keepdims=True) + L2_EPS)
            chains.append(dict(i=i, h=h, q=q, k=k, v=v, k_bf=k.astype(BF16)))

    for ch in chains:
        gate, lane, q, k = gates[ch["i"]], heads + ch["h"], ch["q"], ch["k"]
        beta = gate["beta"][:, ch["h"]:ch["h"] + 1]
        eg = gate["eg"][:, lane:lane + 1]
        kb = k * beta
        gc_col = jnp.broadcast_to(gate["gc"][:, lane:lane + 1], (c, c))
        gc_row = jnp.sum(jnp.where(diag, gc_col, 0.0), axis=0, keepdims=True)
        ch.update(
            decay=jnp.where(causal, jnp.exp(gc_col - gc_row), 0.0),
            kbq=jnp.concatenate([kb, q], axis=0).astype(BF16),
            rhs=jnp.concatenate([ch["v"] * beta, kb * eg], axis=1),
            q_dec=q * eg,
            k_dec=(k * gate["kdec"][:, lane:lane + 1]).astype(BF16),
            gl=gate["gl"][:, lane:lane + 1])

    for ch in chains:
        ch["kq"] = _dot_nt(ch["kbq"], ch["k_bf"])
    for ch in chains:
        ch["a"] = jnp.where(strict, ch["kq"][:c] * ch["decay"], 0.0)
        ch["qk"] = (ch["kq"][c:] * ch["decay"]).astype(BF16)

    if valid_rows <= SOLVE_BLOCK:
        for ch in chains:
            ch["nil"], ch["sol"] = ch["a"], ch["rhs"]
        _nilpotent_apply(chains, valid_rows)
    else:
        same_block = (row // SOLVE_BLOCK) == (col // SOLVE_BLOCK)
        eye = jnp.where(diag, 1.0, 0.0)
        for ch in chains:
            ch["nil"] = jnp.where(same_block, ch["a"], 0.0)
            ch["sol"] = eye
        _nilpotent_apply(chains, SOLVE_BLOCK)
        for ch in chains:
            off_block = jnp.where(same_block, 0.0, ch["a"])
            right = jnp.concatenate([ch["rhs"], off_block], axis=1)
            ch["prod"] = _dot(ch["sol"].astype(BF16), right.astype(BF16))
        for ch in chains:
            ch["sol"] = ch["prod"][:, :dv + dk]
            ch["nil"] = ch["prod"][:, dv + dk:]
        _nilpotent_apply(chains, c // SOLVE_BLOCK)

    for ch in chains:
        ch["s"] = s_scr[ch["i"], ch["h"]]
        lhs = jnp.concatenate([ch["sol"][:, dv:], ch["q_dec"]], axis=0).astype(BF16)
        ch["ws_qs"] = _dot(lhs, ch["s"].astype(BF16))
    for ch in chains:
        ch["v_new"] = (ch["sol"][:, :dv] - ch["ws_qs"][:c]).astype(BF16)
    for ch in chains:
        s_scr[ch["i"], ch["h"]] = ch["s"] * ch["gl"] + _dot_tn(ch["k_dec"], ch["v_new"])
        ch["o"] = ch["ws_qs"][c:] + _dot(ch["qk"], ch["v_new"])
    for ch in chains:
        ch["ms"] = jnp.mean(ch["o"] * ch["o"], axis=-1, keepdims=True)
    for ch in chains:
        i, h, o = ch["i"], ch["h"], ch["o"]
        o = o * lax.rsqrt(ch["ms"] + NORM_EPS) * nw
        o = o * _silu(z_ref[i, :, h * dv:(h + 1) * dv])
        o_ref[i, :, h * dv:(h + 1) * dv] = o.astype(o_ref.dtype)

    @pl.when(n == last)
    def _():
        snew_ref[...] = s_scr[...]


def _gdn(qkv_src, z_src, proj_ba, s0, a_log, dt_bias, norm_w, conv=None, *, chunk, valid_rows, nb, heads, dk, dv):
    (qkv_arr, qkv_col), (z_arr, z_col) = qkv_src, z_src
    b, t, _ = qkv_arr.shape
    assert valid_rows <= SOLVE_BLOCK or (valid_rows == chunk and chunk % SOLVE_BLOCK == 0)
    conv_ch = heads * (2 * dk + dv)
    width = heads * dv
    n_chunks = t // chunk
    qkv_blk = qkv_col // conv_ch
    z_blk = z_col // width
    lane_vec = lambda x: jnp.zeros((1, LANES), F32).at[0, heads:2 * heads].set(x.astype(F32))
    state_spec = pl.BlockSpec((nb, heads, dk, dv), lambda i, n: (i, 0, 0, 0))
    hist_spec = pl.BlockSpec((nb, HISTORY_ROWS, conv_ch), lambda i, n: (i, 0, 0))
    lane_spec = pl.BlockSpec((1, LANES), lambda i, n: (0, 0))
    in_specs = [
        pl.BlockSpec((nb, chunk, conv_ch), lambda i, n: (i, n, qkv_blk)),
        pl.BlockSpec((nb, chunk, width), lambda i, n: (i, n, z_blk)),
        pl.BlockSpec((nb, chunk, LANES), lambda i, n: (i, n, 0)),
        state_spec, lane_spec, lane_spec,
        pl.BlockSpec((1, dv), lambda i, n: (0, 0)),
    ]
    args = [qkv_arr, z_arr, proj_ba, s0, lane_vec(a_log), lane_vec(dt_bias), norm_w.reshape(1, dv)]
    out_specs = [pl.BlockSpec((nb, chunk, width), lambda i, n: (i, n, 0)), state_spec]
    out_shape = [jax.ShapeDtypeStruct((b, t, width), BF16), jax.ShapeDtypeStruct((b, heads, dk, dv), F32)]
    scratch = [pltpu.VMEM((nb, heads, dk, dv), F32)]
    if conv is not None:
        conv_init, conv_w = conv
        in_specs += [hist_spec, pl.BlockSpec((CONV_WIDTH, conv_ch), lambda i, n: (0, 0))]
        args += [conv_init, conv_w]
        out_specs.append(hist_spec)
        out_shape.append(jax.ShapeDtypeStruct((b, HISTORY_ROWS, conv_ch), F32))
        scratch.append(pltpu.VMEM((nb, HISTORY_ROWS, conv_ch), F32))
    return pl.pallas_call(
        functools.partial(_gdn_kernel, valid_rows=valid_rows, heads=heads, dk=dk, dv=dv,
                          conv_here=conv is not None),
        grid=(b // nb, n_chunks),
        in_specs=in_specs,
        out_specs=out_specs,
        out_shape=out_shape,
        scratch_shapes=scratch,
        compiler_params=_params("parallel", "arbitrary"),
        name="gdn_mixer",
    )(*args)


def _ret_kernel(q_ref, k_ref, v_ref, g_ref, s0_ref, invf_ref, gnw_ref, gnb_ref,
                o_ref, snew_ref, s_scr, *, valid_rows, pos0, heads, dk, dv):
    n = pl.program_id(1)
    nb, c, _ = q_ref.shape
    half = dk // 2

    @pl.when(n == 0)
    def _():
        s_scr[...] = s0_ref[...]

    t_idx = lax.broadcasted_iota(jnp.int32, (c, dk), 0)
    pos = (pos0 + n * c + t_idx).astype(F32)
    ang = pos * invf_ref[...]
    cos2 = jnp.cos(ang)
    lane = lax.broadcasted_iota(jnp.int32, (c, dk), 1)
    sin2 = jnp.where(lane < half, -jnp.sin(ang), jnp.sin(ang))

    row = lax.broadcasted_iota(jnp.int32, (c, c), 0)
    col = lax.broadcasted_iota(jnp.int32, (c, c), 1)
    rel = (row - col).astype(F32)
    idx = lax.broadcasted_iota(jnp.int32, (c, 1), 0).astype(F32)
    live = lax.broadcasted_iota(jnp.int32, (c, 1), 0) < valid_rows

    chains = []
    for h in range(heads):
        log_gamma = math.log(1.0 - 2.0 ** (-5.0 - h))
        dmat = jnp.where(rel >= 0, jnp.exp(jnp.maximum(rel, 0.0) * log_gamma), 0.0)
        q_scale = jnp.exp((idx + 1.0) * log_gamma)
        k_scale = jnp.exp((valid_rows - 1.0 - idx) * log_gamma)
        for i in range(nb):
            q = q_ref[i, :, h * dk:(h + 1) * dk]
            k = k_ref[i, :, h * dk:(h + 1) * dk]
            v = v_ref[i, :, h * dv:(h + 1) * dv]
            if valid_rows < c:
                v = jnp.where(live, v, 0.0)
            q = q * cos2 + pltpu.roll(q, half, 1) * sin2
            k = (k * cos2 + pltpu.roll(k, half, 1) * sin2) * (dk ** -0.5)
            chains.append(dict(
                i=i, h=h, dmat=dmat, s_decay=math.exp(valid_rows * log_gamma),
                q_bf=q.astype(BF16), k_bf=k.astype(BF16), v_bf=v.astype(BF16),
                q_dec=(q * q_scale).astype(BF16), k_dec=(k * k_scale).astype(BF16)))

    for ch in chains:
        ch["inner"] = (_dot_nt(ch["q_bf"], ch["k_bf"]) * ch["dmat"]).astype(BF16)
    for ch in chains:
        s = s_scr[ch["i"], ch["h"]]
        ch["o"] = _dot(ch["q_dec"], s.astype(BF16)) + _dot(ch["inner"], ch["v_bf"])
        s_scr[ch["i"], ch["h"]] = s * ch["s_decay"] + _dot_tn(ch["k_dec"], ch["v_bf"])
    for ch in chains:
        ch["mu"] = jnp.mean(ch["o"], axis=-1, keepdims=True)
    for ch in chains:
        ch["oc"] = ch["o"] - ch["mu"]
        ch["var"] = jnp.mean(ch["oc"] * ch["oc"], axis=-1, keepdims=True)
    for ch in chains:
        i, h = ch["i"], ch["h"]
        o = ch["oc"] * lax.rsqrt(ch["var"] + NORM_EPS)
        o = o * gnw_ref[:, h * dv:(h + 1) * dv] + gnb_ref[:, h * dv:(h + 1) * dv]
        o = o * _silu(g_ref[i, :, h * dv:(h + 1) * dv])
        o_ref[i, :, h * dv:(h + 1) * dv] = o.astype(o_ref.dtype)

    @pl.when(n == pl.num_programs(1) - 1)
    def _():
        snew_ref[...] = s_scr[...]


def _ret(proj, col0, s0, gn_w, gn_b, *, chunk, valid_rows, nb, pos0, heads, dk, dv):
    b, t, _ = proj.shape
    width = heads * dv
    n_chunks = t // chunk
    blk0 = col0 // width
    half = dk // 2
    inv_freq = ROPE_BASE ** (-jnp.arange(half, dtype=F32) / half)
    inv_freq2 = jnp.concatenate([inv_freq, inv_freq]).reshape(1, dk)
    col_spec = lambda kk: pl.BlockSpec((nb, chunk, width), lambda i, n: (i, n, blk0 + kk))
    return pl.pallas_call(
        functools.partial(_ret_kernel, valid_rows=valid_rows, pos0=pos0, heads=heads, dk=dk, dv=dv),
        grid=(b // nb, n_chunks),
        in_specs=[
            col_spec(0), col_spec(1), col_spec(2), col_spec(3),
            pl.BlockSpec((nb, heads, dk, dv), lambda i, n: (i, 0, 0, 0)),
            pl.BlockSpec((1, dk), lambda i, n: (0, 0)),
            pl.BlockSpec((1, width), lambda i, n: (0, 0)),
            pl.BlockSpec((1, width), lambda i, n: (0, 0)),
        ],
        out_specs=[
            pl.BlockSpec((nb, chunk, width), lambda i, n: (i, n, 0)),
            pl.BlockSpec((nb, heads, dk, dv), lambda i, n: (i, 0, 0, 0)),
        ],
        out_shape=[
            jax.ShapeDtypeStruct((b, t, width), BF16),
            jax.ShapeDtypeStruct((b, heads, dk, dv), F32),
        ],
        scratch_shapes=[pltpu.VMEM((nb, heads, dk, dv), F32)],
        compiler_params=_params("parallel", "arbitrary"),
        name="ret_mixer",
    )(proj, proj, proj, proj, s0, inv_freq2, gn_w.reshape(1, width), gn_b.reshape(1, width))


def _trunk(x, mod, mod_f, rows_per_group, seq_rows, valid_rows, nb, pos0, to_seq, from_seq,
           conv_init, s_gdn0, s_ret0, w, tm, tf):
    heads, dk, dv = s_gdn0.shape[1:]
    x1, h2 = _ffn(x, mod, (0, 1, 2), w["norm_ffn1"], w["w1_gate"], w["w1_up"], w["w1_down"],
                  mod, (3, 4), w["norm_mix"], rows_per_group=rows_per_group, tm=tm, tf=tf,
                  emit_x=True, hn_dtype=BF16)
    conv_ch = heads * (2 * dk + dv)
    width = heads * dv
    ba0 = conv_ch + width
    tn = PROJ_COLS
    proj_ba = to_seq(_proj(h2, w["w_in_t"], lambda j: ba0, LANES, tm, LANES))
    chunk = min(CHUNK, seq_rows)
    nb_gdn, nb_ret = nb
    gdn_args = dict(chunk=chunk, valid_rows=valid_rows, nb=nb_gdn, heads=heads, dk=dk, dv=dv)
    if conv_init is None:
        qkv, conv_new = _proj_conv(h2, w["w_in_t"], w["conv_w"], conv_ch, seq_rows, tm, tn)
        skip = lambda j: conv_ch + j * tn + jnp.where(j > 0, 2 * heads, 0)
        rest = to_seq(_proj(h2, w["w_in_t"], skip, 5 * width, tm, tn))
        o_gdn, s_gdn = _gdn((to_seq(qkv), 0), (rest, 0), proj_ba, s_gdn0, w["a_log"], w["dt_bias"],
                            w["gdn_norm_w"], **gdn_args)
        ret_src, ret_col = rest, width
    else:
        skip = lambda j: j * tn + jnp.where(j * tn >= ba0, 2 * heads, 0)
        proj = to_seq(_proj(h2, w["w_in_t"], skip, ba0 + 4 * width, tm, tn))
        o_gdn, s_gdn, conv_new = _gdn((proj, 0), (proj, conv_ch), proj_ba, s_gdn0, w["a_log"], w["dt_bias"],
                                      w["gdn_norm_w"], (conv_init, w["conv_w"]), **gdn_args)
        ret_src, ret_col = proj, ba0
    o_ret, s_ret = _ret(ret_src, ret_col, s_ret0, w["ret_gn_w"], w["ret_gn_b"],
                        chunk=chunk, valid_rows=valid_rows, nb=nb_ret, pos0=pos0, heads=heads, dk=dk, dv=dv)
    o_gdn = from_seq(o_gdn)
    o_ret = from_seq(o_ret)
    x2 = _outproj(x1, mod, 5, o_gdn, o_ret, w["w_out"], rows_per_group=rows_per_group, tm=min(tm, OUTPROJ_ROWS))
    (y,) = _ffn(x2, mod, (6, 7, 8), w["norm_ffn2"], w["w2_gate"], w["w2_up"], w["w2_down"],
                mod_f, (0, 1), w["norm_final"], rows_per_group=rows_per_group, tm=tm, tf=tf,
                emit_x=False, hn_dtype=F32)
    return y, s_gdn, conv_new[:, SUBLANES - (CONV_WIDTH - 1):, :], s_ret


def kernel(x_prompt, x_sample, state_gdn, state_conv, state_ret, c_prompt, c_sample, w_ada, b_ada, norm_ffn1, w1_gate, w1_up, w1_down, norm_mix, w_in, conv_w, a_log, dt_bias, gdn_norm_w, ret_gn_w, ret_gn_b, w_out, norm_ffn2, w2_gate, w2_up, w2_down, w_ada_final, b_ada_final, norm_final):
    bp, tp, d = x_prompt.shape
    bs, ts, _ = x_sample.shape
    depth, _, heads, dk, dv = state_gdn.shape
    assert depth == 1, "single-layer trunk"
    assert bs == ROW_TILE, "time-major sample rows must align adaLN vectors with row tiles"
    conv_ch = state_conv.shape[-1]
    width = heads * dv
    ba0 = conv_ch + width
    ret0 = ba0 + 2 * heads

    n_c = bs + bp
    n_c_pad = -(-n_c // SUBLANES) * SUBLANES
    c_all = jnp.concatenate([c_sample, c_prompt, jnp.zeros((n_c_pad - n_c, d), F32)], axis=0)
    ada = _ada_proj(c_all, w_ada[0], b_ada[0], tn=ADA_COLS)
    ada_f = _ada_proj(c_all, w_ada_final, b_ada_final, tn=ADA_COLS)
    mod_s, mod_p = ada, ada[bs:bs + bp].reshape(bp, 1, N_ADA * d)
    modf_s, modf_p = ada_f, ada_f[bs:bs + bp].reshape(bp, 1, 2 * d)

    assert ba0 % PROJ_COLS == 0 and ret0 % SUBLANES == 0 and w_in.shape[-1] == ret0 + 4 * width
    w = dict(norm_ffn1=norm_ffn1[0], w1_gate=w1_gate[0], w1_up=w1_up[0], w1_down=w1_down[0],
             norm_mix=norm_mix[0], w_in_t=w_in[0].T, conv_w=conv_w[0], a_log=a_log[0],
             dt_bias=dt_bias[0], gdn_norm_w=gdn_norm_w[0], ret_gn_w=ret_gn_w[0], ret_gn_b=ret_gn_b[0],
             w_out=w_out[0], norm_ffn2=norm_ffn2[0], w2_gate=w2_gate[0], w2_up=w2_up[0],
             w2_down=w2_down[0], norm_final=norm_final)

    zeros_state = jnp.zeros((bp, heads, dk, dv), F32)
    y_p, gdn_p, conv_p, ret_p = _trunk(
        x_prompt.reshape(bp * tp, d), mod_p, modf_p, tp, tp, min(CHUNK, tp), (bp, SEQS_PER_STEP_FULL), 0,
        lambda a: a.reshape(bp, tp, a.shape[-1]), lambda a: a.reshape(bp * tp, a.shape[-1]),
        None, zeros_state, zeros_state, w, tm=FFN_ROWS, tf=FFN_COLS)

    ts_pad = -(-ts // SUBLANES) * SUBLANES
    to_seq = lambda a: jnp.pad(a.reshape(ts, bs, a.shape[-1]).transpose(1, 0, 2),
                               ((0, 0), (0, ts_pad - ts), (0, 0)))
    from_seq = lambda a: a[:, :ts].transpose(1, 0, 2).reshape(ts * bs, a.shape[-1])
    conv_init = jnp.pad(state_conv[0], ((0, 0), (SUBLANES - (CONV_WIDTH - 1), 0), (0, 0)))
    y_s, gdn_s, conv_s, ret_s = _trunk(
        x_sample.transpose(1, 0, 2).reshape(ts * bs, d), mod_s, modf_s, None, ts_pad, ts,
        (SEQS_PER_STEP_SHORT, SEQS_PER_STEP_SHORT), PAST_LEN,
        to_seq, from_seq, conv_init, state_gdn[0], state_ret[0], w, tm=ts * bs, tf=FFN_COLS_SMALL)
    y_s = y_s.reshape(ts, bs, d).transpose(1, 0, 2)

    return (y_p.reshape(bp, tp, d), y_s, gdn_p[None], conv_p[None], ret_p[None],
            gdn_s[None], conv_s[None], ret_s[None])
```

```python
import functools
import math

import jax
import jax.numpy as jnp
from jax import lax
from jax.experimental import pallas as pl
from jax.experimental.pallas import tpu as pltpu

F32 = jnp.float32
BF16 = jnp.bfloat16

LANES = 128
SUBLANES = 8
MXU_COLS = 256
VMEM_LIMIT_BYTES = 60 * 1024 * 1024

PAST_LEN = 16384
CONV_WIDTH = 4
CHUNK = 64
ROPE_BASE = 10000.0
NORM_EPS = 1e-6
L2_EPS = 1e-6
MACARON_WEIGHT = 0.5
N_ADA = 9
FFN_ROWS = 1024
FFN_COLS = 256
FFN_COLS_SMALL = 512
OUTPROJ_ROWS = 512
ADA_COLS = 512
SEQS_PER_STEP_FULL = 4
SEQS_PER_STEP_SHORT = 8
ROW_TILE = 128
NORM_ROWS = 16
FFN_SUB_ROWS = 512
PROJ_COLS = 1024
HISTORY_ROWS = SUBLANES
CONV_SUB_ROWS = 256


def _sigmoid(x):
    return 1.0 / (1.0 + jnp.exp(-x))


def _silu(x):
    return x * _sigmoid(x)


def _softplus(x):
    return jnp.maximum(x, 0.0) + jnp.log(1.0 + jnp.exp(-jnp.abs(x)))


def _dot(a, b):
    return jnp.dot(a, b, preferred_element_type=F32)


def _dot_nt(a, b):
    return lax.dot_general(a, b, (((1,), (1,)), ((), ())), preferred_element_type=F32)


def _dot_tn(a, b):
    return lax.dot_general(a, b, (((0,), (0,)), ((), ())), preferred_element_type=F32)


def _split3(x):
    hi = x.astype(BF16)
    r = x - hi.astype(F32)
    mid = r.astype(BF16)
    lo = (r - mid.astype(F32)).astype(BF16)
    return hi, mid, lo


def _rms_mod(x, gain, shift, scale):
    y = x * lax.rsqrt(jnp.mean(x * x, axis=-1, keepdims=True) + NORM_EPS)
    return (y * gain) * (1.0 + scale) + shift


def _mod_rows(ref, g):
    if ref.shape[0] == 1:
        return ref[...]
    return ref[g * NORM_ROWS:(g + 1) * NORM_ROWS, :]


def _for_row_groups(n_rows, fn):
    def tile(t, carry):
        base = pl.multiple_of(t * ROW_TILE, ROW_TILE)
        for g in range(ROW_TILE // NORM_ROWS):
            fn(pl.ds(base + g * NORM_ROWS, NORM_ROWS), g)
        return carry
    lax.fori_loop(0, n_rows // ROW_TILE, tile, 0)


def _params(*sem):
    return pltpu.CompilerParams(dimension_semantics=sem, vmem_limit_bytes=VMEM_LIMIT_BYTES)


def _ada_kernel(c_ref, w_ref, b_ref, o_ref):
    a = _silu(c_ref[...]).astype(BF16)
    o_ref[...] = _dot(a, w_ref[...].astype(BF16)) + b_ref[...]


def _ada_proj(c, w, b, tn):
    m, d = c.shape
    n = w.shape[1]
    return pl.pallas_call(
        _ada_kernel,
        grid=(n // tn,),
        in_specs=[
            pl.BlockSpec((m, d), lambda j: (0, 0)),
            pl.BlockSpec((d, tn), lambda j: (0, j)),
            pl.BlockSpec((1, tn), lambda j: (0, j)),
        ],
        out_specs=pl.BlockSpec((m, tn), lambda j: (0, j)),
        out_shape=jax.ShapeDtypeStruct((m, n), F32),
        compiler_params=_params("parallel"),
        name="ada_proj",
    )(c, w, b.reshape(1, n))


def _ffn_kernel(x_ref, sh_ref, sc_ref, gt_ref, gain_ref, wg_ref, wu_ref, wd_ref,
                gain2_ref, sh2_ref, sc2_ref, *rest, emit_x):
    if emit_x:
        xo_ref, hn_hbm, h_scr, stage_scr, stage_sem = rest
        acc_ref = xo_ref
    else:
        hn_hbm, h_scr, stage_scr, stage_sem, acc_ref = rest
    j = pl.program_id(1)
    tm = x_ref.shape[0]

    @pl.when(j == 0)
    def _():
        def group(rows, g):
            h = _rms_mod(x_ref[rows, :], gain_ref[...], _mod_rows(sh_ref, g), _mod_rows(sc_ref, g))
            h_scr[rows, :] = h.astype(BF16)
            acc_ref[rows, :] = jnp.zeros((NORM_ROWS, acc_ref.shape[1]), F32)
        _for_row_groups(tm, group)

    wg = wg_ref[...].astype(BF16)
    wu = wu_ref[...].astype(BF16)
    wd = wd_ref[...].astype(BF16)
    sub = min(tm, FFN_SUB_ROWS)
    for r in range(tm // sub):
        rows = slice(r * sub, (r + 1) * sub)
        h = h_scr[rows, :]
        a = (_silu(_dot(h, wg)) * _dot(h, wu)).astype(BF16)
        acc_ref[rows, :] += _dot(a, wd)

    @pl.when(j == pl.num_programs(1) - 1)
    def _():
        i = pl.program_id(0)
        n_tiles = tm // ROW_TILE

        def hn_copy(t, block):
            row = pl.multiple_of(block * tm + t * ROW_TILE, ROW_TILE)
            return pltpu.make_async_copy(stage_scr.at[t], hn_hbm.at[pl.ds(row, ROW_TILE), :], stage_sem.at[t])

        def wait_block(block):
            for t in range(n_tiles):
                hn_copy(t, block).wait()

        @pl.when(i > 0)
        def _():
            wait_block(i - 1)

        def tile(t, carry):
            base = pl.multiple_of(t * ROW_TILE, ROW_TILE)
            for g in range(ROW_TILE // NORM_ROWS):
                rows = pl.ds(base + g * NORM_ROWS, NORM_ROWS)
                xo = x_ref[rows, :] + (MACARON_WEIGHT * _mod_rows(gt_ref, g)) * acc_ref[rows, :]
                if emit_x:
                    xo_ref[rows, :] = xo
                hn = _rms_mod(xo, gain2_ref[...], _mod_rows(sh2_ref, g), _mod_rows(sc2_ref, g))
                stage_scr[t, g * NORM_ROWS:(g + 1) * NORM_ROWS, :] = hn.astype(stage_scr.dtype)
            hn_copy(t, i).start()
            return carry
        lax.fori_loop(0, n_tiles, tile, 0)

        @pl.when(i == pl.num_programs(0) - 1)
        def _():
            wait_block(i)


def _mod_spec(mod, k, d, rows_per_group, tm):
    if mod.ndim == 3:
        blocks_per_group = rows_per_group // tm
        return pl.BlockSpec((None, 1, d), lambda i, j: (i // blocks_per_group, 0, k))
    return pl.BlockSpec((ROW_TILE, d), lambda i, j: (0, k))


def _ffn(x, mod, ks, gain, wg, wu, wd, mod2, ks2, gain2, *, rows_per_group, tm, tf, emit_x, hn_dtype):
    m, d = x.shape
    f = wg.shape[1]
    assert m % tm == 0 and f % tf == 0 and tm % ROW_TILE == 0 and d % LANES == 0
    assert mod.ndim == 2 or rows_per_group % tm == 0
    k_sh, k_sc, k_gt = ks
    k_sh2, k_sc2 = ks2
    vec = lambda: pl.BlockSpec((1, d), lambda i, j: (0, 0))
    row_block = lambda: pl.BlockSpec((tm, d), lambda i, j: (i, 0))
    out_shape = [jax.ShapeDtypeStruct((m, d), hn_dtype)]
    out_specs = [pl.BlockSpec(memory_space=pl.ANY)]
    if emit_x:
        out_shape = [jax.ShapeDtypeStruct((m, d), F32)] + out_shape
        out_specs = [row_block()] + out_specs
    return pl.pallas_call(
        functools.partial(_ffn_kernel, emit_x=emit_x),
        grid=(m // tm, f // tf),
        in_specs=[
            row_block(),
            _mod_spec(mod, k_sh, d, rows_per_group, tm),
            _mod_spec(mod, k_sc, d, rows_per_group, tm),
            _mod_spec(mod, k_gt, d, rows_per_group, tm),
            vec(),
            pl.BlockSpec((d, tf), lambda i, j: (0, j)),
            pl.BlockSpec((d, tf), lambda i, j: (0, j)),
            pl.BlockSpec((tf, d), lambda i, j: (j, 0)),
            vec(),
            _mod_spec(mod2, k_sh2, d, rows_per_group, tm),
            _mod_spec(mod2, k_sc2, d, rows_per_group, tm),
        ],
        out_specs=out_specs,
        out_shape=out_shape,
        scratch_shapes=[pltpu.VMEM((tm, d), BF16), pltpu.VMEM((tm // ROW_TILE, ROW_TILE, d), hn_dtype),
                        pltpu.SemaphoreType.DMA((tm // ROW_TILE,))]
        + ([] if emit_x else [pltpu.VMEM((tm, d), F32)]),
        compiler_params=_params("arbitrary", "arbitrary"),
        name="ffn",
    )(x, mod, mod, mod, gain.reshape(1, d), wg, wu, wd, gain2.reshape(1, d), mod2, mod2)


def _proj_kernel(h_ref, wt_ref, o_ref, wb_scr):
    @pl.when(pl.program_id(1) == 0)
    def _():
        wb_scr[...] = wt_ref[...].astype(BF16)

    o_ref[...] = _dot_nt(h_ref[...], wb_scr[...])


def _proj(h, wt, row_of, n_cols, tm, tn):
    m, d = h.shape
    return pl.pallas_call(
        _proj_kernel,
        grid=(n_cols // tn, m // tm),
        in_specs=[
            pl.BlockSpec((tm, d), lambda j, i: (i, 0)),
            pl.BlockSpec((pl.Element(tn), pl.Element(d)),
                         lambda j, i: (pl.multiple_of(row_of(j), SUBLANES), 0)),
        ],
        out_specs=pl.BlockSpec((tm, tn), lambda j, i: (i, j)),
        out_shape=jax.ShapeDtypeStruct((m, n_cols), F32),
        scratch_shapes=[pltpu.VMEM((tn, d), BF16)],
        compiler_params=_params("parallel", "arbitrary"),
        name="in_proj",
    )(h, wt)


def _proj_conv_kernel(h_ref, wt_ref, convw_ref, o_ref, cnew_ref, wb_scr, tail_scr, *, blocks_per_seq):
    i = pl.program_id(1)

    @pl.when(i == 0)
    def _():
        wb_scr[...] = wt_ref[...].astype(BF16)

    h = h_ref[...]
    tm, tn = o_ref.shape
    first = i % blocks_per_seq == 0
    for cg in range(tn // MXU_COLS):
        cols = slice(cg * MXU_COLS, (cg + 1) * MXU_COLS)
        raw = _dot_nt(h, wb_scr[cols, :])
        w = convw_ref[:, cols]
        prev = jnp.where(first, 0.0, tail_scr[:, cols])
        for r in range(tm // CONV_SUB_ROWS):
            rows = slice(r * CONV_SUB_ROWS, (r + 1) * CONV_SUB_ROWS)
            x = raw[rows, :]
            y, _ = _causal_conv_silu(prev, x, w)
            o_ref[rows, cols] = y
            prev = x[CONV_SUB_ROWS - HISTORY_ROWS:, :]
        tail_scr[:, cols] = prev
        cnew_ref[:, cols] = prev


def _proj_conv(h, wt, conv_w, n_cols, rows_per_seq, tm, tn):
    m, d = h.shape
    blocks_per_seq = rows_per_seq // tm
    return pl.pallas_call(
        functools.partial(_proj_conv_kernel, blocks_per_seq=blocks_per_seq),
        grid=(n_cols // tn, m // tm),
        in_specs=[
            pl.BlockSpec((tm, d), lambda j, i: (i, 0)),
            pl.BlockSpec((tn, d), lambda j, i: (j, 0)),
            pl.BlockSpec((CONV_WIDTH, tn), lambda j, i: (0, j)),
        ],
        out_specs=[
            pl.BlockSpec((tm, tn), lambda j, i: (i, j)),
            pl.BlockSpec((None, HISTORY_ROWS, tn), lambda j, i: (i // blocks_per_seq, 0, j)),
        ],
        out_shape=[
            jax.ShapeDtypeStruct((m, n_cols), F32),
            jax.ShapeDtypeStruct((m // rows_per_seq, HISTORY_ROWS, n_cols), F32),
        ],
        scratch_shapes=[pltpu.VMEM((tn, d), BF16), pltpu.VMEM((HISTORY_ROWS, tn), F32)],
        compiler_params=_params("parallel", "arbitrary"),
        name="in_proj_conv",
    )(h, wt, conv_w)


def _outproj_kernel(x_ref, gt_ref, og_ref, or_ref, w_ref, o_ref, wb_scr):
    @pl.when(pl.program_id(0) == 0)
    def _():
        wb_scr[...] = w_ref[...].astype(BF16)

    kw = og_ref.shape[1]
    tm = x_ref.shape[0]
    mixed = _dot(og_ref[...], wb_scr[:kw, :]) + _dot(or_ref[...], wb_scr[kw:, :])
    for r in range(tm // ROW_TILE):
        rows = slice(r * ROW_TILE, (r + 1) * ROW_TILE)
        o_ref[rows, :] = x_ref[rows, :] + gt_ref[...] * mixed[rows, :]


def _outproj(x, mod, k_gt, og, orr, w_out, *, rows_per_group, tm):
    m, d = x.shape
    kw = og.shape[1]
    if mod.ndim == 3:
        blocks_per_group = rows_per_group // tm
        gt_spec = pl.BlockSpec((None, 1, d), lambda i: (i // blocks_per_group, 0, k_gt))
    else:
        gt_spec = pl.BlockSpec((ROW_TILE, d), lambda i: (0, k_gt))
    return pl.pallas_call(
        _outproj_kernel,
        grid=(m // tm,),
        in_specs=[
            pl.BlockSpec((tm, d), lambda i: (i, 0)),
            gt_spec,
            pl.BlockSpec((tm, kw), lambda i: (i, 0)),
            pl.BlockSpec((tm, kw), lambda i: (i, 0)),
            pl.BlockSpec(w_out.shape, lambda i: (0, 0), pipeline_mode=pl.Buffered(1)),
        ],
        out_specs=pl.BlockSpec((tm, d), lambda i: (i, 0)),
        out_shape=jax.ShapeDtypeStruct((m, d), F32),
        scratch_shapes=[pltpu.VMEM(w_out.shape, BF16)],
        compiler_params=_params("arbitrary"),
        name="out_proj",
    )(x, mod, og, orr, w_out)


SOLVE_BLOCK = 16


def _nilpotent_apply(chains, index):
    power = 1
    while power < index:
        square = 2 * power < index
        for ch in chains:
            right = jnp.concatenate([ch["sol"], ch["nil"]], axis=1) if square else ch["sol"]
            ch["prod"] = _dot(ch["nil"].astype(BF16), right.astype(BF16))
        for ch in chains:
            width = ch["sol"].shape[1]
            upd = ch["prod"][:, :width]
            ch["sol"] = ch["sol"] - upd if power == 1 else ch["sol"] + upd
            if square:
                ch["nil"] = ch["prod"][:, width:]
        power *= 2


def _causal_conv_silu(prev, x, w):
    xp = jnp.concatenate([prev, x], axis=0)
    acc = x * w[CONV_WIDTH - 1:CONV_WIDTH, :]
    for sft in range(1, CONV_WIDTH):
        tap = CONV_WIDTH - 1 - sft
        acc = acc + pltpu.roll(xp, sft, 0)[HISTORY_ROWS:, :] * w[tap:tap + 1, :]
    return _silu(acc), xp


def _gdn_kernel(*refs, valid_rows, heads, dk, dv, conv_here):
    if conv_here:
        (qkv_ref, z_ref, ba_ref, s0_ref, alog_ref, dtb_ref, nw_ref, cinit_ref, convw_ref,
         o_ref, snew_ref, cnew_ref, s_scr, tail_scr) = refs
    else:
        qkv_ref, z_ref, ba_ref, s0_ref, alog_ref, dtb_ref, nw_ref, o_ref, snew_ref, s_scr = refs
    n = pl.program_id(1)
    last = pl.num_programs(1) - 1
    nb, c, _ = qkv_ref.shape

    @pl.when(n == 0)
    def _():
        s_scr[...] = s0_ref[...]
        if conv_here:
            tail_scr[...] = cinit_ref[...]

    row = lax.broadcasted_iota(jnp.int32, (c, c), 0)
    col = lax.broadcasted_iota(jnp.int32, (c, c), 1)
    causal = row >= col
    strict = row > col
    diag = row == col
    tril = jnp.where(causal, 1.0, 0.0).astype(BF16)
    nw = nw_ref[...]
    neg_a = -jnp.exp(alog_ref[...])
    dtb = dtb_ref[...]

    gates = []
    for i in range(nb):
        ba = ba_ref[i]
        beta_all = _sigmoid(ba)
        g_all = neg_a * _softplus(ba + dtb)
        if valid_rows < c:
            live = lax.broadcasted_iota(jnp.int32, ba.shape, 0) < valid_rows
            beta_all = jnp.where(live, beta_all, 0.0)
            g_all = jnp.where(live, g_all, 0.0)
        g_hi, g_mid, g_lo = _split3(g_all)
        gc_all = _dot(tril, g_hi) + (_dot(tril, g_mid) + _dot(tril, g_lo))
        gc_last_all = gc_all[c - 1:c, :]
        gates.append(dict(beta=beta_all, gc=gc_all, eg=jnp.exp(gc_all),
                          kdec=jnp.exp(gc_last_all - gc_all), gl=jnp.exp(gc_last_all)))

    chains = []
    for i in range(nb):
        if conv_here:
            x = qkv_ref[i]
            qkv, xp = _causal_conv_silu(tail_scr[i], x, convw_ref[...])
            tail_scr[i] = x[c - HISTORY_ROWS:, :]

            @pl.when(n == last)
            def _(i=i, xp=xp):
                cnew_ref[i] = xp[valid_rows:valid_rows + HISTORY_ROWS, :]
        else:
            qkv = qkv_ref[i]
        for h in range(heads):
            q = qkv[:, h * dk:(h + 1) * dk]
            k = qkv[:, heads * dk + h * dk:heads * dk + (h + 1) * dk]
            v = qkv[:, 2 * heads * dk + h * dv:2 * heads * dk + (h + 1) * dv]
            q = q * (lax.rsqrt(jnp.sum(q * q, axis=-1, keepdims=True) + L2_EPS) * (dk ** -0.5))
            k = k * lax.rsqrt(jnp.sum(k * k, axis=-1, keepdims=True) + L2_EPS)
            chains.append(dict(i=i, h=h, q=q, k=k, v=v, k_bf=k.astype(BF16)))

    for ch in chains:
        gate, lane, q, k = gates[ch["i"]], heads + ch["h"], ch["q"], ch["k"]
        beta = gate["beta"][:, ch["h"]:ch["h"] + 1]
        eg = gate["eg"][:, lane:lane + 1]
        kb = k * beta
        gc_col = jnp.broadcast_to(gate["gc"][:, lane:lane + 1], (c, c))
        gc_row = jnp.sum(jnp.where(diag, gc_col, 0.0), axis=0, keepdims=True)
        ch.update(
            decay=jnp.where(causal, jnp.exp(gc_col - gc_row), 0.0),
            kbq=jnp.concatenate([kb, q], axis=0).astype(BF16),
            rhs=jnp.concatenate([ch["v"] * beta, kb * eg], axis=1),
            q_dec=q * eg,
            k_dec=(k * gate["kdec"][:, lane:lane + 1]).astype(BF16),
            gl=gate["gl"][:, lane:lane + 1])

    for ch in chains:
        ch["kq"] = _dot_nt(ch["kbq"], ch["k_bf"])
    for ch in chains:
        ch["a"] = jnp.where(strict, ch["kq"][:c] * ch["decay"], 0.0)
        ch["qk"] = (ch["kq"][c:] * ch["decay"]).astype(BF16)

    if valid_rows <= SOLVE_BLOCK:
        for ch in chains:
            ch["nil"], ch["sol"] = ch["a"], ch["rhs"]
        _nilpotent_apply(chains, valid_rows)
    else:
        same_block = (row // SOLVE_BLOCK) == (col // SOLVE_BLOCK)
        eye = jnp.where(diag, 1.0, 0.0)
        for ch in chains:
            ch["nil"] = jnp.where(same_block, ch["a"], 0.0)
            ch["sol"] = eye
        _nilpotent_apply(chains, SOLVE_BLOCK)
        for ch in chains:
            off_block = jnp.where(same_block, 0.0, ch["a"])
            right = jnp.concatenate([ch["rhs"], off_block], axis=1)
            ch["prod"] = _dot(ch["sol"].astype(BF16), right.astype(BF16))
        for ch in chains:
            ch["sol"] = ch["prod"][:, :dv + dk]
            ch["nil"] = ch["prod"][:, dv + dk:]
        _nilpotent_apply(chains, c // SOLVE_BLOCK)

    for ch in chains:
        ch["s"] = s_scr[ch["i"], ch["h"]]
        lhs = jnp.concatenate([ch["sol"][:, dv:], ch["q_dec"]], axis=0).astype(BF16)
        ch["ws_qs"] = _dot(lhs, ch["s"].astype(BF16))
    for ch in chains:
        ch["v_new"] = (ch["sol"][:, :dv] - ch["ws_qs"][:c]).astype(BF16)
    for ch in chains:
        s_scr[ch["i"], ch["h"]] = ch["s"] * ch["gl"] + _dot_tn(ch["k_dec"], ch["v_new"])
        ch["o"] = ch["ws_qs"][c:] + _dot(ch["qk"], ch["v_new"])
    for ch in chains:
        ch["ms"] = jnp.mean(ch["o"] * ch["o"], axis=-1, keepdims=True)
    for ch in chains:
        i, h, o = ch["i"], ch["h"], ch["o"]
        o = o * lax.rsqrt(ch["ms"] + NORM_EPS) * nw
        o = o * _silu(z_ref[i, :, h * dv:(h + 1) * dv])
        o_ref[i, :, h * dv:(h + 1) * dv] = o.astype(o_ref.dtype)

    @pl.when(n == last)
    def _():
        snew_ref[...] = s_scr[...]


def _gdn(qkv_src, z_src, proj_ba, s0, a_log, dt_bias, norm_w, conv=None, *, chunk, valid_rows, nb, heads, dk, dv):
    (qkv_arr, qkv_col), (z_arr, z_col) = qkv_src, z_src
    b, t, _ = qkv_arr.shape
    assert valid_rows <= SOLVE_BLOCK or (valid_rows == chunk and chunk % SOLVE_BLOCK == 0)
    conv_ch = heads * (2 * dk + dv)
    width = heads * dv
    n_chunks = t // chunk
    qkv_blk = qkv_col // conv_ch
    z_blk = z_col // width
    lane_vec = lambda x: jnp.zeros((1, LANES), F32).at[0, heads:2 * heads].set(x.astype(F32))
    state_spec = pl.BlockSpec((nb, heads, dk, dv), lambda i, n: (i, 0, 0, 0))
    hist_spec = pl.BlockSpec((nb, HISTORY_ROWS, conv_ch), lambda i, n: (i, 0, 0))
    lane_spec = pl.BlockSpec((1, LANES), lambda i, n: (0, 0))
    in_specs = [
        pl.BlockSpec((nb, chunk, conv_ch), lambda i, n: (i, n, qkv_blk)),
        pl.BlockSpec((nb, chunk, width), lambda i, n: (i, n, z_blk)),
        pl.BlockSpec((nb, chunk, LANES), lambda i, n: (i, n, 0)),
        state_spec, lane_spec, lane_spec,
        pl.BlockSpec((1, dv), lambda i, n: (0, 0)),
    ]
    args = [qkv_arr, z_arr, proj_ba, s0, lane_vec(a_log), lane_vec(dt_bias), norm_w.reshape(1, dv)]
    out_specs = [pl.BlockSpec((nb, chunk, width), lambda i, n: (i, n, 0)), state_spec]
    out_shape = [jax.ShapeDtypeStruct((b, t, width), BF16), jax.ShapeDtypeStruct((b, heads, dk, dv), F32)]
    scratch = [pltpu.VMEM((nb, heads, dk, dv), F32)]
    if conv is not None:
        conv_init, conv_w = conv
        in_specs += [hist_spec, pl.BlockSpec((CONV_WIDTH, conv_ch), lambda i, n: (0, 0))]
        args += [conv_init, conv_w]
        out_specs.append(hist_spec)
        out_shape.append(jax.ShapeDtypeStruct((b, HISTORY_ROWS, conv_ch), F32))
        scratch.append(pltpu.VMEM((nb, HISTORY_ROWS, conv_ch), F32))
    return pl.pallas_call(
        functools.partial(_gdn_kernel, valid_rows=valid_rows, heads=heads, dk=dk, dv=dv,
                          conv_here=conv is not None),
        grid=(b // nb, n_chunks),
        in_specs=in_specs,
        out_specs=out_specs,
        out_shape=out_shape,
        scratch_shapes=scratch,
        compiler_params=_params("parallel", "arbitrary"),
        name="gdn_mixer",
    )(*args)


def _ret_kernel(q_ref, k_ref, v_ref, g_ref, s0_ref, invf_ref, gnw_ref, gnb_ref,
                o_ref, snew_ref, s_scr, *, valid_rows, pos0, heads, dk, dv):
    n = pl.program_id(1)
    nb, c, _ = q_ref.shape
    half = dk // 2

    @pl.when(n == 0)
    def _():
        s_scr[...] = s0_ref[...]

    t_idx = lax.broadcasted_iota(jnp.int32, (c, dk), 0)
    pos = (pos0 + n * c + t_idx).astype(F32)
    ang = pos * invf_ref[...]
    cos2 = jnp.cos(ang)
    lane = lax.broadcasted_iota(jnp.int32, (c, dk), 1)
    sin2 = jnp.where(lane < half, -jnp.sin(ang), jnp.sin(ang))

    row = lax.broadcasted_iota(jnp.int32, (c, c), 0)
    col = lax.broadcasted_iota(jnp.int32, (c, c), 1)
    rel = (row - col).astype(F32)
    idx = lax.broadcasted_iota(jnp.int32, (c, 1), 0).astype(F32)
    live = lax.broadcasted_iota(jnp.int32, (c, 1), 0) < valid_rows

    chains = []
    for h in range(heads):
        log_gamma = math.log(1.0 - 2.0 ** (-5.0 - h))
        dmat = jnp.where(rel >= 0, jnp.exp(jnp.maximum(rel, 0.0) * log_gamma), 0.0)
        q_scale = jnp.exp((idx + 1.0) * log_gamma)
        k_scale = jnp.exp((valid_rows - 1.0 - idx) * log_gamma)
        for i in range(nb):
            q = q_ref[i, :, h * dk:(h + 1) * dk]
            k = k_ref[i, :, h * dk:(h + 1) * dk]
            v = v_ref[i, :, h * dv:(h + 1) * dv]
            if valid_rows < c:
                v = jnp.where(live, v, 0.0)
            q = q * cos2 + pltpu.roll(q, half, 1) * sin2
            k = (k * cos2 + pltpu.roll(k, half, 1) * sin2) * (dk ** -0.5)
            chains.append(dict(
                i=i, h=h, dmat=dmat, s_decay=math.exp(valid_rows * log_gamma),
                q_bf=q.astype(BF16), k_bf=k.astype(BF16), v_bf=v.astype(BF16),
                q_dec=(q * q_scale).astype(BF16), k_dec=(k * k_scale).astype(BF16)))

    for ch in chains:
        ch["inner"] = (_dot_nt(ch["q_bf"], ch["k_bf"]) * ch["dmat"]).astype(BF16)
    for ch in chains:
        s = s_scr[ch["i"], ch["h"]]
        ch["o"] = _dot(ch["q_dec"], s.astype(BF16)) + _dot(ch["inner"], ch["v_bf"])
        s_scr[ch["i"], ch["h"]] = s * ch["s_decay"] + _dot_tn(ch["k_dec"], ch["v_bf"])
    for ch in chains:
        ch["mu"] = jnp.mean(ch["o"], axis=-1, keepdims=True)
    for ch in chains:
        ch["oc"] = ch["o"] - ch["mu"]
        ch["var"] = jnp.mean(ch["oc"] * ch["oc"], axis=-1, keepdims=True)
    for ch in chains:
        i, h = ch["i"], ch["h"]
        o = ch["oc"] * lax.rsqrt(ch["var"] + NORM_EPS)
        o = o * gnw_ref[:, h * dv:(h + 1) * dv] + gnb_ref[:, h * dv:(h + 1) * dv]
        o = o * _silu(g_ref[i, :, h * dv:(h + 1) * dv])
        o_ref[i, :, h * dv:(h + 1) * dv] = o.astype(o_ref.dtype)

    @pl.when(n == pl.num_programs(1) - 1)
    def _():
        snew_ref[...] = s_scr[...]


def _ret(proj, col0, s0, gn_w, gn_b, *, chunk, valid_rows, nb, pos0, heads, dk, dv):
    b, t, _ = proj.shape
    width = heads * dv
    n_chunks = t // chunk
    blk0 = col0 // width
    half = dk // 2
    inv_freq = ROPE_BASE ** (-jnp.arange(half, dtype=F32) / half)
    inv_freq2 = jnp.concatenate([inv_freq, inv_freq]).reshape(1, dk)
    col_spec = lambda kk: pl.BlockSpec((nb, chunk, width), lambda i, n: (i, n, blk0 + kk))
    return pl.pallas_call(
        functools.partial(_ret_kernel, valid_rows=valid_rows, pos0=pos0, heads=heads, dk=dk, dv=dv),
        grid=(b // nb, n_chunks),
        in_specs=[
            col_spec(0), col_spec(1), col_spec(2), col_spec(3),
            pl.BlockSpec((nb, heads, dk, dv), lambda i, n: (i, 0, 0, 0)),
            pl.BlockSpec((1, dk), lambda i, n: (0, 0)),
            pl.BlockSpec((1, width), lambda i, n: (0, 0)),
            pl.BlockSpec((1, width), lambda i, n: (0, 0)),
        ],
        out_specs=[
            pl.BlockSpec((nb, chunk, width), lambda i, n: (i, n, 0)),
            pl.BlockSpec((nb, heads, dk, dv), lambda i, n: (i, 0, 0, 0)),
        ],
        out_shape=[
            jax.ShapeDtypeStruct((b, t, width), BF16),
            jax.ShapeDtypeStruct((b, heads, dk, dv), F32),
        ],
        scratch_shapes=[pltpu.VMEM((nb, heads, dk, dv), F32)],
        compiler_params=_params("parallel", "arbitrary"),
        name="ret_mixer",
    )(proj, proj, proj, proj, s0, inv_freq2, gn_w.reshape(1, width), gn_b.reshape(1, width))


def _trunk(x, mod, mod_f, rows_per_group, seq_rows, valid_rows, nb, pos0, to_seq, from_seq,
           conv_init, s_gdn0, s_ret0, w, tm, tf):
    heads, dk, dv = s_gdn0.shape[1:]
    x1, h2 = _ffn(x, mod, (0, 1, 2), w["norm_ffn1"], w["w1_gate"], w["w1_up"], w["w1_down"],
                  mod, (3, 4), w["norm_mix"], rows_per_group=rows_per_group, tm=tm, tf=tf,
                  emit_x=True, hn_dtype=BF16)
    conv_ch = heads * (2 * dk + dv)
    width = heads * dv
    ba0 = conv_ch + width
    tn = PROJ_COLS
    proj_ba = to_seq(_proj(h2, w["w_in_t"], lambda j: ba0, LANES, tm, LANES))
    chunk = min(CHUNK, seq_rows)
    nb_gdn, nb_ret = nb
    gdn_args = dict(chunk=chunk, valid_rows=valid_rows, nb=nb_gdn, heads=heads, dk=dk, dv=dv)
    if conv_init is None:
        qkv, conv_new = _proj_conv(h2, w["w_in_t"], w["conv_w"], conv_ch, seq_rows, tm, tn)
        skip = lambda j: conv_ch + j * tn + jnp.where(j > 0, 2 * heads, 0)
        rest = to_seq(_proj(h2, w["w_in_t"], skip, 5 * width, tm, tn))
        o_gdn, s_gdn = _gdn((to_seq(qkv), 0), (rest, 0), proj_ba, s_gdn0, w["a_log"], w["dt_bias"],
                            w["gdn_norm_w"], **gdn_args)
        ret_src, ret_col = rest, width
    else:
        skip = lambda j: j * tn + jnp.where(j * tn >= ba0, 2 * heads, 0)
        proj = to_seq(_proj(h2, w["w_in_t"], skip, ba0 + 4 * width, tm, tn))
        o_gdn, s_gdn, conv_new = _gdn((proj, 0), (proj, conv_ch), proj_ba, s_gdn0, w["a_log"], w["dt_bias"],
                                      w["gdn_norm_w"], (conv_init, w["conv_w"]), **gdn_args)
        ret_src, ret_col = proj, ba0
    o_ret, s_ret = _ret(ret_src, ret_col, s_ret0, w["ret_gn_w"], w["ret_gn_b"],
                        chunk=chunk, valid_rows=valid_rows, nb=nb_ret, pos0=pos0, heads=heads, dk=dk, dv=dv)
    o_gdn = from_seq(o_gdn)
    o_ret = from_seq(o_ret)
    x2 = _outproj(x1, mod, 5, o_gdn, o_ret, w["w_out"], rows_per_group=rows_per_group, tm=min(tm, OUTPROJ_ROWS))
    (y,) = _ffn(x2, mod, (6, 7, 8), w["norm_ffn2"], w["w2_gate"], w["w2_up"], w["w2_down"],
                mod_f, (0, 1), w["norm_final"], rows_per_group=rows_per_group, tm=tm, tf=tf,
                emit_x=False, hn_dtype=F32)
    return y, s_gdn, conv_new[:, SUBLANES - (CONV_WIDTH - 1):, :], s_ret


def kernel(x_prompt, x_sample, state_gdn, state_conv, state_ret, c_prompt, c_sample, w_ada, b_ada, norm_ffn1, w1_gate, w1_up, w1_down, norm_mix, w_in, conv_w, a_log, dt_bias, gdn_norm_w, ret_gn_w, ret_gn_b, w_out, norm_ffn2, w2_gate, w2_up, w2_down, w_ada_final, b_ada_final, norm_final):
    bp, tp, d = x_prompt.shape
    bs, ts, _ = x_sample.shape
    depth, _, heads, dk, dv = state_gdn.shape
    assert depth == 1, "single-layer trunk"
    assert bs == ROW_TILE, "time-major sample rows must align adaLN vectors with row tiles"
    conv_ch = state_conv.shape[-1]
    width = heads * dv
    ba0 = conv_ch + width
    ret0 = ba0 + 2 * heads

    n_c = bs + bp
    n_c_pad = -(-n_c // SUBLANES) * SUBLANES
    c_all = jnp.concatenate([c_sample, c_prompt, jnp.zeros((n_c_pad - n_c, d), F32)], axis=0)
    ada = _ada_proj(c_all, w_ada[0], b_ada[0], tn=ADA_COLS)
    ada_f = _ada_proj(c_all, w_ada_final, b_ada_final, tn=ADA_COLS)
    mod_s, mod_p = ada, ada[bs:bs + bp].reshape(bp, 1, N_ADA * d)
    modf_s, modf_p = ada_f, ada_f[bs:bs + bp].reshape(bp, 1, 2 * d)

    assert ba0 % PROJ_COLS == 0 and ret0 % SUBLANES == 0 and w_in.shape[-1] == ret0 + 4 * width
    w = dict(norm_ffn1=norm_ffn1[0], w1_gate=w1_gate[0], w1_up=w1_up[0], w1_down=w1_down[0],
             norm_mix=norm_mix[0], w_in_t=w_in[0].T, conv_w=conv_w[0], a_log=a_log[0],
             dt_bias=dt_bias[0], gdn_norm_w=gdn_norm_w[0], ret_gn_w=ret_gn_w[0], ret_gn_b=ret_gn_b[0],
             w_out=w_out[0], norm_ffn2=norm_ffn2[0], w2_gate=w2_gate[0], w2_up=w2_up[0],
             w2_down=w2_down[0], norm_final=norm_final)

    zeros_state = jnp.zeros((bp, heads, dk, dv), F32)
    y_p, gdn_p, conv_p, ret_p = _trunk(
        x_prompt.reshape(bp * tp, d), mod_p, modf_p, tp, tp, min(CHUNK, tp), (bp, SEQS_PER_STEP_FULL), 0,
        lambda a: a.reshape(bp, tp, a.shape[-1]), lambda a: a.reshape(bp * tp, a.shape[-1]),
        None, zeros_state, zeros_state, w, tm=FFN_ROWS, tf=FFN_COLS)

    ts_pad = -(-ts // SUBLANES) * SUBLANES
    to_seq = lambda a: jnp.pad(a.reshape(ts, bs, a.shape[-1]).transpose(1, 0, 2),
                               ((0, 0), (0, ts_pad - ts), (0, 0)))
    from_seq = lambda a: a[:, :ts].transpose(1, 0, 2).reshape(ts * bs, a.shape[-1])
    conv_init = jnp.pad(state_conv[0], ((0, 0), (SUBLANES - (CONV_WIDTH - 1), 0), (0, 0)))
    y_s, gdn_s, conv_s, ret_s = _trunk(
        x_sample.transpose(1, 0, 2).reshape(ts * bs, d), mod_s, modf_s, None, ts_pad, ts,
        (SEQS_PER_STEP_SHORT, SEQS_PER_STEP_SHORT), PAST_LEN,
        to_seq, from_seq, conv_init, state_gdn[0], state_ret[0], w, tm=ts * bs, tf=FFN_COLS_SMALL)
    y_s = y_s.reshape(ts, bs, d).transpose(1, 0, 2)

    return (y_p.reshape(bp, tp, d), y_s, gdn_p[None], conv_p[None], ret_p[None],
            gdn_s[None], conv_s[None], ret_s[None])
```

```python
import functools
import math

import jax
import jax.numpy as jnp
from jax import lax
from jax.experimental import pallas as pl
from jax.experimental.pallas import tpu as pltpu

F32 = jnp.float32
BF16 = jnp.bfloat16

LANES = 128
SUBLANES = 8
MXU_COLS = 256
VMEM_LIMIT_BYTES = 60 * 1024 * 1024

PAST_LEN = 16384
CONV_WIDTH = 4
CHUNK = 64
ROPE_BASE = 10000.0
NORM_EPS = 1e-6
L2_EPS = 1e-6
MACARON_WEIGHT = 0.5
N_ADA = 9
FFN_ROWS = 1024
FFN_COLS = 256
FFN_COLS_SMALL = 512
OUTPROJ_ROWS = 512
ADA_COLS = 512
SEQS_PER_STEP_FULL = 4
SEQS_PER_STEP_SHORT = 8
ROW_TILE = 128
NORM_ROWS = 16
FFN_SUB_ROWS = 512
PROJ_COLS = 1024
HISTORY_ROWS = SUBLANES
CONV_SUB_ROWS = 256


def _sigmoid(x):
    return 1.0 / (1.0 + jnp.exp(-x))


def _silu(x):
    return x * _sigmoid(x)


def _softplus(x):
    return jnp.maximum(x, 0.0) + jnp.log(1.0 + jnp.exp(-jnp.abs(x)))


def _dot(a, b):
    return jnp.dot(a, b, preferred_element_type=F32)


def _dot_nt(a, b):
    return lax.dot_general(a, b, (((1,), (1,)), ((), ())), preferred_element_type=F32)


def _dot_tn(a, b):
    return lax.dot_general(a, b, (((0,), (0,)), ((), ())), preferred_element_type=F32)


def _split3(x):
    hi = x.astype(BF16)
    r = x - hi.astype(F32)
    mid = r.astype(BF16)
    lo = (r - mid.astype(F32)).astype(BF16)
    return hi, mid, lo


def _rms_mod(x, gain, shift, scale):
    y = x * lax.rsqrt(jnp.mean(x * x, axis=-1, keepdims=True) + NORM_EPS)
    return (y * gain) * (1.0 + scale) + shift


def _mod_rows(ref, g):
    if ref.shape[0] == 1:
        return ref[...]
    return ref[g * NORM_ROWS:(g + 1) * NORM_ROWS, :]


def _for_row_groups(n_rows, fn):
    def tile(t, carry):
        base = pl.multiple_of(t * ROW_TILE, ROW_TILE)
        for g in range(ROW_TILE // NORM_ROWS):
            fn(pl.ds(base + g * NORM_ROWS, NORM_ROWS), g)
        return carry
    lax.fori_loop(0, n_rows // ROW_TILE, tile, 0)


def _params(*sem):
    return pltpu.CompilerParams(dimension_semantics=sem, vmem_limit_bytes=VMEM_LIMIT_BYTES)


def _ada_kernel(c_ref, w_ref, b_ref, o_ref):
    a = _silu(c_ref[...]).astype(BF16)
    o_ref[...] = _dot(a, w_ref[...].astype(BF16)) + b_ref[...]


def _ada_proj(c, w, b, tn):
    m, d = c.shape
    n = w.shape[1]
    return pl.pallas_call(
        _ada_kernel,
        grid=(n // tn,),
        in_specs=[
            pl.BlockSpec((m, d), lambda j: (0, 0)),
            pl.BlockSpec((d, tn), lambda j: (0, j)),
            pl.BlockSpec((1, tn), lambda j: (0, j)),
        ],
        out_specs=pl.BlockSpec((m, tn), lambda j: (0, j)),
        out_shape=jax.ShapeDtypeStruct((m, n), F32),
        compiler_params=_params("parallel"),
        name="ada_proj",
    )(c, w, b.reshape(1, n))


def _ffn_kernel(x_ref, sh_ref, sc_ref, gt_ref, gain_ref, wg_ref, wu_ref, wd_ref,
                gain2_ref, sh2_ref, sc2_ref, *rest, emit_x):
    if emit_x:
        xo_ref, hn_hbm, h_scr, stage_scr, stage_sem = rest
        acc_ref = xo_ref
    else:
        hn_hbm, h_scr, stage_scr, stage_sem, acc_ref = rest
    j = pl.program_id(1)
    tm = x_ref.shape[0]

    @pl.when(j == 0)
    def _():
        def group(rows, g):
            h = _rms_mod(x_ref[rows, :], gain_ref[...], _mod_rows(sh_ref, g), _mod_rows(sc_ref, g))
            h_scr[rows, :] = h.astype(BF16)
            acc_ref[rows, :] = jnp.zeros((NORM_ROWS, acc_ref.shape[1]), F32)
        _for_row_groups(tm, group)

    wg = wg_ref[...].astype(BF16)
    wu = wu_ref[...].astype(BF16)
    wd = wd_ref[...].astype(BF16)
    sub = min(tm, FFN_SUB_ROWS)
    for r in range(tm // sub):
        rows = slice(r * sub, (r + 1) * sub)
        h = h_scr[rows, :]
        a = (_silu(_dot(h, wg)) * _dot(h, wu)).astype(BF16)
        acc_ref[rows, :] += _dot(a, wd)

    @pl.when(j == pl.num_programs(1) - 1)
    def _():
        i = pl.program_id(0)
        n_tiles = tm // ROW_TILE

        def hn_copy(t, block):
            row = pl.multiple_of(block * tm + t * ROW_TILE, ROW_TILE)
            return pltpu.make_async_copy(stage_scr.at[t], hn_hbm.at[pl.ds(row, ROW_TILE), :], stage_sem.at[t])

        def wait_block(block):
            for t in range(n_tiles):
                hn_copy(t, block).wait()

        @pl.when(i > 0)
        def _():
            wait_block(i - 1)

        def tile(t, carry):
            base = pl.multiple_of(t * ROW_TILE, ROW_TILE)
            for g in range(ROW_TILE // NORM_ROWS):
                rows = pl.ds(base + g * NORM_ROWS, NORM_ROWS)
                xo = x_ref[rows, :] + (MACARON_WEIGHT * _mod_rows(gt_ref, g)) * acc_ref[rows, :]
                if emit_x:
                    xo_ref[rows, :] = xo
                hn = _rms_mod(xo, gain2_ref[...], _mod_rows(sh2_ref, g), _mod_rows(sc2_ref, g))
                stage_scr[t, g * NORM_ROWS:(g + 1) * NORM_ROWS, :] = hn.astype(stage_scr.dtype)
            hn_copy(t, i).start()
            return carry
        lax.fori_loop(0, n_tiles, tile, 0)

        @pl.when(i == pl.num_programs(0) - 1)
        def _():
            wait_block(i)


def _mod_spec(mod, k, d, rows_per_group, tm):
    if mod.ndim == 3:
        blocks_per_group = rows_per_group // tm
        return pl.BlockSpec((None, 1, d), lambda i, j: (i // blocks_per_group, 0, k))
    return pl.BlockSpec((ROW_TILE, d), lambda i, j: (0, k))


def _ffn(x, mod, ks, gain, wg, wu, wd, mod2, ks2, gain2, *, rows_per_group, tm, tf, emit_x, hn_dtype):
    m, d = x.shape
    f = wg.shape[1]
    assert m % tm == 0 and f % tf == 0 and tm % ROW_TILE == 0 and d % LANES == 0
    assert mod.ndim == 2 or rows_per_group % tm == 0
    k_sh, k_sc, k_gt = ks
    k_sh2, k_sc2 = ks2
    vec = lambda: pl.BlockSpec((1, d), lambda i, j: (0, 0))
    row_block = lambda: pl.BlockSpec((tm, d), lambda i, j: (i, 0))
    out_shape = [jax.ShapeDtypeStruct((m, d), hn_dtype)]
    out_specs = [pl.BlockSpec(memory_space=pl.ANY)]
    if emit_x:
        out_shape = [jax.ShapeDtypeStruct((m, d), F32)] + out_shape
        out_specs = [row_block()] + out_specs
    return pl.pallas_call(
        functools.partial(_ffn_kernel, emit_x=emit_x),
        grid=(m // tm, f // tf),
        in_specs=[
            row_block(),
            _mod_spec(mod, k_sh, d, rows_per_group, tm),
            _mod_spec(mod, k_sc, d, rows_per_group, tm),
            _mod_spec(mod, k_gt, d, rows_per_group, tm),
            vec(),
            pl.BlockSpec((d, tf), lambda i, j: (0, j)),
            pl.BlockSpec((d, tf), lambda i, j: (0, j)),
            pl.BlockSpec((tf, d), lambda i, j: (j, 0)),
            vec(),
            _mod_spec(mod2, k_sh2, d, rows_per_group, tm),
            _mod_spec(mod2, k_sc2, d, rows_per_group, tm),
        ],
        out_specs=out_specs,
        out_shape=out_shape,
        scratch_shapes=[pltpu.VMEM((tm, d), BF16), pltpu.VMEM((tm // ROW_TILE, ROW_TILE, d), hn_dtype),
                        pltpu.SemaphoreType.DMA((tm // ROW_TILE,))]
        + ([] if emit_x else [pltpu.VMEM((tm, d), F32)]),
        compiler_params=_params("arbitrary", "arbitrary"),
        name="ffn",
    )(x, mod, mod, mod, gain.reshape(1, d), wg, wu, wd, gain2.reshape(1, d), mod2, mod2)


def _proj_kernel(h_ref, wt_ref, o_ref, wb_scr):
    @pl.when(pl.program_id(1) == 0)
    def _():
        wb_scr[...] = wt_ref[...].astype(BF16)

    o_ref[...] = _dot_nt(h_ref[...], wb_scr[...])


def _proj(h, wt, row_of, n_cols, tm, tn):
    m, d = h.shape
    return pl.pallas_call(
        _proj_kernel,
        grid=(n_cols // tn, m // tm),
        in_specs=[
            pl.BlockSpec((tm, d), lambda j, i: (i, 0)),
            pl.BlockSpec((pl.Element(tn), pl.Element(d)),
                         lambda j, i: (pl.multiple_of(row_of(j), SUBLANES), 0)),
        ],
        out_specs=pl.BlockSpec((tm, tn), lambda j, i: (i, j)),
        out_shape=jax.ShapeDtypeStruct((m, n_cols), F32),
        scratch_shapes=[pltpu.VMEM((tn, d), BF16)],
        compiler_params=_params("parallel", "arbitrary"),
        name="in_proj",
    )(h, wt)


def _proj_conv_kernel(h_ref, wt_ref, convw_ref, wside_ref, o_ref, cnew_ref, side_hbm,
                      wb_scr, tail_scr, side_scr, side_sem, *, blocks_per_seq):
    j, i = pl.program_id(0), pl.program_id(1)
    n_i = pl.num_programs(1)
    tm_rows = h_ref.shape[0]

    def side_copy(blk):
        rows = pl.ds(pl.multiple_of(blk * tm_rows, tm_rows), tm_rows)
        return pltpu.make_async_copy(side_scr.at[blk], side_hbm.at[rows, :], side_sem.at[blk])

    @pl.when(j == 0)
    def _():
        side_scr[i] = _dot_nt(h_ref[...], wside_ref[...].astype(BF16))
        side_copy(i).start()

    @pl.when(i == 0)
    def _():
        wb_scr[...] = wt_ref[...].astype(BF16)

    h = h_ref[...]
    tm, tn = o_ref.shape
    first = i % blocks_per_seq == 0
    for cg in range(tn // MXU_COLS):
        cols = slice(cg * MXU_COLS, (cg + 1) * MXU_COLS)
        raw = _dot_nt(h, wb_scr[cols, :])
        w = convw_ref[:, cols]
        prev = jnp.where(first, 0.0, tail_scr[:, cols])
        for r in range(tm // CONV_SUB_ROWS):
            rows = slice(r * CONV_SUB_ROWS, (r + 1) * CONV_SUB_ROWS)
            x = raw[rows, :]
            y, _ = _causal_conv_silu(prev, x, w)
            o_ref[rows, cols] = y
            prev = x[CONV_SUB_ROWS - HISTORY_ROWS:, :]
        tail_scr[:, cols] = prev
        cnew_ref[:, cols] = prev

    @pl.when((j == pl.num_programs(0) - 1) & (i == n_i - 1))
    def _():
        for blk in range(side_scr.shape[0]):
            side_copy(blk).wait()


def _proj_conv(h, wt, conv_w, n_cols, side_row0, rows_per_seq, tm, tn):
    m, d = h.shape
    blocks_per_seq = rows_per_seq // tm
    assert side_row0 % LANES == 0
    return pl.pallas_call(
        functools.partial(_proj_conv_kernel, blocks_per_seq=blocks_per_seq),
        grid=(n_cols // tn, m // tm),
        in_specs=[
            pl.BlockSpec((tm, d), lambda j, i: (i, 0)),
            pl.BlockSpec((tn, d), lambda j, i: (j, 0)),
            pl.BlockSpec((CONV_WIDTH, tn), lambda j, i: (0, j)),
            pl.BlockSpec((LANES, d), lambda j, i: (side_row0 // LANES, 0)),
        ],
        out_specs=[
            pl.BlockSpec((tm, tn), lambda j, i: (i, j)),
            pl.BlockSpec((None, HISTORY_ROWS, tn), lambda j, i: (i // blocks_per_seq, 0, j)),
            pl.BlockSpec(memory_space=pl.ANY),
        ],
        out_shape=[
            jax.ShapeDtypeStruct((m, n_cols), F32),
            jax.ShapeDtypeStruct((m // rows_per_seq, HISTORY_ROWS, n_cols), F32),
            jax.ShapeDtypeStruct((m, LANES), F32),
        ],
        scratch_shapes=[pltpu.VMEM((tn, d), BF16), pltpu.VMEM((HISTORY_ROWS, tn), F32),
                        pltpu.VMEM((m // tm, tm, LANES), F32), pltpu.SemaphoreType.DMA((m // tm,))],
        compiler_params=_params("arbitrary", "arbitrary"),
        name="in_proj_conv",
    )(h, wt, conv_w, wt)


def _outproj_kernel(x_ref, gt_ref, og_ref, or_ref, w_ref, o_ref, wb_scr):
    @pl.when(pl.program_id(0) == 0)
    def _():
        wb_scr[...] = w_ref[...].astype(BF16)

    kw = og_ref.shape[1]
    tm = x_ref.shape[0]
    mixed = _dot(og_ref[...], wb_scr[:kw, :]) + _dot(or_ref[...], wb_scr[kw:, :])
    for r in range(tm // ROW_TILE):
        rows = slice(r * ROW_TILE, (r + 1) * ROW_TILE)
        o_ref[rows, :] = x_ref[rows, :] + gt_ref[...] * mixed[rows, :]


def _outproj(x, mod, k_gt, og, orr, w_out, *, rows_per_group, tm):
    m, d = x.shape
    kw = og.shape[1]
    if mod.ndim == 3:
        blocks_per_group = rows_per_group // tm
        gt_spec = pl.BlockSpec((None, 1, d), lambda i: (i // blocks_per_group, 0, k_gt))
    else:
        gt_spec = pl.BlockSpec((ROW_TILE, d), lambda i: (0, k_gt))
    return pl.pallas_call(
        _outproj_kernel,
        grid=(m // tm,),
        in_specs=[
            pl.BlockSpec((tm, d), lambda i: (i, 0)),
            gt_spec,
            pl.BlockSpec((tm, kw), lambda i: (i, 0)),
            pl.BlockSpec((tm, kw), lambda i: (i, 0)),
            pl.BlockSpec(w_out.shape, lambda i: (0, 0), pipeline_mode=pl.Buffered(1)),
        ],
        out_specs=pl.BlockSpec((tm, d), lambda i: (i, 0)),
        out_shape=jax.ShapeDtypeStruct((m, d), F32),
        scratch_shapes=[pltpu.VMEM(w_out.shape, BF16)],
        compiler_params=_params("arbitrary"),
        name="out_proj",
    )(x, mod, og, orr, w_out)


SOLVE_BLOCK = 16


def _nilpotent_apply(chains, index):
    power = 1
    while power < index:
        square = 2 * power < index
        for ch in chains:
            right = jnp.concatenate([ch["sol"], ch["nil"]], axis=1) if square else ch["sol"]
            ch["prod"] = _dot(ch["nil"].astype(BF16), right.astype(BF16))
        for ch in chains:
            width = ch["sol"].shape[1]
            upd = ch["prod"][:, :width]
            ch["sol"] = ch["sol"] - upd if power == 1 else ch["sol"] + upd
            if square:
                ch["nil"] = ch["prod"][:, width:]
        power *= 2


def _causal_conv_silu(prev, x, w):
    xp = jnp.concatenate([prev, x], axis=0)
    acc = x * w[CONV_WIDTH - 1:CONV_WIDTH, :]
    for sft in range(1, CONV_WIDTH):
        tap = CONV_WIDTH - 1 - sft
        acc = acc + pltpu.roll(xp, sft, 0)[HISTORY_ROWS:, :] * w[tap:tap + 1, :]
    return _silu(acc), xp


def _gdn_kernel(*refs, valid_rows, heads, dk, dv, conv_here):
    if conv_here:
        (qkv_ref, z_ref, ba_ref, s0_ref, alog_ref, dtb_ref, nw_ref, cinit_ref, convw_ref,
         o_ref, snew_ref, cnew_ref, s_scr, tail_scr) = refs
    else:
        qkv_ref, z_ref, ba_ref, s0_ref, alog_ref, dtb_ref, nw_ref, o_ref, snew_ref, s_scr = refs
    n = pl.program_id(1)
    last = pl.num_programs(1) - 1
    nb, c, _ = qkv_ref.shape

    @pl.when(n == 0)
    def _():
        s_scr[...] = s0_ref[...]
        if conv_here:
            tail_scr[...] = cinit_ref[...]

    row = lax.broadcasted_iota(jnp.int32, (c, c), 0)
    col = lax.broadcasted_iota(jnp.int32, (c, c), 1)
    causal = row >= col
    strict = row > col
    diag = row == col
    tril = jnp.where(causal, 1.0, 0.0).astype(BF16)
    nw = nw_ref[...]
    neg_a = -jnp.exp(alog_ref[...])
    dtb = dtb_ref[...]

    gates = []
    for i in range(nb):
        ba = ba_ref[i]
        beta_all = _sigmoid(ba)
        g_all = neg_a * _softplus(ba + dtb)
        if valid_rows < c:
            live = lax.broadcasted_iota(jnp.int32, ba.shape, 0) < valid_rows
            beta_all = jnp.where(live, beta_all, 0.0)
            g_all = jnp.where(live, g_all, 0.0)
        g_hi, g_mid, g_lo = _split3(g_all)
        gc_all = _dot(tril, g_hi) + (_dot(tril, g_mid) + _dot(tril, g_lo))
        gc_last_all = gc_all[c - 1:c, :]
        gates.append(dict(beta=beta_all, gc=gc_all, eg=jnp.exp(gc_all),
                          kdec=jnp.exp(gc_last_all - gc_all), gl=jnp.exp(gc_last_all)))

    chains = []
    for i in range(nb):
        if conv_here:
            x = qkv_ref[i]
            qkv, xp = _causal_conv_silu(tail_scr[i], x, convw_ref[...])
            tail_scr[i] = x[c - HISTORY_ROWS:, :]

            @pl.when(n == last)
            def _(i=i, xp=xp):
                cnew_ref[i] = xp[valid_rows:valid_rows + HISTORY_ROWS, :]
        else:
            qkv = qkv_ref[i]
        for h in range(heads):
            q = qkv[:, h * dk:(h + 1) * dk]
            k = qkv[:, heads * dk + h * dk:heads * dk + (h + 1) * dk]
            v = qkv[:, 2 * heads * dk + h * dv:2 * heads * dk + (h + 1) * dv]
            q = q * (lax.rsqrt(jnp.sum(q * q, axis=-1, keepdims=True) + L2_EPS) * (dk ** -0.5))
            k = k * lax.rsqrt(jnp.sum(k * k, axis=-1, keepdims=True) + L2_EPS)
            chains.append(dict(i=i, h=h, q=q, k=k, v=v, k_bf=k.astype(BF16)))

    for ch in chains:
        gate, lane, q, k = gates[ch["i"]], heads + ch["h"], ch["q"], ch["k"]
        beta = gate["beta"][:, ch["h"]:ch["h"] + 1]
        eg = gate["eg"][:, lane:lane + 1]
        kb = k * beta
        gc_col = jnp.broadcast_to(gate["gc"][:, lane:lane + 1], (c, c))
        gc_row = jnp.sum(jnp.where(diag, gc_col, 0.0), axis=0, keepdims=True)
        ch.update(
            decay=jnp.where(causal, jnp.exp(gc_col - gc_row), 0.0),
            kbq=jnp.concatenate([kb, q], axis=0).astype(BF16),
            rhs=jnp.concatenate([ch["v"] * beta, kb * eg], axis=1),
            q_dec=q * eg,
            k_dec=(k * gate["kdec"][:, lane:lane + 1]).astype(BF16),
            gl=gate["gl"][:, lane:lane + 1])

    for ch in chains:
        ch["kq"] = _dot_nt(ch["kbq"], ch["k_bf"])
    for ch in chains:
        ch["a"] = jnp.where(strict, ch["kq"][:c] * ch["decay"], 0.0)
        ch["qk"] = (ch["kq"][c:] * ch["decay"]).astype(BF16)

    if valid_rows <= SOLVE_BLOCK:
        for ch in chains:
            ch["nil"], ch["sol"] = ch["a"], ch["rhs"]
        _nilpotent_apply(chains, valid_rows)
    else:
        same_block = (row // SOLVE_BLOCK) == (col // SOLVE_BLOCK)
        eye = jnp.where(diag, 1.0, 0.0)
        for ch in chains:
            ch["nil"] = jnp.where(same_block, ch["a"], 0.0)
            ch["sol"] = eye
        _nilpotent_apply(chains, SOLVE_BLOCK)
        for ch in chains:
            off_block = jnp.where(same_block, 0.0, ch["a"])
            right = jnp.concatenate([ch["rhs"], off_block], axis=1)
            ch["prod"] = _dot(ch["sol"].astype(BF16), right.astype(BF16))
        for ch in chains:
            ch["sol"] = ch["prod"][:, :dv + dk]
            ch["nil"] = ch["prod"][:, dv + dk:]
        _nilpotent_apply(chains, c // SOLVE_BLOCK)

    for ch in chains:
        ch["s"] = s_scr[ch["i"], ch["h"]]
        lhs = jnp.concatenate([ch["sol"][:, dv:], ch["q_dec"]], axis=0).astype(BF16)
        ch["ws_qs"] = _dot(lhs, ch["s"].astype(BF16))
    for ch in chains:
        ch["v_new"] = (ch["sol"][:, :dv] - ch["ws_qs"][:c]).astype(BF16)
    for ch in chains:
        s_scr[ch["i"], ch["h"]] = ch["s"] * ch["gl"] + _dot_tn(ch["k_dec"], ch["v_new"])
        ch["o"] = ch["ws_qs"][c:] + _dot(ch["qk"], ch["v_new"])
    for ch in chains:
        ch["ms"] = jnp.mean(ch["o"] * ch["o"], axis=-1, keepdims=True)
    for ch in chains:
        i, h, o = ch["i"], ch["h"], ch["o"]
        o = o * lax.rsqrt(ch["ms"] + NORM_EPS) * nw
        o = o * _silu(z_ref[i, :, h * dv:(h + 1) * dv])
        o_ref[i, :, h * dv:(h + 1) * dv] = o.astype(o_ref.dtype)

    @pl.when(n == last)
    def _():
        snew_ref[...] = s_scr[...]


def _gdn(qkv_src, z_src, proj_ba, s0, a_log, dt_bias, norm_w, conv=None, *, chunk, valid_rows, nb, heads, dk, dv):
    (qkv_arr, qkv_col), (z_arr, z_col) = qkv_src, z_src
    b, t, _ = qkv_arr.shape
    assert valid_rows <= SOLVE_BLOCK or (valid_rows == chunk and chunk % SOLVE_BLOCK == 0)
    conv_ch = heads * (2 * dk + dv)
    width = heads * dv
    n_chunks = t // chunk
    qkv_blk = qkv_col // conv_ch
    z_blk = z_col // width
    lane_vec = lambda x: jnp.zeros((1, LANES), F32).at[0, heads:2 * heads].set(x.astype(F32))
    state_spec = pl.BlockSpec((nb, heads, dk, dv), lambda i, n: (i, 0, 0, 0))
    hist_spec = pl.BlockSpec((nb, HISTORY_ROWS, conv_ch), lambda i, n: (i, 0, 0))
    lane_spec = pl.BlockSpec((1, LANES), lambda i, n: (0, 0))
    in_specs = [
        pl.BlockSpec((nb, chunk, conv_ch), lambda i, n: (i, n, qkv_blk)),
        pl.BlockSpec((nb, chunk, width), lambda i, n: (i, n, z_blk)),
        pl.BlockSpec((nb, chunk, LANES), lambda i, n: (i, n, 0)),
        state_spec, lane_spec, lane_spec,
        pl.BlockSpec((1, dv), lambda i, n: (0, 0)),
    ]
    args = [qkv_arr, z_arr, proj_ba, s0, lane_vec(a_log), lane_vec(dt_bias), norm_w.reshape(1, dv)]
    out_specs = [pl.BlockSpec((nb, chunk, width), lambda i, n: (i, n, 0)), state_spec]
    out_shape = [jax.ShapeDtypeStruct((b, t, width), BF16), jax.ShapeDtypeStruct((b, heads, dk, dv), F32)]
    scratch = [pltpu.VMEM((nb, heads, dk, dv), F32)]
    if conv is not None:
        conv_init, conv_w = conv
        in_specs += [hist_spec, pl.BlockSpec((CONV_WIDTH, conv_ch), lambda i, n: (0, 0))]
        args += [conv_init, conv_w]
        out_specs.append(hist_spec)
        out_shape.append(jax.ShapeDtypeStruct((b, HISTORY_ROWS, conv_ch), F32))
        scratch.append(pltpu.VMEM((nb, HISTORY_ROWS, conv_ch), F32))
    return pl.pallas_call(
        functools.partial(_gdn_kernel, valid_rows=valid_rows, heads=heads, dk=dk, dv=dv,
                          conv_here=conv is not None),
        grid=(b // nb, n_chunks),
        in_specs=in_specs,
        out_specs=out_specs,
        out_shape=out_shape,
        scratch_shapes=scratch,
        compiler_params=_params("parallel", "arbitrary"),
        name="gdn_mixer",
    )(*args)


def _ret_kernel(q_ref, k_ref, v_ref, g_ref, s0_ref, invf_ref, gnw_ref, gnb_ref,
                o_ref, snew_ref, s_scr, *, valid_rows, pos0, heads, dk, dv):
    n = pl.program_id(1)
    nb, c, _ = q_ref.shape
    half = dk // 2

    @pl.when(n == 0)
    def _():
        s_scr[...] = s0_ref[...]

    t_idx = lax.broadcasted_iota(jnp.int32, (c, dk), 0)
    pos = (pos0 + n * c + t_idx).astype(F32)
    ang = pos * invf_ref[...]
    cos2 = jnp.cos(ang)
    lane = lax.broadcasted_iota(jnp.int32, (c, dk), 1)
    sin2 = jnp.where(lane < half, -jnp.sin(ang), jnp.sin(ang))

    row = lax.broadcasted_iota(jnp.int32, (c, c), 0)
    col = lax.broadcasted_iota(jnp.int32, (c, c), 1)
    rel = (row - col).astype(F32)
    idx = lax.broadcasted_iota(jnp.int32, (c, 1), 0).astype(F32)
    live = lax.broadcasted_iota(jnp.int32, (c, 1), 0) < valid_rows

    chains = []
    for h in range(heads):
        log_gamma = math.log(1.0 - 2.0 ** (-5.0 - h))
        dmat = jnp.where(rel >= 0, jnp.exp(jnp.maximum(rel, 0.0) * log_gamma), 0.0)
        q_scale = jnp.exp((idx + 1.0) * log_gamma)
        k_scale = jnp.exp((valid_rows - 1.0 - idx) * log_gamma)
        for i in range(nb):
            q = q_ref[i, :, h * dk:(h + 1) * dk]
            k = k_ref[i, :, h * dk:(h + 1) * dk]
            v = v_ref[i, :, h * dv:(h + 1) * dv]
            if valid_rows < c:
                v = jnp.where(live, v, 0.0)
            q = q * cos2 + pltpu.roll(q, half, 1) * sin2
            k = (k * cos2 + pltpu.roll(k, half, 1) * sin2) * (dk ** -0.5)
            chains.append(dict(
                i=i, h=h, dmat=dmat, s_decay=math.exp(valid_rows * log_gamma),
                q_bf=q.astype(BF16), k_bf=k.astype(BF16), v_bf=v.astype(BF16),
                q_dec=(q * q_scale).astype(BF16), k_dec=(k * k_scale).astype(BF16)))

    for ch in chains:
        ch["inner"] = (_dot_nt(ch["q_bf"], ch["k_bf"]) * ch["dmat"]).astype(BF16)
    for ch in chains:
        s = s_scr[ch["i"], ch["h"]]
        ch["o"] = _dot(ch["q_dec"], s.astype(BF16)) + _dot(ch["inner"], ch["v_bf"])
        s_scr[ch["i"], ch["h"]] = s * ch["s_decay"] + _dot_tn(ch["k_dec"], ch["v_bf"])
    for ch in chains:
        ch["mu"] = jnp.mean(ch["o"], axis=-1, keepdims=True)
    for ch in chains:
        ch["oc"] = ch["o"] - ch["mu"]
        ch["var"] = jnp.mean(ch["oc"] * ch["oc"], axis=-1, keepdims=True)
    for ch in chains:
        i, h = ch["i"], ch["h"]
        o = ch["oc"] * lax.rsqrt(ch["var"] + NORM_EPS)
        o = o * gnw_ref[:, h * dv:(h + 1) * dv] + gnb_ref[:, h * dv:(h + 1) * dv]
        o = o * _silu(g_ref[i, :, h * dv:(h + 1) * dv])
        o_ref[i, :, h * dv:(h + 1) * dv] = o.astype(o_ref.dtype)

    @pl.when(n == pl.num_programs(1) - 1)
    def _():
        snew_ref[...] = s_scr[...]


def _ret(proj, col0, s0, gn_w, gn_b, *, chunk, valid_rows, nb, pos0, heads, dk, dv):
    b, t, _ = proj.shape
    width = heads * dv
    n_chunks = t // chunk
    blk0 = col0 // width
    half = dk // 2
    inv_freq = ROPE_BASE ** (-jnp.arange(half, dtype=F32) / half)
    inv_freq2 = jnp.concatenate([inv_freq, inv_freq]).reshape(1, dk)
    col_spec = lambda kk: pl.BlockSpec((nb, chunk, width), lambda i, n: (i, n, blk0 + kk))
    return pl.pallas_call(
        functools.partial(_ret_kernel, valid_rows=valid_rows, pos0=pos0, heads=heads, dk=dk, dv=dv),
        grid=(b // nb, n_chunks),
        in_specs=[
            col_spec(0), col_spec(1), col_spec(2), col_spec(3),
            pl.BlockSpec((nb, heads, dk, dv), lambda i, n: (i, 0, 0, 0)),
            pl.BlockSpec((1, dk), lambda i, n: (0, 0)),
            pl.BlockSpec((1, width), lambda i, n: (0, 0)),
            pl.BlockSpec((1, width), lambda i, n: (0, 0)),
        ],
        out_specs=[
            pl.BlockSpec((nb, chunk, width), lambda i, n: (i, n, 0)),
            pl.BlockSpec((nb, heads, dk, dv), lambda i, n: (i, 0, 0, 0)),
        ],
        out_shape=[
            jax.ShapeDtypeStruct((b, t, width), BF16),
            jax.ShapeDtypeStruct((b, heads, dk, dv), F32),
        ],
        scratch_shapes=[pltpu.VMEM((nb, heads, dk, dv), F32)],
        compiler_params=_params("parallel", "arbitrary"),
        name="ret_mixer",
    )(proj, proj, proj, proj, s0, inv_freq2, gn_w.reshape(1, width), gn_b.reshape(1, width))


def _trunk(x, mod, mod_f, rows_per_group, seq_rows, valid_rows, nb, pos0, to_seq, from_seq,
           conv_init, s_gdn0, s_ret0, w, tm, tf):
    heads, dk, dv = s_gdn0.shape[1:]
    x1, h2 = _ffn(x, mod, (0, 1, 2), w["norm_ffn1"], w["w1_gate"], w["w1_up"], w["w1_down"],
                  mod, (3, 4), w["norm_mix"], rows_per_group=rows_per_group, tm=tm, tf=tf,
                  emit_x=True, hn_dtype=BF16)
    conv_ch = heads * (2 * dk + dv)
    width = heads * dv
    ba0 = conv_ch + width
    tn = PROJ_COLS
    chunk = min(CHUNK, seq_rows)
    nb_gdn, nb_ret = nb
    gdn_args = dict(chunk=chunk, valid_rows=valid_rows, nb=nb_gdn, heads=heads, dk=dk, dv=dv)
    if conv_init is None:
        qkv, conv_new, proj_ba = _proj_conv(h2, w["w_in_t"], w["conv_w"], conv_ch, ba0, seq_rows, tm, tn)
        proj_ba = to_seq(proj_ba)
        skip = lambda j: conv_ch + j * tn + jnp.where(j > 0, 2 * heads, 0)
        rest = to_seq(_proj(h2, w["w_in_t"], skip, 5 * width, tm, tn))
        o_gdn, s_gdn = _gdn((to_seq(qkv), 0), (rest, 0), proj_ba, s_gdn0, w["a_log"], w["dt_bias"],
                            w["gdn_norm_w"], **gdn_args)
        ret_src, ret_col = rest, width
    else:
        proj_ba = to_seq(_proj(h2, w["w_in_t"], lambda j: ba0, LANES, tm, LANES))
        skip = lambda j: j * tn + jnp.where(j * tn >= ba0, 2 * heads, 0)
        proj = to_seq(_proj(h2, w["w_in_t"], skip, ba0 + 4 * width, tm, tn))
        o_gdn, s_gdn, conv_new = _gdn((proj, 0), (proj, conv_ch), proj_ba, s_gdn0, w["a_log"], w["dt_bias"],
                                      w["gdn_norm_w"], (conv_init, w["conv_w"]), **gdn_args)
        ret_src, ret_col = proj, ba0
    o_ret, s_ret = _ret(ret_src, ret_col, s_ret0, w["ret_gn_w"], w["ret_gn_b"],
                        chunk=chunk, valid_rows=valid_rows, nb=nb_ret, pos0=pos0, heads=heads, dk=dk, dv=dv)
    o_gdn = from_seq(o_gdn)
    o_ret = from_seq(o_ret)
    x2 = _outproj(x1, mod, 5, o_gdn, o_ret, w["w_out"], rows_per_group=rows_per_group, tm=min(tm, OUTPROJ_ROWS))
    (y,) = _ffn(x2, mod, (6, 7, 8), w["norm_ffn2"], w["w2_gate"], w["w2_up"], w["w2_down"],
                mod_f, (0, 1), w["norm_final"], rows_per_group=rows_per_group, tm=tm, tf=tf,
                emit_x=False, hn_dtype=F32)
    return y, s_gdn, conv_new[:, SUBLANES - (CONV_WIDTH - 1):, :], s_ret


def kernel(x_prompt, x_sample, state_gdn, state_conv, state_ret, c_prompt, c_sample, w_ada, b_ada, norm_ffn1, w1_gate, w1_up, w1_down, norm_mix, w_in, conv_w, a_log, dt_bias, gdn_norm_w, ret_gn_w, ret_gn_b, w_out, norm_ffn2, w2_gate, w2_up, w2_down, w_ada_final, b_ada_final, norm_final):
    bp, tp, d = x_prompt.shape
    bs, ts, _ = x_sample.shape
    depth, _, heads, dk, dv = state_gdn.shape
    assert depth == 1, "single-layer trunk"
    assert bs == ROW_TILE, "time-major sample rows must align adaLN vectors with row tiles"
    conv_ch = state_conv.shape[-1]
    width = heads * dv
    ba0 = conv_ch + width
    ret0 = ba0 + 2 * heads

    n_c = bs + bp
    n_c_pad = -(-n_c // SUBLANES) * SUBLANES
    c_all = jnp.concatenate([c_sample, c_prompt, jnp.zeros((n_c_pad - n_c, d), F32)], axis=0)
    ada = _ada_proj(c_all, w_ada[0], b_ada[0], tn=ADA_COLS)
    ada_f = _ada_proj(c_all, w_ada_final, b_ada_final, tn=ADA_COLS)
    mod_s, mod_p = ada, ada[bs:bs + bp].reshape(bp, 1, N_ADA * d)
    modf_s, modf_p = ada_f, ada_f[bs:bs + bp].reshape(bp, 1, 2 * d)

    assert ba0 % PROJ_COLS == 0 and ret0 % SUBLANES == 0 and w_in.shape[-1] == ret0 + 4 * width
    w = dict(norm_ffn1=norm_ffn1[0], w1_gate=w1_gate[0], w1_up=w1_up[0], w1_down=w1_down[0],
             norm_mix=norm_mix[0], w_in_t=w_in[0].T, conv_w=conv_w[0], a_log=a_log[0],
             dt_bias=dt_bias[0], gdn_norm_w=gdn_norm_w[0], ret_gn_w=ret_gn_w[0], ret_gn_b=ret_gn_b[0],
             w_out=w_out[0], norm_ffn2=norm_ffn2[0], w2_gate=w2_gate[0], w2_up=w2_up[0],
             w2_down=w2_down[0], norm_final=norm_final)

    zeros_state = jnp.zeros((bp, heads, dk, dv), F32)
    y_p, gdn_p, conv_p, ret_p = _trunk(
        x_prompt.reshape(bp * tp, d), mod_p, modf_p, tp, tp, min(CHUNK, tp), (bp, SEQS_PER_STEP_FULL), 0,
        lambda a: a.reshape(bp, tp, a.shape[-1]), lambda a: a.reshape(bp * tp, a.shape[-1]),
        None, zeros_state, zeros_state, w, tm=FFN_ROWS, tf=FFN_COLS)

    ts_pad = -(-ts // SUBLANES) * SUBLANES
    to_seq = lambda a: jnp.pad(a.reshape(ts, bs, a.shape[-1]).transpose(1, 0, 2),
                               ((0, 0), (0, ts_pad - ts), (0, 0)))
    from_seq = lambda a: a[:, :ts].transpose(1, 0, 2).reshape(ts * bs, a.shape[-1])
    conv_init = jnp.pad(state_conv[0], ((0, 0), (SUBLANES - (CONV_WIDTH - 1), 0), (0, 0)))
    y_s, gdn_s, conv_s, ret_s = _trunk(
        x_sample.transpose(1, 0, 2).reshape(ts * bs, d), mod_s, modf_s, None, ts_pad, ts,
        (SEQS_PER_STEP_SHORT, SEQS_PER_STEP_SHORT), PAST_LEN,
        to_seq, from_seq, conv_init, state_gdn[0], state_ret[0], w, tm=ts * bs, tf=FFN_COLS_SMALL)
    y_s = y_s.reshape(ts, bs, d).transpose(1, 0, 2)

    return (y_p.reshape(bp, tp, d), y_s, gdn_p[None], conv_p[None], ret_p[None],
            gdn_s[None], conv_s[None], ret_s[None])
```
